```python
import math
import jax, jax.numpy as jnp
from jax import lax
import numpy as np

D_MODEL = 1024
BATCH = 8
SEQ = 8192
DEPTH = 1

MEM_LEN = 256
EPS = 1e-6

ATT_HEAD_DIM = 64
ATT_WIDTH = D_MODEL // 2
ATT_HEADS = ATT_WIDTH // ATT_HEAD_DIM
DILATED_CONFIGS = ((128, 1), (512, 4), (2048, 16))
ATT_BLOCK = 128
ROPE_THETA = 500000.0
ROPE_DIM = ATT_HEAD_DIM // 4

M_WIDTH = D_MODEL - ATT_WIDTH
M_HEADS = 4
M_HEAD_DIM = M_WIDTH // M_HEADS
CONV_WIDTH = 4
M_CHUNK = 128

IN_WIDTH = 3 * ATT_WIDTH + 3 * M_WIDTH + 2 * M_HEADS
MIX_WIDTH = ATT_WIDTH + M_WIDTH

X_HEADS = 4
X_HEAD_DIM = 64
X_WIDTH = X_HEADS * X_HEAD_DIM

N_GROUPS = 4
EXPERTS_PER_GROUP = 4
N_EXPERTS = N_GROUPS * EXPERTS_PER_GROUP
TOP_K = 2
EXPERT_FF = 512
MOE_BLOCK = 128

kernel_name = 'hymba_longnet_mlstm_hmoe_block'


def rms_norm(x, g):
    xf = x.astype(jnp.float32)
    y = xf * lax.rsqrt(jnp.mean(xf * xf, axis=-1, keepdims=True) + EPS)
    return (y * g.astype(jnp.float32)).astype(x.dtype)


def partial_rotary(t, positions):
    half = ROPE_DIM // 2
    inv_freq = ROPE_THETA ** (-jnp.arange(0, ROPE_DIM, 2, dtype=jnp.float32) / ROPE_DIM)
    ang = positions.astype(jnp.float32)[..., None] * inv_freq
    cos = jnp.cos(ang)[:, :, None, :]
    sin = jnp.sin(ang)[:, :, None, :]
    tr = t[..., :ROPE_DIM].astype(jnp.float32)
    t1, t2 = tr[..., :half], tr[..., half:]
    rot = jnp.concatenate([t1 * cos - t2 * sin, t2 * cos + t1 * sin], axis=-1)
    return jnp.concatenate([rot.astype(t.dtype), t[..., ROPE_DIM:]], axis=-1)


def causal_window_attention(q, k, v, window, blk):
    N, L, H, Dh = q.shape
    assert window <= blk
    nb = -(-L // blk)
    Lp = nb * blk
    pad = ((0, 0), (0, Lp - L), (0, 0), (0, 0))
    qb, kb, vb = (jnp.pad(t, pad).reshape(N, nb, blk, H, Dh) for t in (q, k, v))

    def with_prev(t):
        prev = jnp.pad(t, ((0, 0), (1, 0), (0, 0), (0, 0), (0, 0)))[:, :-1]
        return jnp.concatenate([prev, t], axis=2)

    kk, vv = with_prev(kb), with_prev(vb)
    s = jnp.einsum('nbqhd,nbkhd->nbhqk', qb, kk).astype(jnp.float32) * (Dh ** -0.5)
    qi = jnp.arange(blk)[:, None] + blk
    ki = jnp.arange(2 * blk)[None, :]
    dist = qi - ki
    band = (dist >= 0) & (dist <= window)
    valid_key = (jnp.arange(nb)[:, None, None] * blk + ki[None] - blk) >= 0
    mask = band[None] & valid_key
    s = jnp.where(mask[None, :, None, :, :], s, -jnp.inf)
    lse = jax.nn.logsumexp(s, axis=-1)
    p = jnp.exp(s - lse[..., None])
    o = jnp.einsum('nbhqk,nbkhd->nbqhd', p.astype(v.dtype), vv).astype(jnp.float32)
    o = o.reshape(N, Lp, H, Dh)[:, :L]
    lse = lse.transpose(0, 1, 3, 2).reshape(N, Lp, H)[:, :L]
    return o, lse


def dilated_attention(q, k, v):
    B, S, H, Dh = q.shape
    outs, lses = [], []
    for window, d in DILATED_CONFIGS:
        L = S // d

        def to_sub(t):
            return t.reshape(B, L, d, H, Dh).transpose(0, 2, 1, 3, 4).reshape(B * d, L, H, Dh)

        o, lse = causal_window_attention(to_sub(q), to_sub(k), to_sub(v), window // d, ATT_BLOCK)
        outs.append(o.reshape(B, d, L, H, Dh).transpose(0, 2, 1, 3, 4).reshape(B, S, H, Dh))
        lses.append(lse.reshape(B, d, L, H).transpose(0, 2, 1, 3).reshape(B, S, H))
    w = jax.nn.softmax(jnp.stack(lses, axis=0), axis=0)
    return jnp.einsum('cbsh,cbshd->bshd', w, jnp.stack(outs, axis=0))


def mlstm_chunkwise(q, k, v, i_pre, f_pre):
    B, H, S, Dh = q.shape
    L = M_CHUNK
    nc = S // L
    q, k, v = (t.reshape(B, H, nc, L, Dh) for t in (q, k, v))
    ig = i_pre.reshape(B, H, nc, L)
    lf = jax.nn.log_sigmoid(f_pre).reshape(B, H, nc, L)
    b = jnp.cumsum(lf, axis=-1)
    b_last = b[..., -1]

    g = b_last[..., None] - b + ig
    m_loc = jnp.max(g, axis=-1)
    wk = jnp.exp(g - m_loc[..., None])
    C_loc = jnp.einsum('bhcl,bhcld,bhcle->bhcde', wk, k, v)
    n_loc = jnp.einsum('bhcl,bhcld->bhcd', wk, k)

    def step(carry, inp):
        C, n, m = carry
        Cl, nl, ml, bl = inp
        m_new = jnp.maximum(bl + m, ml)
        a = jnp.exp(bl + m - m_new)
        c = jnp.exp(ml - m_new)
        C_new = a[..., None, None] * C + c[..., None, None] * Cl
        n_new = a[..., None] * n + c[..., None] * nl
        return (C_new, n_new, m_new), (C, n, m)

    init = (jnp.zeros((B, H, Dh, Dh), jnp.float32), jnp.zeros((B, H, Dh), jnp.float32),
            jnp.zeros((B, H), jnp.float32))
    xs = (jnp.moveaxis(C_loc, 2, 0), jnp.moveaxis(n_loc, 2, 0),
          jnp.moveaxis(m_loc, 2, 0), jnp.moveaxis(b_last, 2, 0))
    _, (C_prev, n_prev, m_prev) = lax.scan(step, init, xs)
    C_prev = jnp.moveaxis(C_prev, 0, 2)
    n_prev = jnp.moveaxis(n_prev, 0, 2)
    m_prev = jnp.moveaxis(m_prev, 0, 2)

    causal = jnp.tril(jnp.ones((L, L), dtype=bool))
    Dlog = b[..., :, None] - b[..., None, :] + ig[..., None, :]
    Dlog = jnp.where(causal, Dlog, -jnp.inf)
    m_intra = jnp.max(Dlog, axis=-1)
    m_inter = b + m_prev[..., None]
    m_t = jnp.maximum(m_inter, m_intra)
    Dw = jnp.exp(Dlog - m_t[..., None])
    inter_w = jnp.exp(m_inter - m_t)
    s = jnp.einsum('bhcld,bhcsd->bhcls', q, k) * Dw
    num = jnp.einsum('bhcls,bhcse->bhcle', s, v) + \
        inter_w[..., None] * jnp.einsum('bhcld,bhcde->bhcle', q, C_prev)
    den = jnp.sum(s, axis=-1) + inter_w * jnp.einsum('bhcld,bhcd->bhcl', q, n_prev)
    h = num / jnp.maximum(jnp.abs(den), jnp.exp(-m_t))[..., None]
    return h.reshape(B, H, S, Dh)


def hybrid_mixer(h, positions, w_in, conv_w, conv_b, w_q_m, w_k_m, b_i, b_f, g_mhn, skip_m, w_out):
    B, S, _ = h.shape
    sizes = [ATT_WIDTH] * 3 + [M_WIDTH] * 3 + [M_HEADS, M_HEADS]
    splits = [int(c) for c in np.cumsum(sizes)[:-1]]
    z = h @ w_in
    aq, ak, av, mu, mv, mo, mi, mf = jnp.split(z, splits, axis=-1)

    aq = partial_rotary(aq.reshape(B, S, ATT_HEADS, ATT_HEAD_DIM), positions)
    ak = partial_rotary(ak.reshape(B, S, ATT_HEADS, ATT_HEAD_DIM), positions)
    av = av.reshape(B, S, ATT_HEADS, ATT_HEAD_DIM)
    y_att = dilated_attention(aq, ak, av).reshape(B, S, ATT_WIDTH).astype(h.dtype)

    f32 = jnp.float32
    mu_pad = jnp.pad(mu.astype(f32), ((0, 0), (CONV_WIDTH - 1, 0), (0, 0)))
    c = sum(mu_pad[:, j:j + S] * conv_w[j].astype(f32) for j in range(CONV_WIDTH)) + conv_b.astype(f32)
    c = jax.nn.silu(c)
    ch = c.reshape(B, S, M_HEADS, M_HEAD_DIM)
    mq = jnp.einsum('bshd,hde->bhse', ch, w_q_m.astype(f32))
    mk = jnp.einsum('bshd,hde->bhse', ch, w_k_m.astype(f32)) * (M_HEAD_DIM ** -0.5)
    mvh = mv.astype(f32).reshape(B, S, M_HEADS, M_HEAD_DIM).transpose(0, 2, 1, 3)
    ig = (mi.astype(f32) + b_i.astype(f32)).transpose(0, 2, 1)
    fg = (mf.astype(f32) + b_f.astype(f32)).transpose(0, 2, 1)
    hc = mlstm_chunkwise(mq, mk, mvh, ig, fg).transpose(0, 2, 1, 3)
    hn = hc * lax.rsqrt(jnp.mean(hc * hc, axis=-1, keepdims=True) + EPS)
    hn = hn.reshape(B, S, M_WIDTH) * g_mhn.astype(f32)
    y_m = jax.nn.sigmoid(mo.astype(f32)) * (hn + skip_m.astype(f32) * c)

    y = jnp.concatenate([y_att, y_m.astype(h.dtype)], axis=-1)
    return y @ w_out


def memory_cross_attention(h, mem_n, w_q, w_kv, w_o):
    B, S, _ = h.shape
    M = mem_n.shape[1]
    q = (h @ w_q).reshape(B, S, X_HEADS, X_HEAD_DIM)
    kv = (mem_n @ w_kv).reshape(B, M, 2, X_HEADS, X_HEAD_DIM)
    k, v = kv[:, :, 0], kv[:, :, 1]
    s = jnp.einsum('bshd,bmhd->bhsm', q, k).astype(jnp.float32) * (X_HEAD_DIM ** -0.5)
    p = jax.nn.softmax(s, axis=-1).astype(v.dtype)
    o = jnp.einsum('bhsm,bmhd->bshd', p, v).reshape(B, S, X_WIDTH)
    return o @ w_o


def hierarchical_moe(h, w_rg, b_rg, w_re, b_re, w1, w3, w2):
    N, D = h.shape
    f32 = jnp.float32
    glog = (h @ w_rg).astype(f32) + b_rg.astype(f32)
    gprob = jax.nn.softmax(glog, axis=-1)
    g_idx = jnp.argmax(glog, axis=-1)
    g_w = jnp.take_along_axis(gprob, g_idx[:, None], axis=-1)[:, 0]
    elog = ((h @ w_re).astype(f32) + b_re.astype(f32)).reshape(N, N_GROUPS, EXPERTS_PER_GROUP)
    elog = jnp.take_along_axis(elog, g_idx[:, None, None], axis=1)[:, 0]
    top_v, top_i = lax.top_k(elog, TOP_K)
    e_w = jax.nn.softmax(top_v, axis=-1) * g_w[:, None]
    e_id = g_idx[:, None] * EXPERTS_PER_GROUP + top_i

    A = N * TOP_K
    flat_e = e_id.reshape(A)
    flat_tok = jnp.repeat(jnp.arange(N, dtype=jnp.int32), TOP_K)
    flat_w = e_w.reshape(A)
    order = jnp.argsort(flat_e)
    se, stok, sw = flat_e[order], flat_tok[order], flat_w[order]
    counts = jnp.bincount(flat_e, length=N_EXPERTS)
    starts = jnp.cumsum(counts) - counts
    padded = ((counts + MOE_BLOCK - 1) // MOE_BLOCK) * MOE_BLOCK
    pends = jnp.cumsum(padded)
    pstarts = pends - padded
    dest = pstarts[se] + (jnp.arange(A) - starts[se])
    P = (-(-A // MOE_BLOCK) + N_EXPERTS) * MOE_BLOCK
    nblk = P // MOE_BLOCK
    row_tok = jnp.full((P,), N, dtype=jnp.int32).at[dest].set(stok)
    row_w = jnp.zeros((P,), f32).at[dest].set(sw)
    blk_e = jnp.minimum(jnp.searchsorted(pends, jnp.arange(nblk) * MOE_BLOCK, side='right'),
                        N_EXPERTS - 1)
    h_pad = jnp.concatenate([h, jnp.zeros((1, D), h.dtype)], axis=0)
    xb = h_pad[row_tok].reshape(nblk, MOE_BLOCK, D)

    def expert_block(args):
        xblk, e = args
        return (jax.nn.silu(xblk @ w1[e]) * (xblk @ w3[e])) @ w2[e]

    yb = lax.map(expert_block, (xb, blk_e)).reshape(P, D)
    out = jnp.zeros((N + 1, D), f32).at[row_tok].add(yb.astype(f32) * row_w[:, None])
    return out[:N]


def setup_inputs(seed: int = 0) -> dict:
    key = jax.random.key(seed)
    ks = jax.random.split(key, 32)
    f32 = jnp.float32
    L, D = DEPTH, D_MODEL

    def nrm(k, shape, scale):
        return jax.random.normal(k, shape, f32) * scale

    def gain(k, shape):
        return 1.0 + 0.02 * jax.random.normal(k, shape, f32)

    return {
        'x': nrm(ks[0], (BATCH, SEQ, D), 1.0),
        'mem': nrm(ks[1], (BATCH, MEM_LEN, D), 1.0),
        'positions': jnp.tile(jnp.arange(SEQ, dtype=jnp.int32)[None, :], (BATCH, 1)),
        'g_mix': gain(ks[2], (L, D)),
        'w_in': nrm(ks[3], (L, D, IN_WIDTH), D ** -0.5),
        'conv_w': nrm(ks[4], (L, CONV_WIDTH, M_WIDTH), CONV_WIDTH ** -0.5),
        'conv_b': nrm(ks[5], (L, M_WIDTH), 0.02),
        'w_q_m': nrm(ks[6], (L, M_HEADS, M_HEAD_DIM, M_HEAD_DIM), M_HEAD_DIM ** -0.5),
        'w_k_m': nrm(ks[7], (L, M_HEADS, M_HEAD_DIM, M_HEAD_DIM), M_HEAD_DIM ** -0.5),
        'b_i': nrm(ks[8], (L, M_HEADS), 0.1),
        'b_f': 3.0 + 3.0 * jax.random.uniform(ks[9], (L, M_HEADS), f32),
        'g_mhn': gain(ks[10], (L, M_WIDTH)),
        'skip_m': gain(ks[11], (L, M_WIDTH)),
        'w_out': nrm(ks[12], (L, MIX_WIDTH, D), MIX_WIDTH ** -0.5),
        'g_cross': gain(ks[13], (L, D)),
        'g_mem': gain(ks[14], (L, D)),
        'w_q_x': nrm(ks[15], (L, D, X_WIDTH), D ** -0.5),
        'w_kv_x': nrm(ks[16], (L, D, 2 * X_WIDTH), D ** -0.5),
        'w_o_x': nrm(ks[17], (L, X_WIDTH, D), X_WIDTH ** -0.5),
        'g_ffn': gain(ks[18], (L, D)),
        'w_router_g': nrm(ks[19], (L, D, N_GROUPS), D ** -0.5),
        'b_router_g': nrm(ks[20], (L, N_GROUPS), 0.01),
        'w_router_e': nrm(ks[21], (L, D, N_EXPERTS), D ** -0.5),
        'b_router_e': nrm(ks[22], (L, N_EXPERTS), 0.01),
        'w1': nrm(ks[23], (L, N_EXPERTS, D, EXPERT_FF), D ** -0.5),
        'w3': nrm(ks[24], (L, N_EXPERTS, D, EXPERT_FF), D ** -0.5),
        'w2': nrm(ks[25], (L, N_EXPERTS, EXPERT_FF, D), EXPERT_FF ** -0.5),
        'g_final': gain(ks[26], (D,)),
    }


def reference(x, mem, positions, g_mix, w_in, conv_w, conv_b, w_q_m, w_k_m, b_i, b_f, g_mhn,
              skip_m, w_out, g_cross, g_mem, w_q_x, w_kv_x, w_o_x, g_ffn, w_router_g,
              b_router_g, w_router_e, b_router_e, w1, w3, w2, g_final):
    B, S, D = x.shape
    for l in range(DEPTH):
        h = rms_norm(x, g_mix[l])
        x = x + hybrid_mixer(h, positions, w_in[l], conv_w[l], conv_b[l], w_q_m[l], w_k_m[l],
                             b_i[l], b_f[l], g_mhn[l], skip_m[l], w_out[l]).astype(x.dtype)
        h = rms_norm(x, g_cross[l])
        x = x + memory_cross_attention(h, rms_norm(mem, g_mem[l]), w_q_x[l], w_kv_x[l],
                                       w_o_x[l]).astype(x.dtype)
        h = rms_norm(x, g_ffn[l]).reshape(B * S, D)
        x = x + hierarchical_moe(h, w_router_g[l], b_router_g[l], w_router_e[l], b_router_e[l],
                                 w1[l], w3[l], w2[l]).reshape(B, S, D).astype(x.dtype)
    return rms_norm(x, g_final)
```

```python
import functools

import jax
import jax.numpy as jnp
import numpy as np
from jax import lax
from jax.experimental import pallas as pl
from jax.experimental.pallas import tpu as pltpu

F32 = jnp.float32
BF16 = jnp.bfloat16

EPS = 1e-6
LANES = 128
ATT_HEAD_DIM = 64
ATT_WIDTH = 512
DILATED_CONFIGS = ((128, 1), (512, 4), (2048, 16))
ATT_BLOCK = 128
ROPE_THETA = 500000.0
ROPE_DIM = ATT_HEAD_DIM // 4
M_WIDTH = 512
M_HEADS = 4
M_HEAD_DIM = 128
CONV_WIDTH = 4
M_CHUNK = 128
X_HEADS = 4
X_HEAD_DIM = 64
X_WIDTH = X_HEADS * X_HEAD_DIM
N_GROUPS = 4
EXPERTS_PER_GROUP = 4
N_EXPERTS = 16
TOP_K = 2
EXPERT_FF = 512
MOE_ROWS = 256
VMEM_LIMIT = 48 * 1024 * 1024

NEG_INF = float("-inf")


def _cparams(sem):
    return pltpu.CompilerParams(dimension_semantics=sem, vmem_limit_bytes=VMEM_LIMIT)


def _rms(x, g):
    return x * lax.rsqrt(jnp.mean(x * x, axis=-1, keepdims=True) + EPS) * g


def _dot(a, b):
    return jnp.dot(a, b, preferred_element_type=F32)


def _dot_nt(a, b):
    return lax.dot_general(a, b, (((1,), (1,)), ((), ())), preferred_element_type=F32)


def _dot_tn(a, b):
    return lax.dot_general(a, b, (((0,), (0,)), ((), ())), preferred_element_type=F32)


def _split_dot(a, b):
    hi = a.astype(BF16)
    lo = (a - hi.astype(F32)).astype(BF16)
    return _dot(hi, b) + _dot(lo, b)


def _log_sigmoid(x):
    return jnp.minimum(x, 0.0) - jnp.log(1.0 + jnp.exp(-jnp.abs(x)))


def _mem_kv_kernel(mem_ref, g_ref, w_ref, k_ref, v_ref):
    h = _rms(mem_ref[0], g_ref[...]).astype(BF16)
    kv = _dot(h, w_ref[...])
    k_ref[0] = kv[:, :X_WIDTH].astype(BF16)
    v_ref[0] = kv[:, X_WIDTH:].astype(BF16)


def _mem_kv(mem, g_mem, w_kv):
    B, M, D = mem.shape
    return pl.pallas_call(
        _mem_kv_kernel,
        grid=(B,),
        in_specs=[pl.BlockSpec((1, M, D), lambda b: (b, 0, 0)),
                  pl.BlockSpec((1, D), lambda b: (0, 0)),
                  pl.BlockSpec((D, 2 * X_WIDTH), lambda b: (0, 0))],
        out_specs=[pl.BlockSpec((1, M, X_WIDTH), lambda b: (b, 0, 0)),
                   pl.BlockSpec((1, M, X_WIDTH), lambda b: (b, 0, 0))],
        out_shape=[jax.ShapeDtypeStruct((B, M, X_WIDTH), BF16)] * 2,
        compiler_params=_cparams(("arbitrary",)),
        name="mem_kv",
    )(mem, g_mem.reshape(1, D), w_kv.astype(BF16))


def _in_proj_kernel(x_ref, pos_ref, g_ref, wqkv_ref, wm_ref, wg_ref, wgt_ref, bg_ref, bgt_ref,
                    invf_ref, sgn_ref, q_ref, k_ref, v_ref, mu_ref, mv_ref, mo_ref, gt_ref, gtt_ref):
    hb = _rms(x_ref[...], g_ref[...]).astype(BF16)
    qkv = _dot(hb, wqkv_ref[...])
    ang = pos_ref[...] * invf_ref[...]
    cos = jnp.cos(ang)
    sin = jnp.sin(ang)
    s_lo = sin * sgn_ref[0:1, :]
    s_hi = sin * sgn_ref[1:2, :]
    half = ROPE_DIM // 2
    for which, out_ref in ((0, q_ref), (1, k_ref)):
        for g in range(ATT_WIDTH // LANES):
            t = qkv[:, which * ATT_WIDTH + g * LANES: which * ATT_WIDTH + (g + 1) * LANES]
            r = t * cos + pltpu.roll(t, LANES - half, 1) * s_lo + pltpu.roll(t, half, 1) * s_hi
            out_ref[:, g * LANES:(g + 1) * LANES] = r.astype(BF16)
    v_ref[...] = qkv[:, 2 * ATT_WIDTH:].astype(BF16)
    mm = _dot(hb, wm_ref[...])
    mu_ref[...] = mm[:, :M_WIDTH]
    mv_ref[...] = mm[:, M_WIDTH:2 * M_WIDTH].astype(BF16)
    mo_ref[...] = mm[:, 2 * M_WIDTH:]
    gt_ref[...] = _dot(hb, wg_ref[...]) + bg_ref[...]
    gtt_ref[...] = _dot_nt(wgt_ref[...], hb) + bgt_ref[...]


def _in_proj(x2d, pos, g_mix, w_in, b_i, b_f, tm):
    N, D = x2d.shape
    A = ATT_WIDTH
    wq = w_in[:, :A] * (ATT_HEAD_DIM ** -0.5)
    wqkv = jnp.concatenate([wq, w_in[:, A:3 * A]], axis=1).astype(BF16)
    wm = w_in[:, 3 * A:3 * A + 3 * M_WIDTH].astype(BF16)
    wgates = w_in[:, 3 * A + 3 * M_WIDTH:]
    wg = jnp.pad(wgates, ((0, 0), (0, LANES - 2 * M_HEADS))).astype(BF16)
    wgt = wgates.T.astype(BF16)
    bias = jnp.concatenate([b_i, b_f]).astype(F32)
    bg = jnp.pad(bias, (0, LANES - 2 * M_HEADS)).reshape(1, LANES)
    bgt = bias.reshape(2 * M_HEADS, 1)
    j = np.arange(LANES) % ATT_HEAD_DIM
    inv_freq = ROPE_THETA ** (-jnp.arange(0, ROPE_DIM, 2, dtype=F32) / ROPE_DIM)
    invf = jnp.where(j < ROPE_DIM, inv_freq[j % (ROPE_DIM // 2)], 0.0).reshape(1, LANES).astype(F32)
    sgn = np.zeros((8, LANES), np.float32)
    sgn[0] = np.where(j < ROPE_DIM // 2, -1.0, 0.0)
    sgn[1] = np.where((j >= ROPE_DIM // 2) & (j < ROPE_DIM), 1.0, 0.0)
    const = lambda shape: pl.BlockSpec(shape, lambda i: (0,) * len(shape))
    row = lambda w: pl.BlockSpec((tm, w), lambda i: (i, 0))
    return pl.pallas_call(
        _in_proj_kernel,
        grid=(N // tm,),
        in_specs=[row(D), row(1), const((1, D)), const((D, 3 * A)), const((D, 3 * M_WIDTH)),
                  const((D, LANES)), const((2 * M_HEADS, D)), const((1, LANES)),
                  const((2 * M_HEADS, 1)), const((1, LANES)), const((8, LANES))],
        out_specs=[row(A), row(A), row(A), row(M_WIDTH), row(M_WIDTH), row(M_WIDTH), row(LANES),
                   pl.BlockSpec((2 * M_HEADS, tm), lambda i: (0, i))],
        out_shape=[jax.ShapeDtypeStruct((N, A), BF16)] * 3
        + [jax.ShapeDtypeStruct((N, M_WIDTH), F32), jax.ShapeDtypeStruct((N, M_WIDTH), BF16),
           jax.ShapeDtypeStruct((N, M_WIDTH), F32), jax.ShapeDtypeStruct((N, LANES), F32),
           jax.ShapeDtypeStruct((2 * M_HEADS, N), F32)],
        compiler_params=_cparams(("arbitrary",)),
        name="in_proj",
    )(x2d, pos, g_mix.reshape(1, D), wqkv, wm, wg, wgt, bg, bgt, invf, jnp.asarray(sgn))


def _attn_kernel(q_ref, kc_ref, kp_ref, vc_ref, vp_ref, o_ref, l_ref, kbuf, vbuf, *, qb):
    blk = ATT_BLOCK
    j = pl.program_id(2)
    kbuf[0:blk, :] = kp_ref[0]
    kbuf[blk:blk + qb, :] = kc_ref[0]
    vbuf[0:blk, :] = vp_ref[0]
    vbuf[blk:blk + qb, :] = vc_ref[0]
    qi = lax.broadcasted_iota(jnp.int32, (blk, 2 * blk), 0)
    ki = lax.broadcasted_iota(jnp.int32, (blk, 2 * blk), 1)
    dist = qi + blk - ki
    band = (dist >= 0) & (dist <= blk)
    band_first = band & ((ki >= blk) | (j > 0))
    lane = lax.broadcasted_iota(jnp.int32, (1, LANES), 1)
    lane_full = lax.broadcasted_iota(jnp.int32, (blk, LANES), 1)
    low = lane < ATT_HEAD_DIM
    for sub in range(qb // blk):
        mask = band_first if sub == 0 else band
        rows = slice(sub * blk, (sub + 1) * blk)
        lse_all = jnp.zeros((blk, LANES), F32)
        for g in range(ATT_WIDTH // LANES):
            cols = slice(g * LANES, (g + 1) * LANES)
            qs = q_ref[0, rows, cols]
            ks = kbuf[sub * blk:(sub + 2) * blk, cols]
            vs = vbuf[sub * blk:(sub + 2) * blk, cols]
            acc = jnp.zeros((blk, LANES), F32)
            for hh in range(2):
                hm = low if hh == 0 else jnp.logical_not(low)
                s = _dot_nt(jnp.where(hm, qs, jnp.zeros_like(qs)), ks)
                s = jnp.where(mask, s, NEG_INF)
                m = jnp.max(s, axis=-1, keepdims=True)
                p = jnp.exp(s - m)
                l = jnp.sum(p, axis=-1, keepdims=True)
                pv = _dot(p.astype(BF16), jnp.where(hm, vs, jnp.zeros_like(vs)))
                acc = acc + pv * (1.0 / l)
                lse_all = jnp.where(lane_full == 2 * g + hh, m + jnp.log(l), lse_all)
            o_ref[0, rows, cols] = acc
        l_ref[0, rows, :] = lse_all


def _attention_config(q, k, v, d):
    B, S, W = q.shape
    L = S // d
    qb = min(512, L)
    nsub = qb // ATT_BLOCK
    qv, kv, vv = (t.reshape(B, L, d * W) for t in (q, k, v))
    cur = pl.BlockSpec((1, qb, W), lambda b, r, j: (b, j, r))
    prev = pl.BlockSpec((1, ATT_BLOCK, W), lambda b, r, j: (b, jnp.maximum(j * nsub - 1, 0), r))
    o, lse = pl.pallas_call(
        functools.partial(_attn_kernel, qb=qb),
        grid=(B, d, L // qb),
        in_specs=[cur, cur, prev, cur, prev],
        out_specs=[pl.BlockSpec((1, qb, W), lambda b, r, j: (b, j, r)),
                   pl.BlockSpec((1, qb, LANES), lambda b, r, j: (b, j, r))],
        out_shape=[jax.ShapeDtypeStruct((B, L, d * W), F32),
                   jax.ShapeDtypeStruct((B, L, d * LANES), F32)],
        scratch_shapes=[pltpu.VMEM((qb + ATT_BLOCK, W), BF16), pltpu.VMEM((qb + ATT_BLOCK, W), BF16)],
        compiler_params=_cparams(("arbitrary", "arbitrary", "arbitrary")),
        name=f"attention_d{d}",
    )(qv, kv, kv, vv, vv)
    return o.reshape(B, S, W), lse.reshape(B, S, LANES)


def _mlstm_kernel(mu_ref, mv_ref, mo_ref, gt_ref, gtt_ref, cw_ref, cb_ref, wq_ref, wk_ref, gn_ref,
                  sk_ref, y_ref, c_scr, n_scr, m_scr, ext_scr):
    L = M_CHUNK
    H = M_HEADS

    @pl.when(pl.program_id(1) == 0)
    def _():
        c_scr[...] = jnp.zeros_like(c_scr)
        n_scr[...] = jnp.zeros_like(n_scr)
        m_scr[...] = jnp.zeros_like(m_scr)
        ext_scr[0:8, :] = jnp.zeros((8, M_WIDTH), F32)

    mu = mu_ref[0]
    ext_scr[8:8 + L, :] = mu
    conv = cb_ref[...]
    for jj in range(CONV_WIDTH):
        conv = conv + ext_scr[8 - (CONV_WIDTH - 1) + jj: 8 - (CONV_WIDTH - 1) + jj + L, :] * cw_ref[jj:jj + 1, :]
    ext_scr[0:8, :] = mu[L - 8:, :]
    cs = conv * jax.nn.sigmoid(conv)
    cb16 = cs.astype(BF16)

    ri = lax.broadcasted_iota(jnp.int32, (L, L), 0)
    ci = lax.broadcasted_iota(jnp.int32, (L, L), 1)
    causal = ci <= ri
    tril = jnp.where(causal, 1.0, 0.0).astype(BF16)
    triu = jnp.where(ri <= ci, 1.0, 0.0).astype(BF16)

    gt = gt_ref[0]
    gtt = gtt_ref[...]
    lf_c = _log_sigmoid(gt)
    lf_r = _log_sigmoid(gtt)
    hi_c = lf_c.astype(BF16)
    lo_c = (lf_c - hi_c.astype(F32)).astype(BF16)
    b_cols = _dot(tril, hi_c) + _dot(tril, lo_c)
    b_rows = _split_dot(lf_r, triu)

    scale = M_HEAD_DIM ** -0.5
    for hd in range(H):
        cols = slice(hd * M_HEAD_DIM, (hd + 1) * M_HEAD_DIM)
        ch = cb16[:, cols]
        q = _dot(ch, wq_ref[hd])
        k = _dot(ch, wk_ref[hd]) * scale
        qb = q.astype(BF16)
        kb = k.astype(BF16)
        v = mv_ref[0, :, cols]
        ig_c = gt[:, hd:hd + 1]
        b_c = b_cols[:, H + hd:H + hd + 1]
        ig_r = gtt[hd:hd + 1, :]
        b_r = b_rows[H + hd:H + hd + 1, :]
        b_last = b_r[:, L - 1:L]
        m_prev = m_scr[hd:hd + 1, 0:1]
        c_prev = c_scr[hd]
        n_prev = n_scr[hd:hd + 1, :]

        dlog = jnp.where(causal, b_c - b_r + ig_r, NEG_INF)
        m_intra = jnp.max(dlog, axis=-1, keepdims=True)
        m_inter = b_c + m_prev
        m_t = jnp.maximum(m_inter, m_intra)
        dw = jnp.exp(dlog - m_t)
        inter_w = jnp.exp(m_inter - m_t)
        s = _dot_nt(qb, kb) * dw
        num = _dot(s.astype(BF16), v) + inter_w * _dot(qb, c_prev.astype(BF16))
        den = jnp.sum(s, axis=-1, keepdims=True) + inter_w * jnp.sum(q * n_prev, axis=-1, keepdims=True)
        hh = num / jnp.maximum(jnp.abs(den), jnp.exp(-m_t))
        hn = hh * lax.rsqrt(jnp.mean(hh * hh, axis=-1, keepdims=True) + EPS) * gn_ref[:, cols]
        y = jax.nn.sigmoid(mo_ref[0, :, cols]) * (hn + sk_ref[:, cols] * cs[:, cols])
        y_ref[0, :, cols] = y.astype(BF16)

        gcol = b_last - b_c + ig_c
        m_loc = jnp.max(gcol, axis=0, keepdims=True)
        kw = k * jnp.exp(gcol - m_loc)
        c_loc = _dot_tn(kw.astype(BF16), v)
        n_loc = jnp.sum(kw, axis=0, keepdims=True)
        m_new = jnp.maximum(b_last + m_prev, m_loc)
        a = jnp.exp(b_last + m_prev - m_new)
        cc = jnp.exp(m_loc - m_new)
        c_scr[hd] = a * c_prev + cc * c_loc
        n_scr[hd:hd + 1, :] = a * n_prev + cc * n_loc
        m_scr[hd:hd + 1, :] = jnp.broadcast_to(m_new, (1, LANES))


def _mlstm(mu, mv, mo, gt, gtt, conv_w, conv_b, w_q_m, w_k_m, g_mhn, skip_m):
    B, S, W = mu.shape
    L = M_CHUNK
    nc = S // L
    tok = pl.BlockSpec((1, L, W), lambda b, c: (b, c, 0))
    const = lambda shape: pl.BlockSpec(shape, lambda b, c: (0,) * len(shape))
    return pl.pallas_call(
        _mlstm_kernel,
        grid=(B, nc),
        in_specs=[tok, tok, tok,
                  pl.BlockSpec((1, L, LANES), lambda b, c: (b, c, 0)),
                  pl.BlockSpec((2 * M_HEADS, L), lambda b, c: (0, b * nc + c)),
                  const((CONV_WIDTH, W)), const((1, W)),
                  const((M_HEADS, M_HEAD_DIM, M_HEAD_DIM)), const((M_HEADS, M_HEAD_DIM, M_HEAD_DIM)),
                  const((1, W)), const((1, W))],
        out_specs=tok,
        out_shape=jax.ShapeDtypeStruct((B, S, W), BF16),
        scratch_shapes=[pltpu.VMEM((M_HEADS, M_HEAD_DIM, M_HEAD_DIM), F32),
                        pltpu.VMEM((8, M_HEAD_DIM), F32),
                        pltpu.VMEM((8, LANES), F32),
                        pltpu.VMEM((8 + L, W), F32)],
        compiler_params=_cparams(("arbitrary", "arbitrary")),
        name="mlstm",
    )(mu, mv, mo, gt, gtt, conv_w.astype(F32), conv_b.reshape(1, W).astype(F32),
      w_q_m.astype(BF16), w_k_m.astype(BF16), g_mhn.reshape(1, W).astype(F32),
      skip_m.reshape(1, W).astype(F32))


def _post_mix_kernel(x_ref, o1_ref, o2_ref, o3_ref, l1_ref, l2_ref, l3_ref, ym_ref, wo_ref, gc_ref,
                     wqx_ref, km_ref, vm_ref, wox_ref, gf_ref, wr_ref, br_ref,
                     x2_ref, h3_ref, route_ref, cnt_ref, run_scr, *, tm):
    @pl.when(pl.program_id(0) == 0)
    def _():
        run_scr[...] = jnp.zeros_like(run_scr)

    lane1 = lax.broadcasted_iota(jnp.int32, (1, LANES), 1)
    low = lane1 < ATT_HEAD_DIM

    l1, l2, l3 = l1_ref[...], l2_ref[...], l3_ref[...]
    mx = jnp.maximum(jnp.maximum(l1, l2), l3)
    e1, e2, e3 = jnp.exp(l1 - mx), jnp.exp(l2 - mx), jnp.exp(l3 - mx)
    inv = 1.0 / (e1 + e2 + e3)
    wts = (e1 * inv, e2 * inv, e3 * inv)
    o_refs = (o1_ref, o2_ref, o3_ref)
    mix = _dot(ym_ref[...], wo_ref[ATT_WIDTH:, :])
    for g in range(ATT_WIDTH // LANES):
        cols = slice(g * LANES, (g + 1) * LANES)
        ya = jnp.zeros((tm, LANES), F32)
        for c in range(3):
            w = jnp.where(low, wts[c][:, 2 * g:2 * g + 1], wts[c][:, 2 * g + 1:2 * g + 2])
            ya = ya + w * o_refs[c][:, cols]
        mix = mix + _dot(ya.astype(BF16), wo_ref[g * LANES:(g + 1) * LANES, :])
    x1 = x_ref[...] + mix

    h2 = _rms(x1, gc_ref[...]).astype(BF16)
    qx = _dot(h2, wqx_ref[...]).astype(BF16)
    ox = []
    for g in range(X_WIDTH // LANES):
        cols = slice(g * LANES, (g + 1) * LANES)
        qs = qx[:, cols]
        ks = km_ref[0, :, cols]
        vs = vm_ref[0, :, cols]
        acc = jnp.zeros((tm, LANES), F32)
        for hh in range(2):
            hm = low if hh == 0 else jnp.logical_not(low)
            s = _dot_nt(jnp.where(hm, qs, jnp.zeros_like(qs)), ks)
            m = jnp.max(s, axis=-1, keepdims=True)
            p = jnp.exp(s - m)
            l = jnp.sum(p, axis=-1, keepdims=True)
            acc = acc + _dot(p.astype(BF16), jnp.where(hm, vs, jnp.zeros_like(vs))) * (1.0 / l)
        ox.append(acc.astype(BF16))
    x2 = x1
    for g in range(X_WIDTH // LANES):
        x2 = x2 + _dot(ox[g], wox_ref[g * LANES:(g + 1) * LANES, :])
    x2_ref[...] = x2

    h3 = _rms(x2, gf_ref[...])
    h3_ref[...] = h3
    logits = jnp.dot(h3, wr_ref[...], preferred_element_type=F32,
                     precision=lax.Precision.HIGHEST) + br_ref[...]
    lane = lax.broadcasted_iota(jnp.int32, (tm, LANES), 1)
    first_e = N_GROUPS
    gmask = lane < N_GROUPS
    gl = jnp.where(gmask, logits, NEG_INF)
    gmax = jnp.max(gl, axis=-1, keepdims=True)
    gidx = jnp.min(jnp.where(gl == gmax, lane, LANES), axis=-1, keepdims=True)
    gsum = jnp.sum(jnp.where(gmask, jnp.exp(gl - gmax), 0.0), axis=-1, keepdims=True)
    g_w = 1.0 / gsum
    lo_lane = first_e + gidx * EXPERTS_PER_GROUP
    emask = (lane >= lo_lane) & (lane < lo_lane + EXPERTS_PER_GROUP)
    el = jnp.where(emask, logits, NEG_INF)
    t1 = jnp.max(el, axis=-1, keepdims=True)
    i1 = jnp.min(jnp.where(el == t1, lane, LANES), axis=-1, keepdims=True)
    el2 = jnp.where(lane == i1, NEG_INF, el)
    t2 = jnp.max(el2, axis=-1, keepdims=True)
    i2 = jnp.min(jnp.where(el2 == t2, lane, LANES), axis=-1, keepdims=True)
    ee = jnp.exp(t2 - t1)
    w1 = g_w / (1.0 + ee)
    w2 = w1 * ee

    hit1 = lane == i1
    hit2 = lane == i2
    cnt = jnp.where(hit1 | hit2, 1.0, 0.0)
    ri = lax.broadcasted_iota(jnp.int32, (tm, tm), 0)
    ci = lax.broadcasted_iota(jnp.int32, (tm, tm), 1)
    before = jnp.where(ci < ri, 1.0, 0.0).astype(BF16)
    prefix = _dot(before, cnt.astype(BF16)) + run_scr[0:1, :]
    r1 = jnp.sum(jnp.where(hit1, prefix, 0.0), axis=-1, keepdims=True)
    r2 = jnp.sum(jnp.where(hit2, prefix, 0.0), axis=-1, keepdims=True)
    total = run_scr[0:1, :] + jnp.sum(cnt, axis=0, keepdims=True)
    run_scr[...] = jnp.broadcast_to(total, run_scr.shape)
    cnt_ref[...] = jnp.broadcast_to(total, cnt_ref.shape)

    fields = ((i1 - first_e).astype(F32), (i2 - first_e).astype(F32), w1, w2, r1, r2)
    route = jnp.zeros((tm, LANES), F32)
    for idx, val in enumerate(fields):
        route = jnp.where(lane == idx, val, route)
    route_ref[...] = route


def _post_mix(x2d, outs, lses, y_m, w_out, g_cross, w_q_x, k_mem, v_mem, w_o_x, g_ffn,
              w_router_g, b_router_g, w_router_e, b_router_e, tm, tiles_per_batch):
    N, D = x2d.shape
    M = k_mem.shape[1]
    wr = jnp.pad(jnp.concatenate([w_router_g, w_router_e], axis=1).astype(F32),
                 ((0, 0), (0, LANES - N_GROUPS - N_EXPERTS)))
    br = jnp.pad(jnp.concatenate([b_router_g, b_router_e]).astype(F32),
                 (0, LANES - N_GROUPS - N_EXPERTS)).reshape(1, LANES)
    const = lambda shape: pl.BlockSpec(shape, lambda i: (0,) * len(shape))
    row = lambda w: pl.BlockSpec((tm, w), lambda i: (i, 0))
    memspec = pl.BlockSpec((1, M, X_WIDTH), lambda i: (i // tiles_per_batch, 0, 0))
    return pl.pallas_call(
        functools.partial(_post_mix_kernel, tm=tm),
        grid=(N // tm,),
        in_specs=[row(D), row(ATT_WIDTH), row(ATT_WIDTH), row(ATT_WIDTH), row(LANES), row(LANES),
                  row(LANES), row(M_WIDTH), const((D, D)), const((1, D)), const((D, X_WIDTH)),
                  memspec, memspec, const((X_WIDTH, D)), const((1, D)), const((D, LANES)),
                  const((1, LANES))],
        out_specs=[row(D), row(D), row(LANES), const((8, LANES))],
        out_shape=[jax.ShapeDtypeStruct((N, D), F32), jax.ShapeDtypeStruct((N, D), F32),
                   jax.ShapeDtypeStruct((N, LANES), F32), jax.ShapeDtypeStruct((8, LANES), F32)],
        scratch_shapes=[pltpu.VMEM((8, LANES), F32)],
        compiler_params=_cparams(("arbitrary",)),
        name="post_mix",
    )(x2d, *outs, *lses, y_m, w_out.astype(BF16), g_cross.reshape(1, D),
      (w_q_x * (X_HEAD_DIM ** -0.5)).astype(BF16), k_mem, v_mem, w_o_x.astype(BF16),
      g_ffn.reshape(1, D), wr, br)


def _row_copy(src, src_row, dst, dst_row, sem):
    return pltpu.make_async_copy(src.at[pl.ds(src_row, 1), :], dst.at[pl.ds(dst_row, 1), :], sem)


def _dispatch_kernel(dest_ref, h_ref, xs_in_ref, xs_ref, sem, *, tm):
    del xs_in_ref

    def start(t, carry):
        for k in range(TOP_K):
            _row_copy(h_ref, t, xs_ref, dest_ref[0, 0, TOP_K * t + k], sem).start()
        return carry

    def wait(t, carry):
        for k in range(TOP_K):
            _row_copy(h_ref, t, xs_ref, dest_ref[0, 0, TOP_K * t + k], sem).wait()
        return carry

    lax.fori_loop(0, tm, start, 0)
    lax.fori_loop(0, tm, wait, 0)


def _dispatch(h3, dest, n_rows, tm):
    N, D = h3.shape
    xs0 = jnp.zeros((n_rows, D), F32)
    return pl.pallas_call(
        functools.partial(_dispatch_kernel, tm=tm),
        grid=(N // tm,),
        in_specs=[pl.BlockSpec((1, 1, TOP_K * tm), lambda i: (i, 0, 0), memory_space=pltpu.SMEM),
                  pl.BlockSpec((tm, D), lambda i: (i, 0)),
                  pl.BlockSpec(memory_space=pl.ANY)],
        out_specs=pl.BlockSpec(memory_space=pl.ANY),
        out_shape=jax.ShapeDtypeStruct((n_rows, D), F32),
        scratch_shapes=[pltpu.SemaphoreType.DMA(())],
        input_output_aliases={2: 0},
        compiler_params=_cparams(("arbitrary",)),
        name="moe_dispatch",
    )(dest.reshape(N // tm, 1, TOP_K * tm), h3, xs0)


def _expert_kernel(blk_e_ref, nvalid_ref, x_ref, w1_ref, w3_ref, w2_ref, y_ref):
    del blk_e_ref

    @pl.when(pl.program_id(0) < nvalid_ref[0])
    def _():
        xb = x_ref[...].astype(BF16)
        a = _dot(xb, w1_ref[0])
        b = _dot(xb, w3_ref[0])
        hidden = (a * jax.nn.sigmoid(a) * b).astype(BF16)
        y_ref[...] = _dot(hidden, w2_ref[0])

    @pl.when(pl.program_id(0) >= nvalid_ref[0])
    def _():
        y_ref[...] = jnp.zeros_like(y_ref)


def _experts(xs, blk_e, nvalid, w1, w3, w2):
    P, D = xs.shape
    nblk = P // MOE_ROWS
    F = w1.shape[-1]
    grid_spec = pltpu.PrefetchScalarGridSpec(
        num_scalar_prefetch=2,
        grid=(nblk,),
        in_specs=[pl.BlockSpec((MOE_ROWS, D), lambda i, be, nv: (i, 0)),
                  pl.BlockSpec((1, D, F), lambda i, be, nv: (be[i], 0, 0)),
                  pl.BlockSpec((1, D, F), lambda i, be, nv: (be[i], 0, 0)),
                  pl.BlockSpec((1, F, D), lambda i, be, nv: (be[i], 0, 0))],
        out_specs=pl.BlockSpec((MOE_ROWS, D), lambda i, be, nv: (i, 0)),
    )
    return pl.pallas_call(
        _expert_kernel,
        grid_spec=grid_spec,
        out_shape=jax.ShapeDtypeStruct((P, D), F32),
        compiler_params=_cparams(("arbitrary",)),
        name="moe_experts",
    )(blk_e, nvalid, xs, w1.astype(BF16), w3.astype(BF16), w2.astype(BF16))


def _combine_kernel(dest_ref, x_ref, w_ref, g_ref, ys_ref, o_ref, buf, sem, *, tm):
    def start(t, carry):
        for k in range(TOP_K):
            _row_copy(ys_ref, dest_ref[0, 0, TOP_K * t + k], buf.at[k], t, sem).start()
        return carry

    def wait(t, carry):
        for k in range(TOP_K):
            _row_copy(ys_ref, dest_ref[0, 0, TOP_K * t + k], buf.at[k], t, sem).wait()
        return carry

    lax.fori_loop(0, tm, start, 0)
    lax.fori_loop(0, tm, wait, 0)
    w = w_ref[...]
    x = x_ref[...] + w[:, 2:3] * buf[0] + w[:, 3:4] * buf[1]
    o_ref[...] = _rms(x, g_ref[...])


def _combine(x2, route, dest, ys, g_final, tm):
    N, D = x2.shape
    return pl.pallas_call(
        functools.partial(_combine_kernel, tm=tm),
        grid=(N // tm,),
        in_specs=[pl.BlockSpec((1, 1, TOP_K * tm), lambda i: (i, 0, 0), memory_space=pltpu.SMEM),
                  pl.BlockSpec((tm, D), lambda i: (i, 0)),
                  pl.BlockSpec((tm, LANES), lambda i: (i, 0)),
                  pl.BlockSpec((1, D), lambda i: (0, 0)),
                  pl.BlockSpec(memory_space=pl.ANY)],
        out_specs=pl.BlockSpec((tm, D), lambda i: (i, 0)),
        out_shape=jax.ShapeDtypeStruct((N, D), F32),
        scratch_shapes=[pltpu.VMEM((TOP_K, tm, D), F32), pltpu.SemaphoreType.DMA(())],
        compiler_params=_cparams(("arbitrary",)),
        name="moe_combine",
    )(dest.reshape(N // tm, 1, TOP_K * tm), x2, route, g_final.reshape(1, D), ys)


def kernel(x, mem, positions, g_mix, w_in, conv_w, conv_b, w_q_m, w_k_m, b_i, b_f, g_mhn, skip_m, w_out, g_cross, g_mem, w_q_x, w_kv_x, w_o_x, g_ffn, w_router_g, b_router_g, w_router_e, b_router_e, w1, w3, w2, g_final):
    B, S, D = x.shape
    N = B * S
    depth = g_mix.shape[0]
    tm_in = 512
    tm_post = 256
    tm_moe = 256
    for l in range(depth):
        x2d = x.reshape(N, D)
        pos = positions.astype(F32).reshape(N, 1)
        k_mem, v_mem = _mem_kv(mem, g_mem[l], w_kv_x[l])
        q, k, v, mu, mv, mo, gt, gtt = _in_proj(x2d, pos, g_mix[l], w_in[l], b_i[l], b_f[l], tm_in)
        q, k, v = (t.reshape(B, S, ATT_WIDTH) for t in (q, k, v))
        outs, lses = [], []
        for window, d in DILATED_CONFIGS:
            assert window // d == ATT_BLOCK
            o, lse = _attention_config(q, k, v, d)
            outs.append(o.reshape(N, ATT_WIDTH))
            lses.append(lse.reshape(N, LANES))
        y_m = _mlstm(mu.reshape(B, S, M_WIDTH), mv.reshape(B, S, M_WIDTH), mo.reshape(B, S, M_WIDTH),
                     gt.reshape(B, S, LANES), gtt, conv_w[l], conv_b[l], w_q_m[l], w_k_m[l],
                     g_mhn[l], skip_m[l]).reshape(N, M_WIDTH)
        x2, h3, route, cnt = _post_mix(x2d, outs, lses, y_m, w_out[l], g_cross[l], w_q_x[l], k_mem,
                                       v_mem, w_o_x[l], g_ffn[l], w_router_g[l], b_router_g[l],
                                       w_router_e[l], b_router_e[l], tm_post, S // tm_post)

        e_id = route[:, 0:TOP_K].astype(jnp.int32)
        rank = route[:, 4:4 + TOP_K].astype(jnp.int32)
        counts = cnt[0, N_GROUPS:N_GROUPS + N_EXPERTS].astype(jnp.int32)
        padded = ((counts + MOE_ROWS - 1) // MOE_ROWS) * MOE_ROWS
        pends = jnp.cumsum(padded)
        pstarts = pends - padded
        onehot = e_id[:, :, None] == jnp.arange(N_EXPERTS, dtype=jnp.int32)
        dest = jnp.sum(jnp.where(onehot, pstarts, 0), axis=-1) + rank
        n_rows = N * TOP_K + N_EXPERTS * MOE_ROWS
        nblk = n_rows // MOE_ROWS
        blk_start = jnp.arange(nblk, dtype=jnp.int32) * MOE_ROWS
        blk_e = jnp.minimum(jnp.sum(pends[None, :] <= blk_start[:, None], axis=1), N_EXPERTS - 1)
        nvalid = (pends[-1] // MOE_ROWS).reshape(1)

        xs = _dispatch(h3, dest, n_rows, tm_moe)
        ys = _experts(xs, blk_e.astype(jnp.int32), nvalid.astype(jnp.int32), w1[l], w3[l], w2[l])
        assert depth == 1
        x = _combine(x2, route, dest, ys, g_final, tm_moe).reshape(B, S, D)
    return x
```

```python
import functools

import jax
import jax.numpy as jnp
import numpy as np
from jax import lax
from jax.experimental import pallas as pl
from jax.experimental.pallas import tpu as pltpu

F32 = jnp.float32
BF16 = jnp.bfloat16

EPS = 1e-6
LANES = 128
ATT_HEAD_DIM = 64
ATT_WIDTH = 512
DILATED_CONFIGS = ((128, 1), (512, 4), (2048, 16))
ATT_BLOCK = 128
ROPE_THETA = 500000.0
ROPE_DIM = ATT_HEAD_DIM // 4
M_WIDTH = 512
M_HEADS = 4
M_HEAD_DIM = 128
CONV_WIDTH = 4
M_CHUNK = 128
M_BATCH = 2
X_HEADS = 4
X_HEAD_DIM = 64
X_WIDTH = X_HEADS * X_HEAD_DIM
N_GROUPS = 4
EXPERTS_PER_GROUP = 4
N_EXPERTS = 16
TOP_K = 2
EXPERT_FF = 512
PAIRS = tuple((a, b) for a in range(EXPERTS_PER_GROUP) for b in range(a + 1, EXPERTS_PER_GROUP))
N_BUCKETS = N_GROUPS * len(PAIRS)
MOE_ROWS = 256
ROUTE_W = LANES
VMEM_LIMIT = 48 * 1024 * 1024
DMA_UNROLL = 8

NEG_INF = float("-inf")


def _cparams(sem):
    return pltpu.CompilerParams(dimension_semantics=sem, vmem_limit_bytes=VMEM_LIMIT)


def _rms(x, g):
    return x * lax.rsqrt(jnp.mean(x * x, axis=-1, keepdims=True) + EPS) * g


def _dot(a, b):
    return jnp.dot(a, b, preferred_element_type=F32)


def _dot_nt(a, b):
    return lax.dot_general(a, b, (((1,), (1,)), ((), ())), preferred_element_type=F32)


def _dot_tn(a, b):
    return lax.dot_general(a, b, (((0,), (0,)), ((), ())), preferred_element_type=F32)


def _hi_lo(a):
    hi = a.astype(BF16)
    return hi, (a - hi.astype(F32)).astype(BF16)


def _log_sigmoid(x):
    return jnp.minimum(x, 0.0) - jnp.log(1.0 + jnp.exp(-jnp.abs(x)))


def _mem_kv_kernel(mem_ref, g_ref, w_ref, k_ref, v_ref):
    h = _rms(mem_ref[0], g_ref[...]).astype(BF16)
    kv = _dot(h, w_ref[...])
    k_ref[0] = kv[:, :X_WIDTH].astype(BF16)
    v_ref[0] = kv[:, X_WIDTH:].astype(BF16)


def _mem_kv(mem, g_mem, w_kv):
    B, M, D = mem.shape
    return pl.pallas_call(
        _mem_kv_kernel,
        grid=(B,),
        in_specs=[pl.BlockSpec((1, M, D), lambda b: (b, 0, 0)),
                  pl.BlockSpec((1, D), lambda b: (0, 0)),
                  pl.BlockSpec((D, 2 * X_WIDTH), lambda b: (0, 0))],
        out_specs=[pl.BlockSpec((1, M, X_WIDTH), lambda b: (b, 0, 0)),
                   pl.BlockSpec((1, M, X_WIDTH), lambda b: (b, 0, 0))],
        out_shape=[jax.ShapeDtypeStruct((B, M, X_WIDTH), BF16)] * 2,
        compiler_params=_cparams(("arbitrary",)),
        name="mem_kv",
    )(mem, g_mem.reshape(1, D), w_kv.astype(BF16))


def _in_proj_kernel(x_ref, pos_ref, g_ref, wqkv_ref, wm_ref, wg_ref, wgt_ref, bg_ref, bgt_ref,
                    invf_ref, sgn_ref, q_ref, k_ref, v_ref, q4_ref, k4_ref, v4_ref, q16_ref, k16_ref,
                    v16_ref, mu_ref, mv_ref, mo_ref, gt_ref, gtt_ref, stage, *, tm):
    hb = _rms(x_ref[...], g_ref[...]).astype(BF16)
    qkv = _dot(hb, wqkv_ref[...])
    ang = pos_ref[...] * invf_ref[...]
    cos = jnp.cos(ang)
    sin = jnp.sin(ang)
    s_lo = sin * sgn_ref[0:1, :]
    s_hi = sin * sgn_ref[1:2, :]
    half = ROPE_DIM // 2

    def emit(nat_ref, dil_refs):
        for g in range(ATT_WIDTH // LANES):
            cols = slice(g * LANES, (g + 1) * LANES)
            nat_ref[:, cols] = stage[g].astype(BF16)
            for (_, d), ref in zip(DILATED_CONFIGS[1:], dil_refs):
                for r in range(d):
                    ref[0, r, :, cols] = stage[g, pl.ds(r, tm // d, stride=d), :].astype(BF16)

    for which, refs in ((0, (q_ref, q4_ref, q16_ref)), (1, (k_ref, k4_ref, k16_ref))):
        for g in range(ATT_WIDTH // LANES):
            t = qkv[:, which * ATT_WIDTH + g * LANES: which * ATT_WIDTH + (g + 1) * LANES]
            stage[g] = t * cos + pltpu.roll(t, LANES - half, 1) * s_lo + pltpu.roll(t, half, 1) * s_hi
        emit(refs[0], refs[1:])
    for g in range(ATT_WIDTH // LANES):
        stage[g] = qkv[:, 2 * ATT_WIDTH + g * LANES:2 * ATT_WIDTH + (g + 1) * LANES]
    emit(v_ref, (v4_ref, v16_ref))
    mm = _dot(hb, wm_ref[...])
    mu_ref[...] = mm[:, :M_WIDTH]
    mv_ref[...] = mm[:, M_WIDTH:2 * M_WIDTH].astype(BF16)
    mo_ref[...] = mm[:, 2 * M_WIDTH:]
    gt_ref[...] = _dot(hb, wg_ref[...]) + bg_ref[...]
    gtt_ref[0] = _dot_nt(wgt_ref[...], hb) + bgt_ref[...]


def _in_proj(x2d, pos, g_mix, w_in, b_i, b_f, tm, B):
    N, D = x2d.shape
    S = N // B
    tps = S // tm
    A = ATT_WIDTH
    wq = w_in[:, :A] * (ATT_HEAD_DIM ** -0.5)
    wqkv = jnp.concatenate([wq, w_in[:, A:3 * A]], axis=1).astype(BF16)
    wm = w_in[:, 3 * A:3 * A + 3 * M_WIDTH].astype(BF16)
    wgates = w_in[:, 3 * A + 3 * M_WIDTH:]
    wg = jnp.pad(wgates, ((0, 0), (0, LANES - 2 * M_HEADS))).astype(BF16)
    wgt = wgates.T.astype(BF16)
    bias = jnp.concatenate([b_i, b_f]).astype(F32)
    bg = jnp.pad(bias, (0, LANES - 2 * M_HEADS)).reshape(1, LANES)
    bgt = bias.reshape(2 * M_HEADS, 1)
    j = np.arange(LANES) % ATT_HEAD_DIM
    inv_freq = ROPE_THETA ** (-jnp.arange(0, ROPE_DIM, 2, dtype=F32) / ROPE_DIM)
    invf = jnp.where(j < ROPE_DIM, inv_freq[j % (ROPE_DIM // 2)], 0.0).reshape(1, LANES).astype(F32)
    sgn = np.zeros((8, LANES), np.float32)
    sgn[0] = np.where(j < ROPE_DIM // 2, -1.0, 0.0)
    sgn[1] = np.where((j >= ROPE_DIM // 2) & (j < ROPE_DIM), 1.0, 0.0)
    const = lambda shape: pl.BlockSpec(shape, lambda i: (0,) * len(shape))
    row = lambda w: pl.BlockSpec((tm, w), lambda i: (i, 0))
    dil = lambda d: pl.BlockSpec((1, d, tm // d, A), lambda i: (i // tps, 0, i % tps, 0))
    dil_specs = [dil(d) for _, d in DILATED_CONFIGS[1:]]
    dil_shapes = [jax.ShapeDtypeStruct((B, d, S // d, A), BF16) for _, d in DILATED_CONFIGS[1:]]
    return pl.pallas_call(
        functools.partial(_in_proj_kernel, tm=tm),
        grid=(N // tm,),
        in_specs=[row(D), row(1), const((1, D)), const((D, 3 * A)), const((D, 3 * M_WIDTH)),
                  const((D, LANES)), const((2 * M_HEADS, D)), const((1, LANES)),
                  const((2 * M_HEADS, 1)), const((1, LANES)), const((8, LANES))],
        out_specs=[row(A), row(A), row(A)] + [sp for sp in dil_specs for _ in range(3)]
        + [row(M_WIDTH), row(M_WIDTH), row(M_WIDTH), row(LANES),
           pl.BlockSpec((1, 2 * M_HEADS, tm), lambda i: (i // tps, 0, i % tps))],
        out_shape=[jax.ShapeDtypeStruct((N, A), BF16)] * 3 + [sh for sh in dil_shapes for _ in range(3)]
        + [jax.ShapeDtypeStruct((N, M_WIDTH), F32), jax.ShapeDtypeStruct((N, M_WIDTH), BF16),
           jax.ShapeDtypeStruct((N, M_WIDTH), F32), jax.ShapeDtypeStruct((N, LANES), F32),
           jax.ShapeDtypeStruct((B, 2 * M_HEADS, S), F32)],
        scratch_shapes=[pltpu.VMEM((A // LANES, tm, LANES), F32)],
        compiler_params=_cparams(("arbitrary",)),
        name="in_proj",
    )(x2d, pos, g_mix.reshape(1, D), wqkv, wm, wg, wgt, bg, bgt, invf, jnp.asarray(sgn))


def _attn_kernel(q_ref, kc_ref, kp_ref, vc_ref, vp_ref, o_ref, l_ref, kbuf, vbuf, *, qb):
    blk = ATT_BLOCK
    j = pl.program_id(2)
    kbuf[0:blk, :] = kp_ref[...]
    kbuf[blk:blk + qb, :] = kc_ref[...]
    vbuf[0:blk, :] = vp_ref[...]
    vbuf[blk:blk + qb, :] = vc_ref[...]
    qi = lax.broadcasted_iota(jnp.int32, (blk, 2 * blk), 0)
    ki = lax.broadcasted_iota(jnp.int32, (blk, 2 * blk), 1)
    dist = qi + blk - ki
    band = (dist >= 0) & (dist <= blk)
    band_first = band & ((ki >= blk) | (j > 0))
    lane = lax.broadcasted_iota(jnp.int32, (1, LANES), 1)
    lane_full = lax.broadcasted_iota(jnp.int32, (blk, LANES), 1)
    low = lane < ATT_HEAD_DIM
    for sub in range(qb // blk):
        mask = band_first if sub == 0 else band
        rows = slice(sub * blk, (sub + 1) * blk)
        lse_all = jnp.zeros((blk, LANES), F32)
        for g in range(ATT_WIDTH // LANES):
            cols = slice(g * LANES, (g + 1) * LANES)
            qs = q_ref[rows, cols]
            ks = kbuf[sub * blk:(sub + 2) * blk, cols]
            vs = vbuf[sub * blk:(sub + 2) * blk, cols]
            acc = jnp.zeros((blk, LANES), F32)
            for hh in range(2):
                hm = low if hh == 0 else jnp.logical_not(low)
                s = _dot_nt(jnp.where(hm, qs, jnp.zeros_like(qs)), ks)
                s = jnp.where(mask, s, NEG_INF)
                m = jnp.max(s, axis=-1, keepdims=True)
                p = jnp.exp(s - m)
                l = jnp.sum(p, axis=-1, keepdims=True)
                pv = _dot(p.astype(BF16), jnp.where(hm, vs, jnp.zeros_like(vs)))
                acc = acc + pv * (1.0 / l)
                lse_all = jnp.where(lane_full == 2 * g + hh, m + jnp.log(l), lse_all)
            o_ref[rows, cols] = acc
        l_ref[rows, :] = lse_all


def _attention_config(q, k, v):
    B, d, L, W = q.shape
    qb = min(512, L)
    nsub = qb // ATT_BLOCK
    cur = pl.BlockSpec((None, None, qb, W), lambda b, r, j: (b, r, j, 0))
    prev = pl.BlockSpec((None, None, ATT_BLOCK, W),
                        lambda b, r, j: (b, r, jnp.maximum(j * nsub - 1, 0), 0))
    return pl.pallas_call(
        functools.partial(_attn_kernel, qb=qb),
        grid=(B, d, L // qb),
        in_specs=[cur, cur, prev, cur, prev],
        out_specs=[pl.BlockSpec((None, None, qb, W), lambda b, r, j: (b, r, j, 0)),
                   pl.BlockSpec((None, None, qb, LANES), lambda b, r, j: (b, r, j, 0))],
        out_shape=[jax.ShapeDtypeStruct((B, d, L, W), F32),
                   jax.ShapeDtypeStruct((B, d, L, LANES), F32)],
        scratch_shapes=[pltpu.VMEM((qb + ATT_BLOCK, W), BF16), pltpu.VMEM((qb + ATT_BLOCK, W), BF16)],
        compiler_params=_cparams(("arbitrary", "arbitrary", "arbitrary")),
        name=f"attention_d{d}",
    )(q, k, k, v, v)


def _mlstm_kernel(mu_ref, mv_ref, mo_ref, gt_ref, gtt_ref, cw_ref, cb_ref, wq_ref, wk_ref, gn_ref,
                  sk_ref, y_ref, c_scr, n_scr, m_scr, ext_scr):
    L = M_CHUNK
    H = M_HEADS

    @pl.when(pl.program_id(1) == 0)
    def _():
        c_scr[...] = jnp.zeros_like(c_scr)
        n_scr[...] = jnp.zeros_like(n_scr)
        m_scr[...] = jnp.zeros_like(m_scr)
        for bb in range(M_BATCH):
            ext_scr[bb, 0:8, :] = jnp.zeros((8, M_WIDTH), F32)

    ri = lax.broadcasted_iota(jnp.int32, (L, L), 0)
    ci = lax.broadcasted_iota(jnp.int32, (L, L), 1)
    causal = ci <= ri
    tril = jnp.where(causal, 1.0, 0.0).astype(BF16)
    triu = jnp.where(ri <= ci, 1.0, 0.0).astype(BF16)
    scale = M_HEAD_DIM ** -0.5

    for bb in range(M_BATCH):
        mu = mu_ref[bb]
        ext_scr[bb, 8:8 + L, :] = mu
        conv = cb_ref[...]
        for jj in range(CONV_WIDTH):
            lo = 8 - (CONV_WIDTH - 1) + jj
            conv = conv + ext_scr[bb, lo:lo + L, :] * cw_ref[jj:jj + 1, :]
        ext_scr[bb, 0:8, :] = mu[L - 8:, :]
        cs = conv * jax.nn.sigmoid(conv)
        cb16 = cs.astype(BF16)

        gt = gt_ref[bb]
        gtt = gtt_ref[bb]
        hi_c, lo_c = _hi_lo(_log_sigmoid(gt))
        b_cols = _dot(tril, hi_c) + _dot(tril, lo_c)
        hi_r, lo_r = _hi_lo(_log_sigmoid(gtt))
        b_rows = _dot(hi_r, triu) + _dot(lo_r, triu)

        for hd in range(H):
            cols = slice(hd * M_HEAD_DIM, (hd + 1) * M_HEAD_DIM)
            st = bb * H + hd
            ch = cb16[:, cols]
            q = _dot(ch, wq_ref[hd])
            k = _dot(ch, wk_ref[hd]) * scale
            qb = q.astype(BF16)
            kb = k.astype(BF16)
            v = mv_ref[bb, :, cols]
            ig_c = gt[:, hd:hd + 1]
            b_c = b_cols[:, H + hd:H + hd + 1]
            ig_r = gtt[hd:hd + 1, :]
            b_r = b_rows[H + hd:H + hd + 1, :]
            b_last = b_r[:, L - 1:L]
            m_prev = m_scr[st:st + 1, 0:1]
            c_prev = c_scr[st]
            n_prev = n_scr[st:st + 1, :]

            dlog = jnp.where(causal, b_c - b_r + ig_r, NEG_INF)
            m_intra = jnp.max(dlog, axis=-1, keepdims=True)
            m_inter = b_c + m_prev
            m_t = jnp.maximum(m_inter, m_intra)
            dw = jnp.exp(dlog - m_t)
            inter_w = jnp.exp(m_inter - m_t)
            s = _dot_nt(qb, kb) * dw
            num = _dot(s.astype(BF16), v) + inter_w * _dot(qb, c_prev.astype(BF16))
            den = (jnp.sum(s, axis=-1, keepdims=True)
                   + inter_w * jnp.sum(q * n_prev, axis=-1, keepdims=True))
            hh = num / jnp.maximum(jnp.abs(den), jnp.exp(-m_t))
            hn = hh * lax.rsqrt(jnp.mean(hh * hh, axis=-1, keepdims=True) + EPS) * gn_ref[:, cols]
            y = jax.nn.sigmoid(mo_ref[bb, :, cols]) * (hn + sk_ref[:, cols] * cs[:, cols])
            y_ref[bb, :, cols] = y.astype(BF16)

            gcol = b_last - b_c + ig_c
            m_loc = jnp.max(gcol, axis=0, keepdims=True)
            kw = k * jnp.exp(gcol - m_loc)
            c_loc = _dot_tn(kw.astype(BF16), v)
            n_loc = jnp.sum(kw, axis=0, keepdims=True)
            m_new = jnp.maximum(b_last + m_prev, m_loc)
            a = jnp.exp(b_last + m_prev - m_new)
            cc = jnp.exp(m_loc - m_new)
            c_scr[st] = a * c_prev + cc * c_loc
            n_scr[st:st + 1, :] = a * n_prev + cc * n_loc
            m_scr[st:st + 1, :] = jnp.broadcast_to(m_new, (1, LANES))


def _mlstm(mu, mv, mo, gt, gtt, conv_w, conv_b, w_q_m, w_k_m, g_mhn, skip_m):
    B, S, W = mu.shape
    L = M_CHUNK
    nc = S // L
    nb = M_BATCH
    tok = pl.BlockSpec((nb, L, W), lambda b, c: (b, c, 0))
    const = lambda shape: pl.BlockSpec(shape, lambda b, c: (0,) * len(shape))
    return pl.pallas_call(
        _mlstm_kernel,
        grid=(B // nb, nc),
        in_specs=[tok, tok, tok,
                  pl.BlockSpec((nb, L, LANES), lambda b, c: (b, c, 0)),
                  pl.BlockSpec((nb, 2 * M_HEADS, L), lambda b, c: (b, 0, c)),
                  const((CONV_WIDTH, W)), const((1, W)),
                  const((M_HEADS, M_HEAD_DIM, M_HEAD_DIM)), const((M_HEADS, M_HEAD_DIM, M_HEAD_DIM)),
                  const((1, W)), const((1, W))],
        out_specs=tok,
        out_shape=jax.ShapeDtypeStruct((B, S, W), BF16),
        scratch_shapes=[pltpu.VMEM((nb * M_HEADS, M_HEAD_DIM, M_HEAD_DIM), F32),
                        pltpu.VMEM((nb * M_HEADS, M_HEAD_DIM), F32),
                        pltpu.VMEM((nb * M_HEADS, LANES), F32),
                        pltpu.VMEM((nb, 8 + L, W), F32)],
        compiler_params=_cparams(("arbitrary", "arbitrary")),
        name="mlstm",
    )(mu, mv, mo, gt, gtt, conv_w.astype(F32), conv_b.reshape(1, W).astype(F32),
      w_q_m.astype(BF16), w_k_m.astype(BF16), g_mhn.reshape(1, W).astype(F32),
      skip_m.reshape(1, W).astype(F32))


def _post_mix_kernel(x_ref, o1_ref, o4_ref, o16_ref, l1_ref, l4_ref, l16_ref, ym_ref, wo_ref, gc_ref,
                     wqx_ref, km_ref, vm_ref, wox_ref, gf_ref, wr_ref, br_ref,
                     x2_ref, hx_ref, route_ref, cnt_ref, run_scr, o4_scr, o16_scr, l4_scr, l16_scr, *, tm):
    @pl.when(pl.program_id(0) == 0)
    def _():
        run_scr[...] = jnp.zeros_like(run_scr)

    for (_, d), src, dst in zip(DILATED_CONFIGS[1:], (o4_ref, o16_ref), (o4_scr, o16_scr)):
        for r in range(d):
            for g in range(ATT_WIDTH // LANES):
                dst[g, pl.ds(r, tm // d, stride=d), :] = src[r, :, g * LANES:(g + 1) * LANES]
    for (_, d), src, dst in zip(DILATED_CONFIGS[1:], (l4_ref, l16_ref), (l4_scr, l16_scr)):
        for r in range(d):
            dst[pl.ds(r, tm // d, stride=d), :] = src[r]

    lane1 = lax.broadcasted_iota(jnp.int32, (1, LANES), 1)
    low = lane1 < ATT_HEAD_DIM

    l1, l2, l3 = l1_ref[...], l4_scr[...], l16_scr[...]
    mx = jnp.maximum(jnp.maximum(l1, l2), l3)
    e1, e2, e3 = jnp.exp(l1 - mx), jnp.exp(l2 - mx), jnp.exp(l3 - mx)
    inv = 1.0 / (e1 + e2 + e3)
    wts = (e1 * inv, e2 * inv, e3 * inv)
    mix = _dot(ym_ref[...], wo_ref[ATT_WIDTH:, :])
    for g in range(ATT_WIDTH // LANES):
        cols = slice(g * LANES, (g + 1) * LANES)
        o_slabs = (o1_ref[:, cols], o4_scr[g], o16_scr[g])
        ya = jnp.zeros((tm, LANES), F32)
        for c in range(3):
            w = jnp.where(low, wts[c][:, 2 * g:2 * g + 1], wts[c][:, 2 * g + 1:2 * g + 2])
            ya = ya + w * o_slabs[c]
        mix = mix + _dot(ya.astype(BF16), wo_ref[g * LANES:(g + 1) * LANES, :])
    x1 = x_ref[...] + mix

    h2 = _rms(x1, gc_ref[...]).astype(BF16)
    qx = _dot(h2, wqx_ref[...]).astype(BF16)
    ox = []
    for g in range(X_WIDTH // LANES):
        cols = slice(g * LANES, (g + 1) * LANES)
        qs = qx[:, cols]
        ks = km_ref[0, :, cols]
        vs = vm_ref[0, :, cols]
        acc = jnp.zeros((tm, LANES), F32)
        for hh in range(2):
            hm = low if hh == 0 else jnp.logical_not(low)
            s = _dot_nt(jnp.where(hm, qs, jnp.zeros_like(qs)), ks)
            m = jnp.max(s, axis=-1, keepdims=True)
            p = jnp.exp(s - m)
            l = jnp.sum(p, axis=-1, keepdims=True)
            acc = acc + _dot(p.astype(BF16), jnp.where(hm, vs, jnp.zeros_like(vs))) * (1.0 / l)
        ox.append(acc.astype(BF16))
    x2 = x1
    for g in range(X_WIDTH // LANES):
        x2 = x2 + _dot(ox[g], wox_ref[g * LANES:(g + 1) * LANES, :])
    x2_ref[...] = x2

    h3 = _rms(x2, gf_ref[...])
    h_hi, h_lo = _hi_lo(h3)
    t = _dot(h_hi, wr_ref[...])
    logits = t[:, :LANES] + t[:, LANES:] + _dot(h_lo, wr_ref[:, :LANES]) + br_ref[...]

    lane = lax.broadcasted_iota(jnp.int32, (tm, LANES), 1).astype(F32)
    far = float(LANES)
    gmask = lane < N_GROUPS
    gl = jnp.where(gmask, logits, NEG_INF)
    gmax = jnp.max(gl, axis=-1, keepdims=True)
    gidx = jnp.min(jnp.where(gl == gmax, lane, far), axis=-1, keepdims=True)
    gsum = jnp.sum(jnp.where(gmask, jnp.exp(gl - gmax), 0.0), axis=-1, keepdims=True)
    g_w = 1.0 / gsum
    lo_lane = N_GROUPS + gidx * EXPERTS_PER_GROUP
    emask = (lane >= lo_lane) & (lane < lo_lane + EXPERTS_PER_GROUP)
    el = jnp.where(emask, logits, NEG_INF)
    t1 = jnp.max(el, axis=-1, keepdims=True)
    i1 = jnp.min(jnp.where(el == t1, lane, far), axis=-1, keepdims=True)
    el2 = jnp.where(lane == i1, NEG_INF, el)
    t2 = jnp.max(el2, axis=-1, keepdims=True)
    i2 = jnp.min(jnp.where(el2 == t2, lane, far), axis=-1, keepdims=True)
    ee = jnp.exp(t2 - t1)
    w1 = g_w / (1.0 + ee)
    w2 = w1 * ee

    first = i1 < i2
    ia = jnp.minimum(i1, i2)
    ib = jnp.maximum(i1, i2)
    wa = jnp.where(first, w1, w2)
    wb = jnp.where(first, w2, w1)
    la = ia - lo_lane
    lb = ib - lo_lane
    pair = la * (EXPERTS_PER_GROUP - 1) - la * (la - 1.0) * 0.5 + (lb - la - 1.0)
    bucket = gidx * len(PAIRS) + pair

    hit = lane == bucket
    cnt = jnp.where(hit, 1.0, 0.0)
    ri = lax.broadcasted_iota(jnp.int32, (tm, tm), 0)
    ci = lax.broadcasted_iota(jnp.int32, (tm, tm), 1)
    before = jnp.where(ci < ri, 1.0, 0.0).astype(BF16)
    prefix = _dot(before, cnt.astype(BF16)) + run_scr[0:1, :]
    rank = jnp.sum(jnp.where(hit, prefix, 0.0), axis=-1, keepdims=True)
    total = run_scr[0:1, :] + jnp.sum(cnt, axis=0, keepdims=True)
    run_scr[...] = jnp.broadcast_to(total, run_scr.shape)
    cnt_ref[...] = jnp.broadcast_to(total, cnt_ref.shape)

    fields = (ia - N_GROUPS, ib - N_GROUPS, wa, wb, rank, bucket)
    route = jnp.zeros((tm, LANES), F32)
    for idx, val in enumerate(fields):
        route = jnp.where(lane == float(idx), val, route)
    route_ref[...] = route
    hx_ref[:, :x2_ref.shape[1]] = h3
    hx_ref[:, x2_ref.shape[1]:] = route


def _post_mix(x2d, outs, lses, y_m, w_out, g_cross, w_q_x, k_mem, v_mem, w_o_x, g_ffn,
              w_router_g, b_router_g, w_router_e, b_router_e, tm, B):
    N, D = x2d.shape
    S = N // B
    tps = S // tm
    M = k_mem.shape[1]
    wr = jnp.pad(jnp.concatenate([w_router_g, w_router_e], axis=1).astype(F32),
                 ((0, 0), (0, LANES - N_GROUPS - N_EXPERTS)))
    wr_hi = wr.astype(BF16)
    wr_cat = jnp.concatenate([wr_hi, (wr - wr_hi.astype(F32)).astype(BF16)], axis=1)
    br = jnp.pad(jnp.concatenate([b_router_g, b_router_e]).astype(F32),
                 (0, LANES - N_GROUPS - N_EXPERTS)).reshape(1, LANES)
    const = lambda shape: pl.BlockSpec(shape, lambda i: (0,) * len(shape))
    row = lambda w: pl.BlockSpec((tm, w), lambda i: (i, 0))
    dil = lambda d, w: pl.BlockSpec((None, d, tm // d, w), lambda i: (i // tps, 0, i % tps, 0))
    memspec = pl.BlockSpec((1, M, X_WIDTH), lambda i: (i // tps, 0, 0))
    d4, d16 = DILATED_CONFIGS[1][1], DILATED_CONFIGS[2][1]
    return pl.pallas_call(
        functools.partial(_post_mix_kernel, tm=tm),
        grid=(N // tm,),
        in_specs=[row(D), row(ATT_WIDTH), dil(d4, ATT_WIDTH), dil(d16, ATT_WIDTH), row(LANES),
                  dil(d4, LANES), dil(d16, LANES), row(M_WIDTH), const((D, D)), const((1, D)),
                  const((D, X_WIDTH)), memspec, memspec, const((X_WIDTH, D)), const((1, D)),
                  const((D, 2 * LANES)), const((1, LANES))],
        out_specs=[row(D), row(D + ROUTE_W), row(LANES), const((8, LANES))],
        out_shape=[jax.ShapeDtypeStruct((N, D), F32), jax.ShapeDtypeStruct((N, D + ROUTE_W), F32),
                   jax.ShapeDtypeStruct((N, LANES), F32), jax.ShapeDtypeStruct((8, LANES), F32)],
        scratch_shapes=[pltpu.VMEM((8, LANES), F32), pltpu.VMEM((ATT_WIDTH // LANES, tm, LANES), F32),
                        pltpu.VMEM((ATT_WIDTH // LANES, tm, LANES), F32), pltpu.VMEM((tm, LANES), F32),
                        pltpu.VMEM((tm, LANES), F32)],
        compiler_params=_cparams(("arbitrary",)),
        name="post_mix",
    )(x2d, outs[0].reshape(N, ATT_WIDTH), outs[1], outs[2], lses[0].reshape(N, LANES), lses[1], lses[2],
      y_m, w_out.astype(BF16), g_cross.reshape(1, D), (w_q_x * (X_HEAD_DIM ** -0.5)).astype(BF16),
      k_mem, v_mem, w_o_x.astype(BF16), g_ffn.reshape(1, D), wr_cat, br)


def _row_copy(src, src_row, dst, dst_row, sem):
    return pltpu.make_async_copy(src.at[pl.ds(src_row, 1), :], dst.at[pl.ds(dst_row, 1), :], sem)


def _dispatch_kernel(dest_ref, h_ref, xs_in_ref, xs_ref, sem, *, tm):
    del xs_in_ref

    def start(t, carry):
        _row_copy(h_ref, t, xs_ref, dest_ref[0, 0, t], sem).start()
        return carry

    def wait(t, carry):
        _row_copy(h_ref, t, xs_ref, dest_ref[0, 0, t], sem).wait()
        return carry

    lax.fori_loop(0, tm, start, 0, unroll=DMA_UNROLL)
    lax.fori_loop(0, tm, wait, 0, unroll=DMA_UNROLL)


def _dispatch(hx, dest, n_rows, tm):
    N, W = hx.shape
    xs0 = jnp.zeros((n_rows, W), F32)
    return pl.pallas_call(
        functools.partial(_dispatch_kernel, tm=tm),
        grid=(N // tm,),
        in_specs=[pl.BlockSpec((1, 1, tm), lambda i: (i, 0, 0), memory_space=pltpu.SMEM),
                  pl.BlockSpec((tm, W), lambda i: (i, 0)),
                  pl.BlockSpec(memory_space=pl.ANY)],
        out_specs=pl.BlockSpec(memory_space=pl.ANY),
        out_shape=jax.ShapeDtypeStruct((n_rows, W), F32),
        scratch_shapes=[pltpu.SemaphoreType.DMA(())],
        input_output_aliases={2: 0},
        compiler_params=_cparams(("arbitrary",)),
        name="moe_dispatch",
    )(dest.reshape(N // tm, 1, tm), hx, xs0)


def _expert_kernel(blk_a_ref, blk_b_ref, nvalid_ref, x_ref, w1a_ref, w3a_ref, w2a_ref, w1b_ref, w3b_ref,
                   w2b_ref, y_ref):
    del blk_a_ref, blk_b_ref
    D = y_ref.shape[1]

    def ffn(xb, w1_ref, w3_ref, w2_ref):
        a = _dot(xb, w1_ref[0])
        b = _dot(xb, w3_ref[0])
        return _dot((a * jax.nn.sigmoid(a) * b).astype(BF16), w2_ref[0])

    @pl.when(pl.program_id(0) < nvalid_ref[0])
    def _():
        xb = x_ref[:, :D].astype(BF16)
        wa = x_ref[:, D + 2:D + 3]
        wb = x_ref[:, D + 3:D + 4]
        y_ref[...] = (wa * ffn(xb, w1a_ref, w3a_ref, w2a_ref)
                      + wb * ffn(xb, w1b_ref, w3b_ref, w2b_ref))

    @pl.when(pl.program_id(0) >= nvalid_ref[0])
    def _():
        y_ref[...] = jnp.zeros_like(y_ref)


def _experts(xs, blk_a, blk_b, nvalid, w1, w3, w2):
    P, W = xs.shape
    D = W - ROUTE_W
    nblk = P // MOE_ROWS
    F = w1.shape[-1]
    up_a = pl.BlockSpec((1, D, F), lambda i, ba, bb, nv: (ba[i], 0, 0))
    up_b = pl.BlockSpec((1, D, F), lambda i, ba, bb, nv: (bb[i], 0, 0))
    grid_spec = pltpu.PrefetchScalarGridSpec(
        num_scalar_prefetch=3,
        grid=(nblk,),
        in_specs=[pl.BlockSpec((MOE_ROWS, W), lambda i, ba, bb, nv: (i, 0)),
                  up_a, up_a, pl.BlockSpec((1, F, D), lambda i, ba, bb, nv: (ba[i], 0, 0)),
                  up_b, up_b, pl.BlockSpec((1, F, D), lambda i, ba, bb, nv: (bb[i], 0, 0))],
        out_specs=pl.BlockSpec((MOE_ROWS, D), lambda i, ba, bb, nv: (i, 0)),
    )
    w1b, w3b, w2b = w1.astype(BF16), w3.astype(BF16), w2.astype(BF16)
    return pl.pallas_call(
        _expert_kernel,
        grid_spec=grid_spec,
        out_shape=jax.ShapeDtypeStruct((P, D), F32),
        compiler_params=_cparams(("arbitrary",)),
        name="moe_experts",
    )(blk_a, blk_b, nvalid, xs, w1b, w3b, w2b, w1b, w3b, w2b)


def _combine_kernel(dest_ref, x_ref, g_ref, ys_ref, o_ref, buf, sem, *, tm):
    def start(t, carry):
        _row_copy(ys_ref, dest_ref[0, 0, t], buf, t, sem).start()
        return carry

    def wait(t, carry):
        _row_copy(ys_ref, dest_ref[0, 0, t], buf, t, sem).wait()
        return carry

    lax.fori_loop(0, tm, start, 0, unroll=DMA_UNROLL)
    lax.fori_loop(0, tm, wait, 0, unroll=DMA_UNROLL)
    o_ref[...] = _rms(x_ref[...] + buf[...], g_ref[...])


def _combine(x2, dest, ys, g_final, tm):
    N, D = x2.shape
    return pl.pallas_call(
        functools.partial(_combine_kernel, tm=tm),
        grid=(N // tm,),
        in_specs=[pl.BlockSpec((1, 1, tm), lambda i: (i, 0, 0), memory_space=pltpu.SMEM),
                  pl.BlockSpec((tm, D), lambda i: (i, 0)),
                  pl.BlockSpec((1, D), lambda i: (0, 0)),
                  pl.BlockSpec(memory_space=pl.ANY)],
        out_specs=pl.BlockSpec((tm, D), lambda i: (i, 0)),
        out_shape=jax.ShapeDtypeStruct((N, D), F32),
        scratch_shapes=[pltpu.VMEM((tm, D), F32), pltpu.SemaphoreType.DMA(())],
        compiler_params=_cparams(("arbitrary",)),
        name="moe_combine",
    )(dest.reshape(N // tm, 1, tm), x2, g_final.reshape(1, D), ys)


def kernel(x, mem, positions, g_mix, w_in, conv_w, conv_b, w_q_m, w_k_m, b_i, b_f, g_mhn, skip_m, w_out, g_cross, g_mem, w_q_x, w_kv_x, w_o_x, g_ffn, w_router_g, b_router_g, w_router_e, b_router_e, w1, w3, w2, g_final):
    B, S, D = x.shape
    N = B * S
    depth = g_mix.shape[0]
    tm_in = 512
    tm_post = 256
    tm_moe = 512
    assert all(window // d == ATT_BLOCK for window, d in DILATED_CONFIGS)
    assert depth == 1
    for l in range(depth):
        x2d = x.reshape(N, D)
        pos = positions.astype(F32).reshape(N, 1)
        k_mem, v_mem = _mem_kv(mem, g_mem[l], w_kv_x[l])
        (q, k, v, q4, k4, v4, q16, k16, v16, mu, mv, mo, gt, gtt) = _in_proj(
            x2d, pos, g_mix[l], w_in[l], b_i[l], b_f[l], tm_in, B)
        q, k, v = (t.reshape(B, 1, S, ATT_WIDTH) for t in (q, k, v))
        outs, lses = zip(*(_attention_config(*qkv) for qkv in ((q, k, v), (q4, k4, v4), (q16, k16, v16))))
        y_m = _mlstm(mu.reshape(B, S, M_WIDTH), mv.reshape(B, S, M_WIDTH), mo.reshape(B, S, M_WIDTH),
                     gt.reshape(B, S, LANES), gtt, conv_w[l], conv_b[l], w_q_m[l], w_k_m[l],
                     g_mhn[l], skip_m[l]).reshape(N, M_WIDTH)
        x2, hx, route, cnt = _post_mix(x2d, outs, lses, y_m, w_out[l], g_cross[l], w_q_x[l], k_mem,
                                       v_mem, w_o_x[l], g_ffn[l], w_router_g[l], b_router_g[l],
                                       w_router_e[l], b_router_e[l], tm_post, B)

        rank = route[:, 4].astype(jnp.int32)
        bucket = route[:, 5].astype(jnp.int32)
        counts = cnt[0, :N_BUCKETS].astype(jnp.int32)
        padded = ((counts + MOE_ROWS - 1) // MOE_ROWS) * MOE_ROWS
        pends = jnp.cumsum(padded)
        pstarts = pends - padded
        onehot = bucket[:, None] == jnp.arange(N_BUCKETS, dtype=jnp.int32)
        dest = jnp.sum(jnp.where(onehot, pstarts, 0), axis=-1) + rank
        n_rows = N + N_BUCKETS * MOE_ROWS
        nblk = n_rows // MOE_ROWS
        blk_start = jnp.arange(nblk, dtype=jnp.int32) * MOE_ROWS
        blk_bucket = jnp.minimum(jnp.sum(pends[None, :] <= blk_start[:, None], axis=1), N_BUCKETS - 1)
        base = (np.arange(N_BUCKETS) // len(PAIRS)) * EXPERTS_PER_GROUP
        expert_a = jnp.asarray(base + np.array([p[0] for p in PAIRS] * N_GROUPS), jnp.int32)
        expert_b = jnp.asarray(base + np.array([p[1] for p in PAIRS] * N_GROUPS), jnp.int32)
        blk_a = jnp.take(expert_a, blk_bucket).astype(jnp.int32)
        blk_b = jnp.take(expert_b, blk_bucket).astype(jnp.int32)
        nvalid = (pends[-1] // MOE_ROWS).reshape(1).astype(jnp.int32)

        xs = _dispatch(hx, dest, n_rows, tm_moe)
        ys = _experts(xs, blk_a, blk_b, nvalid, w1[l], w3[l], w2[l])
        x = _combine(x2, dest, ys, g_final, tm_moe).reshape(B, S, D)
    return x
```

```python
import functools

import jax
import jax.numpy as jnp
import numpy as np
from jax import lax
from jax.experimental import pallas as pl
from jax.experimental.pallas import tpu as pltpu

F32 = jnp.float32
BF16 = jnp.bfloat16

EPS = 1e-6
LANES = 128
ATT_HEAD_DIM = 64
ATT_WIDTH = 512
DILATED_CONFIGS = ((128, 1), (512, 4), (2048, 16))
CLASSES = 16
ATT_BLOCK = 128
ROPE_THETA = 500000.0
ROPE_DIM = ATT_HEAD_DIM // 4
M_WIDTH = 512
M_HEADS = 4
M_HEAD_DIM = 128
CONV_WIDTH = 4
M_CHUNK = 128
M_BATCH = 2
X_HEADS = 4
X_HEAD_DIM = 64
X_WIDTH = X_HEADS * X_HEAD_DIM
N_GROUPS = 4
EXPERTS_PER_GROUP = 4
N_EXPERTS = 16
TOP_K = 2
EXPERT_FF = 512
PAIRS = tuple((a, b) for a in range(EXPERTS_PER_GROUP) for b in range(a + 1, EXPERTS_PER_GROUP))
N_BUCKETS = N_GROUPS * len(PAIRS)
MOE_ROWS = 256
ROUTE_W = LANES
VMEM_LIMIT = 56 * 1024 * 1024
DMA_UNROLL = 8

NEG_INF = float("-inf")


def _cparams(sem):
    return pltpu.CompilerParams(dimension_semantics=sem, vmem_limit_bytes=VMEM_LIMIT)


def _rms(x, g):
    return x * lax.rsqrt(jnp.mean(x * x, axis=-1, keepdims=True) + EPS) * g


def _dot(a, b):
    return jnp.dot(a, b, preferred_element_type=F32)


def _dot_nt(a, b):
    return lax.dot_general(a, b, (((1,), (1,)), ((), ())), preferred_element_type=F32)


def _dot_tn(a, b):
    return lax.dot_general(a, b, (((0,), (0,)), ((), ())), preferred_element_type=F32)


def _hi_lo(a):
    hi = a.astype(BF16)
    return hi, (a - hi.astype(F32)).astype(BF16)


def _log_sigmoid(x):
    return jnp.minimum(x, 0.0) - jnp.log(1.0 + jnp.exp(-jnp.abs(x)))


def _sigmoid(x):
    return 0.5 * jnp.tanh(0.5 * x) + 0.5


def _mem_kv_kernel(mem_ref, g_ref, w_ref, k_ref, v_ref):
    h = _rms(mem_ref[0], g_ref[...]).astype(BF16)
    kv = _dot(h, w_ref[...])
    k_ref[0] = kv[:, :X_WIDTH].astype(BF16)
    v_ref[0] = kv[:, X_WIDTH:].astype(BF16)


def _mem_kv(mem, g_mem, w_kv):
    B, M, D = mem.shape
    return pl.pallas_call(
        _mem_kv_kernel,
        grid=(B,),
        in_specs=[pl.BlockSpec((1, M, D), lambda b: (b, 0, 0)),
                  pl.BlockSpec((1, D), lambda b: (0, 0)),
                  pl.BlockSpec((D, 2 * X_WIDTH), lambda b: (0, 0))],
        out_specs=[pl.BlockSpec((1, M, X_WIDTH), lambda b: (b, 0, 0)),
                   pl.BlockSpec((1, M, X_WIDTH), lambda b: (b, 0, 0))],
        out_shape=[jax.ShapeDtypeStruct((B, M, X_WIDTH), BF16)] * 2,
        compiler_params=_cparams(("arbitrary",)),
        name="mem_kv",
    )(mem, g_mem.reshape(1, D), w_kv.astype(BF16))


def _in_proj_kernel(x_ref, pos_ref, g_ref, wqkv_ref, wm_ref, wmvt_ref, wg_ref, wgt_ref, bg_ref, bgt_ref,
                    invf_ref, sgn_ref, q_ref, k_ref, v_ref, qc_ref, kc_ref, vc_ref,
                    mu_ref, mvt_ref, mo_ref, gt_ref, gtt_ref, stage, *, tm):
    hb = _rms(x_ref[...], g_ref[...]).astype(BF16)
    qkv = _dot(hb, wqkv_ref[...])
    ang = pos_ref[...] * invf_ref[...]
    cos = jnp.cos(ang)
    sin = jnp.sin(ang)
    s_lo = sin * sgn_ref[0:1, :]
    s_hi = sin * sgn_ref[1:2, :]
    half = ROPE_DIM // 2

    def emit(nat_ref, cls_ref):
        for g in range(ATT_WIDTH // LANES):
            cols = slice(g * LANES, (g + 1) * LANES)
            nat_ref[:, cols] = stage[g].astype(BF16)
            for r in range(CLASSES):
                cls_ref[0, r, :, cols] = stage[g, pl.ds(r, tm // CLASSES, stride=CLASSES), :].astype(BF16)

    for which, refs in ((0, (q_ref, qc_ref)), (1, (k_ref, kc_ref))):
        for g in range(ATT_WIDTH // LANES):
            t = qkv[:, which * ATT_WIDTH + g * LANES: which * ATT_WIDTH + (g + 1) * LANES]
            stage[g] = t * cos + pltpu.roll(t, LANES - half, 1) * s_lo + pltpu.roll(t, half, 1) * s_hi
        emit(*refs)
    for g in range(ATT_WIDTH // LANES):
        stage[g] = qkv[:, 2 * ATT_WIDTH + g * LANES:2 * ATT_WIDTH + (g + 1) * LANES]
    emit(v_ref, vc_ref)
    mm = _dot(hb, wm_ref[...])
    mu_ref[...] = mm[:, :M_WIDTH].astype(BF16)
    mo_ref[...] = mm[:, M_WIDTH:].astype(BF16)
    mvt = _dot_nt(wmvt_ref[...], hb)
    for c in range(tm // M_CHUNK):
        mvt_ref[c] = mvt[:, c * M_CHUNK:(c + 1) * M_CHUNK].astype(BF16)
    gt_ref[...] = _dot(hb, wg_ref[...]) + bg_ref[...]
    gtt_ref[0] = _dot_nt(wgt_ref[...], hb) + bgt_ref[...]


def _in_proj(x2d, pos, g_mix, w_in, b_i, b_f, tm, B):
    N, D = x2d.shape
    S = N // B
    tps = S // tm
    A = ATT_WIDTH
    wq = w_in[:, :A] * (ATT_HEAD_DIM ** -0.5)
    wqkv = jnp.concatenate([wq, w_in[:, A:3 * A]], axis=1).astype(BF16)
    o = 3 * A
    wm = jnp.concatenate([w_in[:, o:o + M_WIDTH], w_in[:, o + 2 * M_WIDTH:o + 3 * M_WIDTH]],
                         axis=1).astype(BF16)
    wmvt = w_in[:, o + M_WIDTH:o + 2 * M_WIDTH].T.astype(BF16)
    wgates = w_in[:, 3 * A + 3 * M_WIDTH:]
    wg = jnp.pad(wgates, ((0, 0), (0, LANES - 2 * M_HEADS))).astype(BF16)
    wgt = wgates.T.astype(BF16)
    bias = jnp.concatenate([b_i, b_f]).astype(F32)
    bg = jnp.pad(bias, (0, LANES - 2 * M_HEADS)).reshape(1, LANES)
    bgt = bias.reshape(2 * M_HEADS, 1)
    j = np.arange(LANES) % ATT_HEAD_DIM
    inv_freq = ROPE_THETA ** (-jnp.arange(0, ROPE_DIM, 2, dtype=F32) / ROPE_DIM)
    invf = jnp.where(j < ROPE_DIM, inv_freq[j % (ROPE_DIM // 2)], 0.0).reshape(1, LANES).astype(F32)
    sgn = np.zeros((8, LANES), np.float32)
    sgn[0] = np.where(j < ROPE_DIM // 2, -1.0, 0.0)
    sgn[1] = np.where((j >= ROPE_DIM // 2) & (j < ROPE_DIM), 1.0, 0.0)
    const = lambda shape: pl.BlockSpec(shape, lambda i: (0,) * len(shape))
    row = lambda w: pl.BlockSpec((tm, w), lambda i: (i, 0))
    cls = pl.BlockSpec((1, CLASSES, tm // CLASSES, A), lambda i: (i // tps, 0, i % tps, 0))
    cls_shape = jax.ShapeDtypeStruct((B, CLASSES, S // CLASSES, A), BF16)
    return pl.pallas_call(
        functools.partial(_in_proj_kernel, tm=tm),
        grid=(N // tm,),
        in_specs=[row(D), row(1), const((1, D)), const((D, 3 * A)), const((D, 2 * M_WIDTH)),
                  const((M_WIDTH, D)), const((D, LANES)), const((2 * M_HEADS, D)), const((1, LANES)),
                  const((2 * M_HEADS, 1)), const((1, LANES)), const((8, LANES))],
        out_specs=[row(A), row(A), row(A), cls, cls, cls, row(M_WIDTH),
                   pl.BlockSpec((tm // M_CHUNK, M_WIDTH, M_CHUNK), lambda i: (i, 0, 0)), row(M_WIDTH),
                   row(LANES), pl.BlockSpec((1, 2 * M_HEADS, tm), lambda i: (i // tps, 0, i % tps))],
        out_shape=[jax.ShapeDtypeStruct((N, A), BF16)] * 3 + [cls_shape] * 3
        + [jax.ShapeDtypeStruct((N, M_WIDTH), BF16),
           jax.ShapeDtypeStruct((N // M_CHUNK, M_WIDTH, M_CHUNK), BF16),
           jax.ShapeDtypeStruct((N, M_WIDTH), BF16),
           jax.ShapeDtypeStruct((N, LANES), F32), jax.ShapeDtypeStruct((B, 2 * M_HEADS, S), F32)],
        scratch_shapes=[pltpu.VMEM((A // LANES, tm, LANES), F32)],
        compiler_params=_cparams(("arbitrary",)),
        name="in_proj",
    )(x2d, pos, g_mix.reshape(1, D), wqkv, wm, wmvt, wg, wgt, bg, bgt, invf, jnp.asarray(sgn))


def _attn_kernel(q_ref, kc_ref, kp_ref, vc_ref, vp_ref, o_ref, l_ref, kbuf, vbuf, *, qb, nc):
    blk = ATT_BLOCK
    piece = blk // nc
    nsub = qb // blk
    first = pl.program_id(2) == 0
    for c in range(nc):
        kbuf[c * piece:(c + 1) * piece, :] = kp_ref[c]
        vbuf[c * piece:(c + 1) * piece, :] = vp_ref[c]
        for sub in range(nsub):
            dst = slice((sub + 1) * blk + c * piece, (sub + 1) * blk + (c + 1) * piece)
            src = slice(sub * piece, (sub + 1) * piece)
            kbuf[dst, :] = kc_ref[c, src, :]
            vbuf[dst, :] = vc_ref[c, src, :]

    def pos(p):
        p = p & (blk - 1)
        return nc * (p & (piece - 1)) + (p >> (piece.bit_length() - 1))

    qi = lax.broadcasted_iota(jnp.int32, (blk, 2 * blk), 0)
    ki = lax.broadcasted_iota(jnp.int32, (blk, 2 * blk), 1)
    dist = pos(qi) - pos(ki) + jnp.where(ki < blk, blk, 0)
    band = (dist >= 0) & (dist <= blk)
    band_first = band & ((ki >= blk) | jnp.logical_not(first))
    lane = lax.broadcasted_iota(jnp.int32, (1, LANES), 1)
    lane_full = lax.broadcasted_iota(jnp.int32, (blk, LANES), 1)
    low = lane < ATT_HEAD_DIM
    for sub in range(nsub):
        mask = band_first if sub == 0 else band
        lse_all = jnp.zeros((blk, LANES), F32)
        for g in range(ATT_WIDTH // LANES):
            cols = slice(g * LANES, (g + 1) * LANES)
            qs = jnp.concatenate([q_ref[c, sub * piece:(sub + 1) * piece, cols] for c in range(nc)], axis=0)
            ks = kbuf[sub * blk:(sub + 2) * blk, cols]
            vs = vbuf[sub * blk:(sub + 2) * blk, cols]
            acc = jnp.zeros((blk, LANES), F32)
            for hh in range(2):
                hm = low if hh == 0 else jnp.logical_not(low)
                s = _dot_nt(jnp.where(hm, qs, jnp.zeros_like(qs)), ks)
                s = jnp.where(mask, s, NEG_INF)
                m = jnp.max(s, axis=-1, keepdims=True)
                p = jnp.exp(s - m)
                l = jnp.sum(p, axis=-1, keepdims=True)
                pv = _dot(p.astype(BF16), jnp.where(hm, vs, jnp.zeros_like(vs)))
                acc = acc + pv * (1.0 / l)
                lse_all = jnp.where(lane_full == 2 * g + hh, m + jnp.log(l), lse_all)
            acc = acc.astype(BF16)
            for c in range(nc):
                o_ref[c, sub * piece:(sub + 1) * piece, cols] = acc[c * piece:(c + 1) * piece]
        for c in range(nc):
            l_ref[c, sub * piece:(sub + 1) * piece, :] = lse_all[c * piece:(c + 1) * piece]


def _attention_config(q, k, v, d):
    B, C, L, W = q.shape
    nc = C // d
    qb = min(512, L * nc)
    rows = qb // nc
    piece = ATT_BLOCK // nc
    nsub = qb // ATT_BLOCK
    view = lambda t: t.reshape(B, nc, d, L, t.shape[-1])
    cur = lambda w: pl.BlockSpec((None, nc, None, rows, w), lambda b, r, j: (b, 0, r, j, 0))
    prev = pl.BlockSpec((None, nc, None, piece, W),
                        lambda b, r, j: (b, 0, r, jnp.maximum(j * nsub - 1, 0), 0))
    o, lse = pl.pallas_call(
        functools.partial(_attn_kernel, qb=qb, nc=nc),
        grid=(B, d, L // rows),
        in_specs=[cur(W), cur(W), prev, cur(W), prev],
        out_specs=[cur(W), cur(LANES)],
        out_shape=[jax.ShapeDtypeStruct((B, nc, d, L, W), BF16),
                   jax.ShapeDtypeStruct((B, nc, d, L, LANES), F32)],
        scratch_shapes=[pltpu.VMEM((qb + ATT_BLOCK, W), BF16), pltpu.VMEM((qb + ATT_BLOCK, W), BF16)],
        compiler_params=_cparams(("arbitrary", "arbitrary", "arbitrary")),
        name=f"attention_d{d}",
    )(view(q), view(k), view(k), view(v), view(v))
    return o.reshape(B, C, L, W), lse.reshape(B, C, L, LANES)


def _mlstm_kernel(mu_ref, mvt_ref, mo_ref, gt_ref, gtt_ref, cw_ref, cb_ref, wqk_ref, gn_ref,
                  sk_ref, y_ref, c_scr, n_scr, m_scr, ext_scr):
    L = M_CHUNK
    H = M_HEADS
    HD = M_HEAD_DIM

    @pl.when(pl.program_id(1) == 0)
    def _():
        c_scr[...] = jnp.zeros_like(c_scr)
        n_scr[...] = jnp.zeros_like(n_scr)
        m_scr[...] = jnp.zeros_like(m_scr)
        for bb in range(M_BATCH):
            ext_scr[bb, 0:8, :] = jnp.zeros((8, M_WIDTH), F32)

    ri = lax.broadcasted_iota(jnp.int32, (L, L), 0)
    ci = lax.broadcasted_iota(jnp.int32, (L, L), 1)
    causal_t = ri <= ci
    tril = jnp.where(ci <= ri, 1.0, 0.0).astype(BF16)
    triu = jnp.where(causal_t, 1.0, 0.0).astype(BF16)
    scale = HD ** -0.5

    for bb in range(M_BATCH):
        mu = mu_ref[bb].astype(F32)
        ext_scr[bb, 8:8 + L, :] = mu
        conv = cb_ref[...]
        for jj in range(CONV_WIDTH):
            lo = 8 - (CONV_WIDTH - 1) + jj
            conv = conv + ext_scr[bb, lo:lo + L, :] * cw_ref[jj:jj + 1, :]
        ext_scr[bb, 0:8, :] = mu[L - 8:, :]
        cs = conv * _sigmoid(conv)
        cb16 = cs.astype(BF16)

        gt = gt_ref[bb]
        gtt = gtt_ref[bb]
        hi_c, lo_c = _hi_lo(_log_sigmoid(gt))
        b_cols = _dot(tril, hi_c) + _dot(tril, lo_c)
        hi_r, lo_r = _hi_lo(_log_sigmoid(gtt))
        b_rows = _dot(hi_r, triu) + _dot(lo_r, triu)

        for hd in range(H):
            cols = slice(hd * HD, (hd + 1) * HD)
            st = bb * H + hd
            qk = _dot(cb16[:, cols], wqk_ref[hd])
            qb = qk[:, :HD].astype(BF16)
            kb = (qk[:, HD:] * scale).astype(BF16)
            vt = mvt_ref[bb, cols, :]
            ig_r = gtt[hd:hd + 1, :]
            b_r = b_rows[H + hd:H + hd + 1, :]
            key_c = gt[:, hd:hd + 1] - b_cols[:, H + hd:H + hd + 1]
            b_last = b_r[:, L - 1:L]
            m_prev = m_scr[st:st + 1, 0:1]
            ct_prev = c_scr[st]
            n_prev = n_scr[st]

            dlog = jnp.where(causal_t, b_r + key_c, NEG_INF)
            m_intra = jnp.max(dlog, axis=0, keepdims=True)
            m_inter = b_r + m_prev
            m_t = jnp.maximum(m_inter, m_intra)
            inter_w = jnp.exp(m_inter - m_t)
            st_w = _dot_nt(kb, qb) * jnp.exp(dlog - m_t)
            num = _dot(vt, st_w.astype(BF16)) + inter_w * _dot_nt(ct_prev.astype(BF16), qb)
            den = (jnp.sum(st_w, axis=0, keepdims=True)
                   + inter_w * _dot_nt(n_prev.astype(BF16), qb)[0:1, :])
            ht = num * (1.0 / jnp.maximum(jnp.abs(den), jnp.exp(-m_t)))
            ht = ht * lax.rsqrt(jnp.mean(ht * ht, axis=0, keepdims=True) + EPS)
            hn = ht.T * gn_ref[:, cols]
            y = _sigmoid(mo_ref[bb, :, cols].astype(F32)) * (hn + sk_ref[:, cols] * cs[:, cols])
            y_ref[bb, :, cols] = y.astype(BF16)

            g_r = b_last - b_r + ig_r
            m_loc = jnp.max(g_r, axis=1, keepdims=True)
            wk_r = jnp.exp(g_r - m_loc)
            c_loc = _dot((vt.astype(F32) * wk_r).astype(BF16), kb)
            n_loc = _dot(jnp.broadcast_to(wk_r, (8, L)).astype(BF16), kb)
            m_new = jnp.maximum(b_last + m_prev, m_loc)
            a = jnp.exp(b_last + m_prev - m_new)
            cc = jnp.exp(m_loc - m_new)
            c_scr[st] = a * ct_prev + cc * c_loc
            n_scr[st] = a * n_prev + cc * n_loc
            m_scr[st:st + 1, :] = jnp.broadcast_to(m_new, (1, LANES))


def _mlstm(mu, mvt, mo, gt, gtt, conv_w, conv_b, w_q_m, w_k_m, g_mhn, skip_m):
    B, S, W = mu.shape
    L = M_CHUNK
    nc = S // L
    nb = M_BATCH
    tok = pl.BlockSpec((nb, L, W), lambda b, c: (b, c, 0))
    const = lambda shape: pl.BlockSpec(shape, lambda b, c: (0,) * len(shape))
    wqk = jnp.concatenate([w_q_m, w_k_m], axis=-1).astype(BF16)
    return pl.pallas_call(
        _mlstm_kernel,
        grid=(B // nb, nc),
        in_specs=[tok, pl.BlockSpec((nb, None, W, L), lambda b, c: (b, c, 0, 0)), tok,
                  pl.BlockSpec((nb, L, LANES), lambda b, c: (b, c, 0)),
                  pl.BlockSpec((nb, 2 * M_HEADS, L), lambda b, c: (b, 0, c)),
                  const((CONV_WIDTH, W)), const((1, W)),
                  const((M_HEADS, M_HEAD_DIM, 2 * M_HEAD_DIM)), const((1, W)), const((1, W))],
        out_specs=tok,
        out_shape=jax.ShapeDtypeStruct((B, S, W), BF16),
        scratch_shapes=[pltpu.VMEM((nb * M_HEADS, M_HEAD_DIM, M_HEAD_DIM), F32),
                        pltpu.VMEM((nb * M_HEADS, 8, M_HEAD_DIM), F32),
                        pltpu.VMEM((nb * M_HEADS, LANES), F32),
                        pltpu.VMEM((nb, 8 + L, W), F32)],
        compiler_params=_cparams(("arbitrary", "arbitrary")),
        name="mlstm",
    )(mu, mvt.reshape(B, nc, W, L), mo, gt, gtt, conv_w.astype(F32), conv_b.reshape(1, W).astype(F32), wqk,
      g_mhn.reshape(1, W).astype(F32), skip_m.reshape(1, W).astype(F32))


def _post_mix_kernel(x_ref, o1_ref, o4_ref, o16_ref, l1_ref, l4_ref, l16_ref, ym_ref, wo_ref, gc_ref,
                     wqx_ref, km_ref, vm_ref, wox_ref, gf_ref, wr_ref, br_ref,
                     x2_ref, hx_ref, route_ref, cnt_ref,
                     run_scr, o4_scr, o16_scr, l4_scr, l16_scr, ya_scr, ox_scr, *, tm):
    @pl.when(pl.program_id(0) == 0)
    def _():
        run_scr[...] = jnp.zeros_like(run_scr)

    rows = tm // CLASSES
    for src, dst in ((o4_ref, o4_scr), (o16_ref, o16_scr)):
        for r in range(CLASSES):
            for g in range(ATT_WIDTH // LANES):
                dst[g, pl.ds(r, rows, stride=CLASSES), :] = src[r, :, g * LANES:(g + 1) * LANES].astype(F32)
    for src, dst in ((l4_ref, l4_scr), (l16_ref, l16_scr)):
        for r in range(CLASSES):
            dst[pl.ds(r, rows, stride=CLASSES), :] = src[r]

    lane1 = lax.broadcasted_iota(jnp.int32, (1, LANES), 1)
    low = lane1 < ATT_HEAD_DIM

    l1, l2, l3 = l1_ref[...], l4_scr[...], l16_scr[...]
    mx = jnp.maximum(jnp.maximum(l1, l2), l3)
    e1, e2, e3 = jnp.exp(l1 - mx), jnp.exp(l2 - mx), jnp.exp(l3 - mx)
    inv = 1.0 / (e1 + e2 + e3)
    wts = (e1 * inv, e2 * inv, e3 * inv)
    for g in range(ATT_WIDTH // LANES):
        cols = slice(g * LANES, (g + 1) * LANES)
        o_slabs = (o1_ref[:, cols].astype(F32), o4_scr[g], o16_scr[g])
        ya = jnp.zeros((tm, LANES), F32)
        for c in range(3):
            w = jnp.where(low, wts[c][:, 2 * g:2 * g + 1], wts[c][:, 2 * g + 1:2 * g + 2])
            ya = ya + w * o_slabs[c]
        ya_scr[:, cols] = ya.astype(BF16)
    x1 = x_ref[...] + (_dot(ya_scr[...], wo_ref[:ATT_WIDTH, :]) + _dot(ym_ref[...], wo_ref[ATT_WIDTH:, :]))

    h2 = _rms(x1, gc_ref[...]).astype(BF16)
    qx = _dot(h2, wqx_ref[...]).astype(BF16)
    for g in range(X_WIDTH // LANES):
        cols = slice(g * LANES, (g + 1) * LANES)
        qs = qx[:, cols]
        ks = km_ref[0, :, cols]
        vs = vm_ref[0, :, cols]
        acc = jnp.zeros((tm, LANES), F32)
        for hh in range(2):
            hm = low if hh == 0 else jnp.logical_not(low)
            s = _dot_nt(jnp.where(hm, qs, jnp.zeros_like(qs)), ks)
            m = jnp.max(s, axis=-1, keepdims=True)
            p = jnp.exp(s - m)
            l = jnp.sum(p, axis=-1, keepdims=True)
            acc = acc + _dot(p.astype(BF16), jnp.where(hm, vs, jnp.zeros_like(vs))) * (1.0 / l)
        ox_scr[:, cols] = acc.astype(BF16)
    x2 = x1 + _dot(ox_scr[...], wox_ref[...])
    x2_ref[...] = x2

    h3 = _rms(x2, gf_ref[...])
    h_hi, h_lo = _hi_lo(h3)
    t = _dot(h_hi, wr_ref[...])
    logits = t[:, :LANES] + t[:, LANES:] + _dot(h_lo, wr_ref[:, :LANES]) + br_ref[...]

    lane = lax.broadcasted_iota(jnp.int32, (tm, LANES), 1).astype(F32)
    far = float(LANES)
    gmask = lane < N_GROUPS
    gl = jnp.where(gmask, logits, NEG_INF)
    gmax = jnp.max(gl, axis=-1, keepdims=True)
    gidx = jnp.min(jnp.where(gl == gmax, lane, far), axis=-1, keepdims=True)
    gsum = jnp.sum(jnp.where(gmask, jnp.exp(gl - gmax), 0.0), axis=-1, keepdims=True)
    g_w = 1.0 / gsum
    lo_lane = N_GROUPS + gidx * EXPERTS_PER_GROUP
    emask = (lane >= lo_lane) & (lane < lo_lane + EXPERTS_PER_GROUP)
    el = jnp.where(emask, logits, NEG_INF)
    t1 = jnp.max(el, axis=-1, keepdims=True)
    i1 = jnp.min(jnp.where(el == t1, lane, far), axis=-1, keepdims=True)
    el2 = jnp.where(lane == i1, NEG_INF, el)
    t2 = jnp.max(el2, axis=-1, keepdims=True)
    i2 = jnp.min(jnp.where(el2 == t2, lane, far), axis=-1, keepdims=True)
    ee = jnp.exp(t2 - t1)
    w1 = g_w / (1.0 + ee)
    w2 = w1 * ee

    first = i1 < i2
    ia = jnp.minimum(i1, i2)
    ib = jnp.maximum(i1, i2)
    wa = jnp.where(first, w1, w2)
    wb = jnp.where(first, w2, w1)
    la = ia - lo_lane
    lb = ib - lo_lane
    pair = la * (EXPERTS_PER_GROUP - 1) - la * (la - 1.0) * 0.5 + (lb - la - 1.0)
    bucket = gidx * len(PAIRS) + pair

    hit = lane == bucket
    cnt = jnp.where(hit, 1.0, 0.0)
    ri = lax.broadcasted_iota(jnp.int32, (tm, tm), 0)
    ci = lax.broadcasted_iota(jnp.int32, (tm, tm), 1)
    before = jnp.where(ci < ri, 1.0, 0.0).astype(BF16)
    prefix = _dot(before, cnt.astype(BF16)) + run_scr[0:1, :]
    rank = jnp.sum(jnp.where(hit, prefix, 0.0), axis=-1, keepdims=True)
    total = run_scr[0:1, :] + jnp.sum(cnt, axis=0, keepdims=True)
    run_scr[...] = jnp.broadcast_to(total, run_scr.shape)
    cnt_ref[...] = jnp.broadcast_to(total, cnt_ref.shape)

    fields = (ia - N_GROUPS, ib - N_GROUPS, wa, wb, rank, bucket)
    route = jnp.zeros((tm, LANES), F32)
    for idx, val in enumerate(fields):
        route = jnp.where(lane == float(idx), val, route)
    route_ref[...] = route
    hx_ref[:, :x2_ref.shape[1]] = h3
    hx_ref[:, x2_ref.shape[1]:] = route


def _post_mix(x2d, outs, lses, y_m, w_out, g_cross, w_q_x, k_mem, v_mem, w_o_x, g_ffn,
              w_router_g, b_router_g, w_router_e, b_router_e, tm, B):
    N, D = x2d.shape
    S = N // B
    tps = S // tm
    M = k_mem.shape[1]
    wr = jnp.pad(jnp.concatenate([w_router_g, w_router_e], axis=1).astype(F32),
                 ((0, 0), (0, LANES - N_GROUPS - N_EXPERTS)))
    wr_hi = wr.astype(BF16)
    wr_cat = jnp.concatenate([wr_hi, (wr - wr_hi.astype(F32)).astype(BF16)], axis=1)
    br = jnp.pad(jnp.concatenate([b_router_g, b_router_e]).astype(F32),
                 (0, LANES - N_GROUPS - N_EXPERTS)).reshape(1, LANES)
    const = lambda shape: pl.BlockSpec(shape, lambda i: (0,) * len(shape))
    row = lambda w: pl.BlockSpec((tm, w), lambda i: (i, 0))
    cls = lambda w: pl.BlockSpec((None, CLASSES, tm // CLASSES, w), lambda i: (i // tps, 0, i % tps, 0))
    memspec = pl.BlockSpec((1, M, X_WIDTH), lambda i: (i // tps, 0, 0))
    slabs = ATT_WIDTH // LANES
    return pl.pallas_call(
        functools.partial(_post_mix_kernel, tm=tm),
        grid=(N // tm,),
        in_specs=[row(D), row(ATT_WIDTH), cls(ATT_WIDTH), cls(ATT_WIDTH), row(LANES),
                  cls(LANES), cls(LANES), row(M_WIDTH), const((D, D)), const((1, D)),
                  const((D, X_WIDTH)), memspec, memspec, const((X_WIDTH, D)), const((1, D)),
                  const((D, 2 * LANES)), const((1, LANES))],
        out_specs=[row(D), row(D + ROUTE_W), row(LANES), const((8, LANES))],
        out_shape=[jax.ShapeDtypeStruct((N, D), F32), jax.ShapeDtypeStruct((N, D + ROUTE_W), F32),
                   jax.ShapeDtypeStruct((N, LANES), F32), jax.ShapeDtypeStruct((8, LANES), F32)],
        scratch_shapes=[pltpu.VMEM((8, LANES), F32), pltpu.VMEM((slabs, tm, LANES), F32),
                        pltpu.VMEM((slabs, tm, LANES), F32), pltpu.VMEM((tm, LANES), F32),
                        pltpu.VMEM((tm, LANES), F32), pltpu.VMEM((tm, ATT_WIDTH), BF16),
                        pltpu.VMEM((tm, X_WIDTH), BF16)],
        compiler_params=_cparams(("arbitrary",)),
        name="post_mix",
    )(x2d, outs[0].reshape(N, ATT_WIDTH), outs[1], outs[2], lses[0].reshape(N, LANES), lses[1], lses[2],
      y_m, w_out.astype(BF16), g_cross.reshape(1, D), (w_q_x * (X_HEAD_DIM ** -0.5)).astype(BF16),
      k_mem, v_mem, w_o_x.astype(BF16), g_ffn.reshape(1, D), wr_cat, br)


def _row_copy(src, src_row, dst, dst_row, sem):
    return pltpu.make_async_copy(src.at[pl.ds(src_row, 1), :], dst.at[pl.ds(dst_row, 1), :], sem)


def _dispatch_kernel(pad_lo_ref, pad_hi_ref, nvalid_ref, dest_ref, h_ref, xs_ref, zero_scr, sem, zsem,
                     *, tm, nblk):
    def start(t, carry):
        _row_copy(h_ref, t, xs_ref, dest_ref[0, 0, t], sem).start()
        return carry

    def wait(t, carry):
        _row_copy(h_ref, t, xs_ref, dest_ref[0, 0, t], sem).wait()
        return carry

    lax.fori_loop(0, tm, start, 0, unroll=DMA_UNROLL)

    @pl.when(pl.program_id(0) == 0)
    def _():
        zero_scr[...] = jnp.zeros_like(zero_scr)

        def tail_copy(blk):
            return pltpu.make_async_copy(zero_scr, xs_ref.at[pl.ds(blk * MOE_ROWS, MOE_ROWS), :], zsem)

        for q in range(N_BUCKETS):
            lax.fori_loop(pad_lo_ref[q], pad_hi_ref[q],
                          lambda r, c: (_row_copy(zero_scr, 0, xs_ref, r, zsem).start(), c)[1], 0)
        lax.fori_loop(nvalid_ref[0], nblk, lambda blk, c: (tail_copy(blk).start(), c)[1], 0)
        for q in range(N_BUCKETS):
            lax.fori_loop(pad_lo_ref[q], pad_hi_ref[q],
                          lambda r, c: (_row_copy(zero_scr, 0, xs_ref, r, zsem).wait(), c)[1], 0)
        lax.fori_loop(nvalid_ref[0], nblk, lambda blk, c: (tail_copy(blk).wait(), c)[1], 0)

    lax.fori_loop(0, tm, wait, 0, unroll=DMA_UNROLL)


def _dispatch(hx, dest, pad_lo, pad_hi, nvalid, n_rows, tm):
    N, W = hx.shape
    grid_spec = pltpu.PrefetchScalarGridSpec(
        num_scalar_prefetch=3,
        grid=(N // tm,),
        in_specs=[pl.BlockSpec((1, 1, tm), lambda i, lo, hi, nv: (i, 0, 0), memory_space=pltpu.SMEM),
                  pl.BlockSpec((tm, W), lambda i, lo, hi, nv: (i, 0))],
        out_specs=pl.BlockSpec(memory_space=pl.ANY),
        scratch_shapes=[pltpu.VMEM((MOE_ROWS, W), F32), pltpu.SemaphoreType.DMA(()),
                        pltpu.SemaphoreType.DMA(())],
    )
    return pl.pallas_call(
        functools.partial(_dispatch_kernel, tm=tm, nblk=n_rows // MOE_ROWS),
        grid_spec=grid_spec,
        out_shape=jax.ShapeDtypeStruct((n_rows, W), F32),
        compiler_params=_cparams(("arbitrary",)),
        name="moe_dispatch",
    )(pad_lo, pad_hi, nvalid, dest.reshape(N // tm, 1, tm), hx)


def _expert_kernel(blk_a_ref, blk_b_ref, nvalid_ref, x_ref, w1a_ref, w3a_ref, w2a_ref, w1b_ref, w3b_ref,
                   w2b_ref, y_ref):
    del blk_a_ref, blk_b_ref
    D = y_ref.shape[1]

    def ffn(xb, w1_ref, w3_ref, w2_ref):
        a = _dot(xb, w1_ref[0])
        b = _dot(xb, w3_ref[0])
        return _dot((a * _sigmoid(a) * b).astype(BF16), w2_ref[0])

    @pl.when(pl.program_id(0) < nvalid_ref[0])
    def _():
        xb = x_ref[:, :D].astype(BF16)
        wa = x_ref[:, D + 2:D + 3]
        wb = x_ref[:, D + 3:D + 4]
        y_ref[...] = (wa * ffn(xb, w1a_ref, w3a_ref, w2a_ref)
                      + wb * ffn(xb, w1b_ref, w3b_ref, w2b_ref))

    @pl.when(pl.program_id(0) >= nvalid_ref[0])
    def _():
        y_ref[...] = jnp.zeros_like(y_ref)


def _experts(xs, blk_a, blk_b, nvalid, w1, w3, w2):
    P, W = xs.shape
    D = W - ROUTE_W
    nblk = P // MOE_ROWS
    F = w1.shape[-1]
    up_a = pl.BlockSpec((1, D, F), lambda i, ba, bb, nv: (ba[i], 0, 0))
    up_b = pl.BlockSpec((1, D, F), lambda i, ba, bb, nv: (bb[i], 0, 0))
    grid_spec = pltpu.PrefetchScalarGridSpec(
        num_scalar_prefetch=3,
        grid=(nblk,),
        in_specs=[pl.BlockSpec((MOE_ROWS, W),
                               lambda i, ba, bb, nv: (jnp.maximum(jnp.minimum(i, nv[0] - 1), 0), 0)),
                  up_a, up_a, pl.BlockSpec((1, F, D), lambda i, ba, bb, nv: (ba[i], 0, 0)),
                  up_b, up_b, pl.BlockSpec((1, F, D), lambda i, ba, bb, nv: (bb[i], 0, 0))],
        out_specs=pl.BlockSpec((MOE_ROWS, D), lambda i, ba, bb, nv: (i, 0)),
    )
    w1b, w3b, w2b = w1.astype(BF16), w3.astype(BF16), w2.astype(BF16)
    return pl.pallas_call(
        _expert_kernel,
        grid_spec=grid_spec,
        out_shape=jax.ShapeDtypeStruct((P, D), F32),
        compiler_params=_cparams(("arbitrary",)),
        name="moe_experts",
    )(blk_a, blk_b, nvalid, xs, w1b, w3b, w2b, w1b, w3b, w2b)


def _combine_kernel(dest_ref, x_ref, g_ref, ys_ref, o_ref, buf, sem, *, tm):
    def start(t, carry):
        _row_copy(ys_ref, dest_ref[0, 0, t], buf, t, sem).start()
        return carry

    def wait(t, carry):
        _row_copy(ys_ref, dest_ref[0, 0, t], buf, t, sem).wait()
        return carry

    lax.fori_loop(0, tm, start, 0, unroll=DMA_UNROLL)
    lax.fori_loop(0, tm, wait, 0, unroll=DMA_UNROLL)
    o_ref[...] = _rms(x_ref[...] + buf[...], g_ref[...])


def _combine(x2, dest, ys, g_final, tm):
    N, D = x2.shape
    return pl.pallas_call(
        functools.partial(_combine_kernel, tm=tm),
        grid=(N // tm,),
        in_specs=[pl.BlockSpec((1, 1, tm), lambda i: (i, 0, 0), memory_space=pltpu.SMEM),
                  pl.BlockSpec((tm, D), lambda i: (i, 0)),
                  pl.BlockSpec((1, D), lambda i: (0, 0)),
                  pl.BlockSpec(memory_space=pl.ANY)],
        out_specs=pl.BlockSpec((tm, D), lambda i: (i, 0)),
        out_shape=jax.ShapeDtypeStruct((N, D), F32),
        scratch_shapes=[pltpu.VMEM((tm, D), F32), pltpu.SemaphoreType.DMA(())],
        compiler_params=_cparams(("arbitrary",)),
        name="moe_combine",
    )(dest.reshape(N // tm, 1, tm), x2, g_final.reshape(1, D), ys)


def kernel(x, mem, positions, g_mix, w_in, conv_w, conv_b, w_q_m, w_k_m, b_i, b_f, g_mhn, skip_m, w_out, g_cross, g_mem, w_q_x, w_kv_x, w_o_x, g_ffn, w_router_g, b_router_g, w_router_e, b_router_e, w1, w3, w2, g_final):
    B, S, D = x.shape
    N = B * S
    depth = g_mix.shape[0]
    tm_in = 512
    tm_post = 512
    tm_moe = 512
    assert all(window // d == ATT_BLOCK and CLASSES % d == 0 for window, d in DILATED_CONFIGS)
    assert B % M_BATCH == 0 and S % tm_in == 0 and S % tm_post == 0 and N % tm_moe == 0
    assert depth == 1
    for l in range(depth):
        x2d = x.reshape(N, D)
        pos = positions.astype(F32).reshape(N, 1)
        k_mem, v_mem = _mem_kv(mem, g_mem[l], w_kv_x[l])
        q, k, v, qc, kc, vc, mu, mvt, mo, gt, gtt = _in_proj(
            x2d, pos, g_mix[l], w_in[l], b_i[l], b_f[l], tm_in, B)
        q, k, v = (t.reshape(B, 1, S, ATT_WIDTH) for t in (q, k, v))
        outs, lses = zip(*(_attention_config(*(qkv + (d,)))
                           for qkv, (_, d) in zip(((q, k, v), (qc, kc, vc), (qc, kc, vc)), DILATED_CONFIGS)))
        y_m = _mlstm(mu.reshape(B, S, M_WIDTH), mvt, mo.reshape(B, S, M_WIDTH),
                     gt.reshape(B, S, LANES), gtt, conv_w[l], conv_b[l], w_q_m[l], w_k_m[l],
                     g_mhn[l], skip_m[l]).reshape(N, M_WIDTH)
        x2, hx, route, cnt = _post_mix(x2d, outs, lses, y_m, w_out[l], g_cross[l], w_q_x[l], k_mem,
                                       v_mem, w_o_x[l], g_ffn[l], w_router_g[l], b_router_g[l],
                                       w_router_e[l], b_router_e[l], tm_post, B)

        rank = route[:, 4].astype(jnp.int32)
        bucket = route[:, 5].astype(jnp.int32)
        counts = cnt[0, :N_BUCKETS].astype(jnp.int32)
        padded = ((counts + MOE_ROWS - 1) // MOE_ROWS) * MOE_ROWS
        pends = jnp.cumsum(padded)
        pstarts = pends - padded
        onehot = bucket[:, None] == jnp.arange(N_BUCKETS, dtype=jnp.int32)
        dest = jnp.sum(jnp.where(onehot, pstarts, 0), axis=-1) + rank
        n_rows = N + N_BUCKETS * MOE_ROWS
        nblk = n_rows // MOE_ROWS
        blk_start = jnp.arange(nblk, dtype=jnp.int32) * MOE_ROWS
        blk_bucket = jnp.minimum(jnp.sum(pends[None, :] <= blk_start[:, None], axis=1), N_BUCKETS - 1)
        base = (np.arange(N_BUCKETS) // len(PAIRS)) * EXPERTS_PER_GROUP
        expert_a = jnp.asarray(base + np.array([p[0] for p in PAIRS] * N_GROUPS), jnp.int32)
        expert_b = jnp.asarray(base + np.array([p[1] for p in PAIRS] * N_GROUPS), jnp.int32)
        blk_a = jnp.take(expert_a, blk_bucket).astype(jnp.int32)
        blk_b = jnp.take(expert_b, blk_bucket).astype(jnp.int32)
        nvalid = (pends[-1] // MOE_ROWS).reshape(1).astype(jnp.int32)

        xs = _dispatch(hx, dest, (pstarts + counts).astype(jnp.int32), pends.astype(jnp.int32),
                       nvalid, n_rows, tm_moe)
        ys = _experts(xs, blk_a, blk_b, nvalid, w1[l], w3[l], w2[l])
        x = _combine(x2, dest, ys, g_final, tm_moe).reshape(B, S, D)
    return x
```

```python
import functools

import jax
import jax.numpy as jnp
import numpy as np
from jax import lax
from jax.experimental import pallas as pl
from jax.experimental.pallas import tpu as pltpu

F32 = jnp.float32
BF16 = jnp.bfloat16

EPS = 1e-6
LANES = 128
SUBLANES = 8
ATT_HEAD_DIM = 64
ATT_WIDTH = 512
DILATED_CONFIGS = ((128, 1), (512, 4), (2048, 16))
CLASSES = 16
ATT_BLOCK = 128
ROPE_THETA = 500000.0
ROPE_DIM = ATT_HEAD_DIM // 4
M_WIDTH = 512
M_HEADS = 4
M_HEAD_DIM = 128
CONV_WIDTH = 4
M_CHUNK = 128
M_BATCH = 2
X_HEADS = 4
X_HEAD_DIM = 64
X_WIDTH = X_HEADS * X_HEAD_DIM
N_GROUPS = 4
EXPERTS_PER_GROUP = 4
N_EXPERTS = 16
TOP_K = 2
EXPERT_FF = 512
PAIRS = tuple((a, b) for a in range(EXPERTS_PER_GROUP) for b in range(a + 1, EXPERTS_PER_GROUP))
N_BUCKETS = N_GROUPS * len(PAIRS)
MOE_ROWS = 512
ROUTE_W = LANES
VMEM_LIMIT = 56 * 1024 * 1024

NEG_INF = float("-inf")


def _cparams(sem):
    return pltpu.CompilerParams(dimension_semantics=sem, vmem_limit_bytes=VMEM_LIMIT)


def _rms(x, g):
    return x * lax.rsqrt(jnp.mean(x * x, axis=-1, keepdims=True) + EPS) * g


def _dot(a, b):
    return jnp.dot(a, b, preferred_element_type=F32)


def _dot_nt(a, b):
    return lax.dot_general(a, b, (((1,), (1,)), ((), ())), preferred_element_type=F32)


def _dot_tn(a, b):
    return lax.dot_general(a, b, (((0,), (0,)), ((), ())), preferred_element_type=F32)


def _hi_lo(a):
    hi = a.astype(BF16)
    return hi, (a - hi.astype(F32)).astype(BF16)


def _log_sigmoid(x):
    return jnp.minimum(x, 0.0) - jnp.log(1.0 + jnp.exp(-jnp.abs(x)))


def _sigmoid(x):
    return 0.5 * jnp.tanh(0.5 * x) + 0.5


def _mem_kv_kernel(mem_ref, g_ref, w_ref, k_ref, v_ref):
    h = _rms(mem_ref[0], g_ref[...]).astype(BF16)
    kv = _dot(h, w_ref[...])
    k_ref[0] = kv[:, :X_WIDTH].astype(BF16)
    v_ref[0] = kv[:, X_WIDTH:].astype(BF16)


def _mem_kv(mem, g_mem, w_kv):
    B, M, D = mem.shape
    return pl.pallas_call(
        _mem_kv_kernel,
        grid=(B,),
        in_specs=[pl.BlockSpec((1, M, D), lambda b: (b, 0, 0)),
                  pl.BlockSpec((1, D), lambda b: (0, 0)),
                  pl.BlockSpec((D, 2 * X_WIDTH), lambda b: (0, 0))],
        out_specs=[pl.BlockSpec((1, M, X_WIDTH), lambda b: (b, 0, 0)),
                   pl.BlockSpec((1, M, X_WIDTH), lambda b: (b, 0, 0))],
        out_shape=[jax.ShapeDtypeStruct((B, M, X_WIDTH), BF16)] * 2,
        compiler_params=_cparams(("arbitrary",)),
        name="mem_kv",
    )(mem, g_mem.reshape(1, D), w_kv.astype(BF16))


def _in_proj_kernel(x_ref, pos_ref, g_ref, wqkv_ref, wm_ref, wmvt_ref, wgt_ref, bgt_ref,
                    invf_ref, sgn_ref, perm_ref, q_ref, k_ref, v_ref, qc_ref, kc_ref, vc_ref,
                    mu_ref, mvt_ref, mo_ref, gtt_ref, *, tm):
    hb = _rms(x_ref[...], g_ref[...]).astype(BF16)
    qkv = _dot(hb, wqkv_ref[...])
    ang = pos_ref[...] * invf_ref[...]
    cos = jnp.cos(ang)
    sin = jnp.sin(ang)
    s_lo = sin * sgn_ref[0:1, :]
    s_hi = sin * sgn_ref[1:2, :]
    half = ROPE_DIM // 2

    def emit_classes(nat_ref, cls_ref):
        rows = tm // CLASSES
        by_class = _dot(perm_ref[...], nat_ref[...])
        for r in range(CLASSES):
            cls_ref[0, r] = by_class[r * rows:(r + 1) * rows].astype(BF16)

    for which, refs in ((0, (q_ref, qc_ref)), (1, (k_ref, kc_ref))):
        for g in range(ATT_WIDTH // LANES):
            cols = slice(g * LANES, (g + 1) * LANES)
            t = qkv[:, which * ATT_WIDTH + g * LANES: which * ATT_WIDTH + (g + 1) * LANES]
            refs[0][:, cols] = (t * cos + pltpu.roll(t, LANES - half, 1) * s_lo
                                + pltpu.roll(t, half, 1) * s_hi).astype(BF16)
        emit_classes(*refs)
    v_ref[...] = qkv[:, 2 * ATT_WIDTH:].astype(BF16)
    emit_classes(v_ref, vc_ref)
    mm = _dot(hb, wm_ref[...])
    mu_ref[...] = mm[:, :M_WIDTH].astype(BF16)
    mo_ref[...] = mm[:, M_WIDTH:].astype(BF16)
    mvt = _dot_nt(wmvt_ref[...], hb)
    for c in range(tm // M_CHUNK):
        mvt_ref[c] = mvt[:, c * M_CHUNK:(c + 1) * M_CHUNK].astype(BF16)
    gtt_ref[0] = _dot_nt(wgt_ref[...], hb) + bgt_ref[...]


def _in_proj(x2d, pos, g_mix, w_in, b_i, b_f, tm, B):
    N, D = x2d.shape
    S = N // B
    tps = S // tm
    A = ATT_WIDTH
    wq = w_in[:, :A] * (ATT_HEAD_DIM ** -0.5)
    wqkv = jnp.concatenate([wq, w_in[:, A:3 * A]], axis=1).astype(BF16)
    o = 3 * A
    wm = jnp.concatenate([w_in[:, o:o + M_WIDTH], w_in[:, o + 2 * M_WIDTH:o + 3 * M_WIDTH]],
                         axis=1).astype(BF16)
    wmvt = w_in[:, o + M_WIDTH:o + 2 * M_WIDTH].T.astype(BF16)
    wgates = w_in[:, 3 * A + 3 * M_WIDTH:]
    wgt = wgates.T.astype(BF16)
    bgt = jnp.concatenate([b_i, b_f]).astype(F32).reshape(2 * M_HEADS, 1)
    j = np.arange(LANES) % ATT_HEAD_DIM
    inv_freq = ROPE_THETA ** (-jnp.arange(0, ROPE_DIM, 2, dtype=F32) / ROPE_DIM)
    invf = jnp.where(j < ROPE_DIM, inv_freq[j % (ROPE_DIM // 2)], 0.0).reshape(1, LANES).astype(F32)
    sgn = np.zeros((8, LANES), np.float32)
    sgn[0] = np.where(j < ROPE_DIM // 2, -1.0, 0.0)
    sgn[1] = np.where((j >= ROPE_DIM // 2) & (j < ROPE_DIM), 1.0, 0.0)
    tok = np.arange(tm)
    perm = np.zeros((tm, tm), np.float32)
    perm[(tok % CLASSES) * (tm // CLASSES) + tok // CLASSES, tok] = 1.0
    const = lambda shape: pl.BlockSpec(shape, lambda i: (0,) * len(shape))
    row = lambda w: pl.BlockSpec((tm, w), lambda i: (i, 0))
    cls = pl.BlockSpec((1, CLASSES, tm // CLASSES, A), lambda i: (i // tps, 0, i % tps, 0))
    cls_shape = jax.ShapeDtypeStruct((B, CLASSES, S // CLASSES, A), BF16)
    return pl.pallas_call(
        functools.partial(_in_proj_kernel, tm=tm),
        grid=(N // tm,),
        in_specs=[row(D), row(1), const((1, D)), const((D, 3 * A)), const((D, 2 * M_WIDTH)),
                  const((M_WIDTH, D)), const((2 * M_HEADS, D)), const((2 * M_HEADS, 1)),
                  const((1, LANES)), const((8, LANES)), const((tm, tm))],
        out_specs=[row(A), row(A), row(A), cls, cls, cls, row(M_WIDTH),
                   pl.BlockSpec((tm // M_CHUNK, M_WIDTH, M_CHUNK), lambda i: (i, 0, 0)), row(M_WIDTH),
                   pl.BlockSpec((1, 2 * M_HEADS, tm), lambda i: (i // tps, 0, i % tps))],
        out_shape=[jax.ShapeDtypeStruct((N, A), BF16)] * 3 + [cls_shape] * 3
        + [jax.ShapeDtypeStruct((N, M_WIDTH), BF16),
           jax.ShapeDtypeStruct((N // M_CHUNK, M_WIDTH, M_CHUNK), BF16),
           jax.ShapeDtypeStruct((N, M_WIDTH), BF16),
           jax.ShapeDtypeStruct((B, 2 * M_HEADS, S), F32)],
        compiler_params=_cparams(("arbitrary",)),
        name="in_proj",
    )(x2d, pos, g_mix.reshape(1, D), wqkv, wm, wmvt, wgt, bgt, invf, jnp.asarray(sgn),
      jnp.asarray(perm, BF16))


def _attn_kernel(q_ref, kc_ref, kp_ref, vc_ref, vp_ref, o_ref, l_ref, kbuf, vbuf, bias_scr, *, qb, nc):
    blk = ATT_BLOCK
    piece = blk // nc
    nsub = qb // blk
    first = pl.program_id(2) == 0
    for c in range(nc):
        kbuf[c * piece:(c + 1) * piece, :] = kp_ref[c]
        vbuf[c * piece:(c + 1) * piece, :] = vp_ref[c]
        for sub in range(nsub):
            dst = slice((sub + 1) * blk + c * piece, (sub + 1) * blk + (c + 1) * piece)
            src = slice(sub * piece, (sub + 1) * piece)
            kbuf[dst, :] = kc_ref[c, src, :]
            vbuf[dst, :] = vc_ref[c, src, :]

    def pos(p):
        p = p & (blk - 1)
        return nc * (p & (piece - 1)) + (p >> (piece.bit_length() - 1))

    qi = lax.broadcasted_iota(jnp.int32, (blk, 2 * blk), 0)
    ki = lax.broadcasted_iota(jnp.int32, (blk, 2 * blk), 1)
    dist = pos(qi) - pos(ki) + jnp.where(ki < blk, blk, 0)
    band = (dist >= 0) & (dist <= blk)
    band_first = band & ((ki >= blk) | jnp.logical_not(first))
    bias_scr[0] = jnp.where(band_first, 0.0, NEG_INF)
    bias_scr[1] = jnp.where(band, 0.0, NEG_INF)
    lane = lax.broadcasted_iota(jnp.int32, (1, LANES), 1)
    lane_full = lax.broadcasted_iota(jnp.int32, (blk, LANES), 1)
    low = lane < ATT_HEAD_DIM
    for sub in range(nsub):
        bias = bias_scr.at[min(sub, 1)]
        prow = slice(sub * piece, (sub + 1) * piece)
        krow = slice(sub * blk, (sub + 2) * blk)
        lse_all = jnp.zeros((blk, LANES), F32)
        for g in range(ATT_WIDTH // LANES):
            cols = slice(g * LANES, (g + 1) * LANES)
            qs = jnp.concatenate([q_ref[c, prow, cols] for c in range(nc)], axis=0)
            ks = kbuf[krow, cols]
            vs = vbuf[krow, cols]
            acc = jnp.zeros((blk, LANES), F32)
            for hh in range(2):
                hm = low if hh == 0 else jnp.logical_not(low)
                s = _dot_nt(jnp.where(hm, qs, jnp.zeros_like(qs)), ks) + bias[...]
                m = jnp.max(s, axis=-1, keepdims=True)
                p = jnp.exp(s - m)
                l = jnp.sum(p, axis=-1, keepdims=True)
                pv = _dot(p.astype(BF16), jnp.where(hm, vs, jnp.zeros_like(vs)))
                acc = acc + pv * (1.0 / l)
                lse_all = jnp.where(lane_full == 2 * g + hh, m + jnp.log(l), lse_all)
            acc = acc.astype(BF16)
            for c in range(nc):
                o_ref[c, prow, cols] = acc[c * piece:(c + 1) * piece]
        for c in range(nc):
            l_ref[c, prow, :] = lse_all[c * piece:(c + 1) * piece]


def _attention_config(q, k, v, d):
    B, C, L, W = q.shape
    nc = C // d
    qb = min(512, L * nc)
    rows = qb // nc
    piece = ATT_BLOCK // nc
    nsub = qb // ATT_BLOCK
    view = lambda t: t.reshape(B, nc, d, L, t.shape[-1])
    cur = lambda w: pl.BlockSpec((None, nc, None, rows, w), lambda b, r, j: (b, 0, r, j, 0))
    prev = pl.BlockSpec((None, nc, None, piece, W),
                        lambda b, r, j: (b, 0, r, jnp.maximum(j * nsub - 1, 0), 0))
    o, lse = pl.pallas_call(
        functools.partial(_attn_kernel, qb=qb, nc=nc),
        grid=(B, d, L // rows),
        in_specs=[cur(W), cur(W), prev, cur(W), prev],
        out_specs=[cur(W), cur(LANES)],
        out_shape=[jax.ShapeDtypeStruct((B, nc, d, L, W), BF16),
                   jax.ShapeDtypeStruct((B, nc, d, L, LANES), F32)],
        scratch_shapes=[pltpu.VMEM((qb + ATT_BLOCK, W), BF16), pltpu.VMEM((qb + ATT_BLOCK, W), BF16),
                        pltpu.VMEM((2, ATT_BLOCK, 2 * ATT_BLOCK), F32)],
        compiler_params=_cparams(("arbitrary", "arbitrary", "arbitrary")),
        name=f"attention_d{d}",
    )(view(q), view(k), view(k), view(v), view(v))
    return o.reshape(B, C, L, W), lse.reshape(B, C, L, LANES)


def _mlstm_kernel(mu_ref, mvt_ref, mo_ref, gtt_ref, cw_ref, cb_ref, wqk_ref, gn_ref,
                  sk_ref, y_ref, c_scr, n_scr, m_scr, ext_scr):
    L = M_CHUNK
    H = M_HEADS
    HD = M_HEAD_DIM

    @pl.when(pl.program_id(1) == 0)
    def _():
        c_scr[...] = jnp.zeros_like(c_scr)
        n_scr[...] = jnp.zeros_like(n_scr)
        m_scr[...] = jnp.zeros_like(m_scr)
        for bb in range(M_BATCH):
            ext_scr[bb, 0:8, :] = jnp.zeros((8, M_WIDTH), F32)

    ri = lax.broadcasted_iota(jnp.int32, (L, L), 0)
    ci = lax.broadcasted_iota(jnp.int32, (L, L), 1)
    causal_t = ri <= ci
    triu = jnp.where(causal_t, 1.0, 0.0).astype(BF16)
    scale = HD ** -0.5

    for bb in range(M_BATCH):
        mu = mu_ref[bb].astype(F32)
        ext_scr[bb, 8:8 + L, :] = mu
        conv = cb_ref[...]
        for jj in range(CONV_WIDTH):
            lo = 8 - (CONV_WIDTH - 1) + jj
            conv = conv + ext_scr[bb, lo:lo + L, :] * cw_ref[jj:jj + 1, :]
        ext_scr[bb, 0:8, :] = mu[L - 8:, :]
        cs = conv * _sigmoid(conv)
        cb16 = cs.astype(BF16)

        gtt = gtt_ref[bb]
        hi_r, lo_r = _hi_lo(_log_sigmoid(gtt))
        b_rows = _dot(hi_r, triu) + _dot(lo_r, triu)
        key_rows = gtt - pltpu.roll(b_rows, H, 0)
        key_cols = jnp.concatenate([key_rows, jnp.zeros((L - 2 * H, L), F32)], axis=0).T

        for hd in range(H):
            cols = slice(hd * HD, (hd + 1) * HD)
            st = bb * H + hd
            qk = _dot(cb16[:, cols], wqk_ref[hd])
            qb = qk[:, :HD].astype(BF16)
            kb = (qk[:, HD:] * scale).astype(BF16)
            vt = mvt_ref[bb, cols, :]
            ig_r = gtt[hd:hd + 1, :]
            b_r = b_rows[H + hd:H + hd + 1, :]
            key_c = key_cols[:, hd:hd + 1]
            b_last = b_r[:, L - 1:L]
            m_prev = m_scr[st:st + 1, 0:1]
            ct_prev = c_scr[st]
            n_prev = n_scr[st]

            dlog = jnp.where(causal_t, b_r + key_c, NEG_INF)
            m_intra = jnp.max(dlog, axis=0, keepdims=True)
            m_inter = b_r + m_prev
            m_t = jnp.maximum(m_inter, m_intra)
            inter_w = jnp.exp(m_inter - m_t)
            st_w = _dot_nt(kb, qb) * jnp.exp(dlog - m_t)
            num = _dot(vt, st_w.astype(BF16)) + inter_w * _dot_nt(ct_prev.astype(BF16), qb)
            den = (jnp.sum(st_w, axis=0, keepdims=True)
                   + inter_w * _dot_nt(n_prev.astype(BF16), qb)[0:1, :])
            ht = num * (1.0 / jnp.maximum(jnp.abs(den), jnp.exp(-m_t)))
            ht = ht * lax.rsqrt(jnp.mean(ht * ht, axis=0, keepdims=True) + EPS)
            hn = ht.T * gn_ref[:, cols]
            y = _sigmoid(mo_ref[bb, :, cols].astype(F32)) * (hn + sk_ref[:, cols] * cs[:, cols])
            y_ref[bb, :, cols] = y.astype(BF16)

            g_r = b_last - b_r + ig_r
            m_loc = jnp.max(g_r, axis=1, keepdims=True)
            wk_r = jnp.exp(g_r - m_loc)
            c_loc = _dot((vt.astype(F32) * wk_r).astype(BF16), kb)
            n_loc = _dot(jnp.broadcast_to(wk_r, (8, L)).astype(BF16), kb)
            m_new = jnp.maximum(b_last + m_prev, m_loc)
            a = jnp.exp(b_last + m_prev - m_new)
            cc = jnp.exp(m_loc - m_new)
            c_scr[st] = a * ct_prev + cc * c_loc
            n_scr[st] = a * n_prev + cc * n_loc
            m_scr[st:st + 1, :] = jnp.broadcast_to(m_new, (1, LANES))


def _mlstm(mu, mvt, mo, gtt, conv_w, conv_b, w_q_m, w_k_m, g_mhn, skip_m):
    B, S, W = mu.shape
    L = M_CHUNK
    nc = S // L
    nb = M_BATCH
    tok = pl.BlockSpec((nb, L, W), lambda b, c: (b, c, 0))
    const = lambda shape: pl.BlockSpec(shape, lambda b, c: (0,) * len(shape))
    wqk = jnp.concatenate([w_q_m, w_k_m], axis=-1).astype(BF16)
    return pl.pallas_call(
        _mlstm_kernel,
        grid=(B // nb, nc),
        in_specs=[tok, pl.BlockSpec((nb, None, W, L), lambda b, c: (b, c, 0, 0)), tok,
                  pl.BlockSpec((nb, 2 * M_HEADS, L), lambda b, c: (b, 0, c)),
                  const((CONV_WIDTH, W)), const((1, W)),
                  const((M_HEADS, M_HEAD_DIM, 2 * M_HEAD_DIM)), const((1, W)), const((1, W))],
        out_specs=tok,
        out_shape=jax.ShapeDtypeStruct((B, S, W), BF16),
        scratch_shapes=[pltpu.VMEM((nb * M_HEADS, M_HEAD_DIM, M_HEAD_DIM), F32),
                        pltpu.VMEM((nb * M_HEADS, 8, M_HEAD_DIM), F32),
                        pltpu.VMEM((nb * M_HEADS, LANES), F32),
                        pltpu.VMEM((nb, 8 + L, W), F32)],
        compiler_params=_cparams(("arbitrary", "arbitrary")),
        name="mlstm",
    )(mu, mvt.reshape(B, nc, W, L), mo, gtt, conv_w.astype(F32), conv_b.reshape(1, W).astype(F32), wqk,
      g_mhn.reshape(1, W).astype(F32), skip_m.reshape(1, W).astype(F32))


def _post_mix_kernel(x_ref, o1_ref, o4_ref, o16_ref, l1_ref, l4_ref, l16_ref, ym_ref, wo_ref, gc_ref,
                     wqx_ref, km_ref, vm_ref, wox_ref, gf_ref, wr_ref, br_ref, before_ref,
                     x2_ref, hx_ref, route_ref, cnt_ref,
                     run_scr, o4_scr, o16_scr, l4_scr, l16_scr, ya_scr, ox_scr, *, tm):
    @pl.when(pl.program_id(0) == 0)
    def _():
        run_scr[...] = jnp.zeros_like(run_scr)

    rows = tm // CLASSES
    for src, dst in ((o4_ref, o4_scr), (o16_ref, o16_scr)):
        for r in range(CLASSES):
            for g in range(ATT_WIDTH // LANES):
                dst[g, pl.ds(r, rows, stride=CLASSES), :] = src[r, :, g * LANES:(g + 1) * LANES].astype(F32)
    for src, dst in ((l4_ref, l4_scr), (l16_ref, l16_scr)):
        for r in range(CLASSES):
            dst[pl.ds(r, rows, stride=CLASSES), :] = src[r]

    lane1 = lax.broadcasted_iota(jnp.int32, (1, LANES), 1)
    low = lane1 < ATT_HEAD_DIM

    l1, l2, l3 = l1_ref[...], l4_scr[...], l16_scr[...]
    mx = jnp.maximum(jnp.maximum(l1, l2), l3)
    e1, e2, e3 = jnp.exp(l1 - mx), jnp.exp(l2 - mx), jnp.exp(l3 - mx)
    inv = 1.0 / (e1 + e2 + e3)
    wts = (e1 * inv, e2 * inv, e3 * inv)
    for g in range(ATT_WIDTH // LANES):
        cols = slice(g * LANES, (g + 1) * LANES)
        o_slabs = (o1_ref[:, cols].astype(F32), o4_scr[g], o16_scr[g])
        ya = jnp.zeros((tm, LANES), F32)
        for c in range(3):
            w = jnp.where(low, wts[c][:, 2 * g:2 * g + 1], wts[c][:, 2 * g + 1:2 * g + 2])
            ya = ya + w * o_slabs[c]
        ya_scr[:, cols] = ya.astype(BF16)
    x1 = x_ref[...] + (_dot(ya_scr[...], wo_ref[:ATT_WIDTH, :]) + _dot(ym_ref[...], wo_ref[ATT_WIDTH:, :]))

    h2 = _rms(x1, gc_ref[...]).astype(BF16)
    qx = _dot(h2, wqx_ref[...]).astype(BF16)
    for g in range(X_WIDTH // LANES):
        cols = slice(g * LANES, (g + 1) * LANES)
        qs = qx[:, cols]
        ks = km_ref[0, :, cols]
        vs = vm_ref[0, :, cols]
        acc = jnp.zeros((tm, LANES), F32)
        for hh in range(2):
            hm = low if hh == 0 else jnp.logical_not(low)
            s = _dot_nt(jnp.where(hm, qs, jnp.zeros_like(qs)), ks)
            m = jnp.max(s, axis=-1, keepdims=True)
            p = jnp.exp(s - m)
            l = jnp.sum(p, axis=-1, keepdims=True)
            acc = acc + _dot(p.astype(BF16), jnp.where(hm, vs, jnp.zeros_like(vs))) * (1.0 / l)
        ox_scr[:, cols] = acc.astype(BF16)
    x2 = x1 + _dot(ox_scr[...], wox_ref[...])
    x2_ref[...] = x2

    h3 = _rms(x2, gf_ref[...])
    h_hi, h_lo = _hi_lo(h3)
    t = _dot(h_hi, wr_ref[...])
    logits = t[:, :LANES] + t[:, LANES:] + _dot(h_lo, wr_ref[:, :LANES]) + br_ref[...]

    lane = lax.broadcasted_iota(jnp.int32, (tm, LANES), 1).astype(F32)
    far = float(LANES)
    gmask = lane < N_GROUPS
    gl = jnp.where(gmask, logits, NEG_INF)
    gmax = jnp.max(gl, axis=-1, keepdims=True)
    gidx = jnp.min(jnp.where(gl == gmax, lane, far), axis=-1, keepdims=True)
    gsum = jnp.sum(jnp.where(gmask, jnp.exp(gl - gmax), 0.0), axis=-1, keepdims=True)
    g_w = 1.0 / gsum
    lo_lane = N_GROUPS + gidx * EXPERTS_PER_GROUP
    emask = (lane >= lo_lane) & (lane < lo_lane + EXPERTS_PER_GROUP)
    el = jnp.where(emask, logits, NEG_INF)
    t1 = jnp.max(el, axis=-1, keepdims=True)
    i1 = jnp.min(jnp.where(el == t1, lane, far), axis=-1, keepdims=True)
    el2 = jnp.where(lane == i1, NEG_INF, el)
    t2 = jnp.max(el2, axis=-1, keepdims=True)
    i2 = jnp.min(jnp.where(el2 == t2, lane, far), axis=-1, keepdims=True)
    ee = jnp.exp(t2 - t1)
    w1 = g_w / (1.0 + ee)
    w2 = w1 * ee

    first = i1 < i2
    ia = jnp.minimum(i1, i2)
    ib = jnp.maximum(i1, i2)
    wa = jnp.where(first, w1, w2)
    wb = jnp.where(first, w2, w1)
    la = ia - lo_lane
    lb = ib - lo_lane
    pair = la * (EXPERTS_PER_GROUP - 1) - la * (la - 1.0) * 0.5 + (lb - la - 1.0)
    bucket = gidx * len(PAIRS) + pair

    hit = lane == bucket
    cnt = jnp.where(hit, 1.0, 0.0)
    prefix = _dot(before_ref[...], cnt.astype(BF16)) + run_scr[0:1, :]
    rank = jnp.sum(jnp.where(hit, prefix, 0.0), axis=-1, keepdims=True)
    total = run_scr[0:1, :] + jnp.sum(cnt, axis=0, keepdims=True)
    run_scr[...] = jnp.broadcast_to(total, run_scr.shape)
    cnt_ref[...] = jnp.broadcast_to(total, cnt_ref.shape)

    fields = (ia - N_GROUPS, ib - N_GROUPS, wa, wb, rank, bucket)
    route = jnp.zeros((tm, LANES), F32)
    for idx, val in enumerate(fields):
        route = jnp.where(lane == float(idx), val, route)
    route_ref[...] = route
    hx_ref[:, :x2_ref.shape[1]] = h3
    hx_ref[:, x2_ref.shape[1]:] = route


def _post_mix(x2d, outs, lses, y_m, w_out, g_cross, w_q_x, k_mem, v_mem, w_o_x, g_ffn,
              w_router_g, b_router_g, w_router_e, b_router_e, tm, B):
    N, D = x2d.shape
    S = N // B
    tps = S // tm
    M = k_mem.shape[1]
    wr = jnp.pad(jnp.concatenate([w_router_g, w_router_e], axis=1).astype(F32),
                 ((0, 0), (0, LANES - N_GROUPS - N_EXPERTS)))
    wr_hi = wr.astype(BF16)
    wr_cat = jnp.concatenate([wr_hi, (wr - wr_hi.astype(F32)).astype(BF16)], axis=1)
    br = jnp.pad(jnp.concatenate([b_router_g, b_router_e]).astype(F32),
                 (0, LANES - N_GROUPS - N_EXPERTS)).reshape(1, LANES)
    const = lambda shape: pl.BlockSpec(shape, lambda i: (0,) * len(shape))
    row = lambda w: pl.BlockSpec((tm, w), lambda i: (i, 0))
    cls = lambda w: pl.BlockSpec((None, CLASSES, tm // CLASSES, w), lambda i: (i // tps, 0, i % tps, 0))
    memspec = pl.BlockSpec((1, M, X_WIDTH), lambda i: (i // tps, 0, 0))
    slabs = ATT_WIDTH // LANES
    return pl.pallas_call(
        functools.partial(_post_mix_kernel, tm=tm),
        grid=(N // tm,),
        in_specs=[row(D), row(ATT_WIDTH), cls(ATT_WIDTH), cls(ATT_WIDTH), row(LANES),
                  cls(LANES), cls(LANES), row(M_WIDTH), const((D, D)), const((1, D)),
                  const((D, X_WIDTH)), memspec, memspec, const((X_WIDTH, D)), const((1, D)),
                  const((D, 2 * LANES)), const((1, LANES)), const((tm, tm))],
        out_specs=[row(D), row(D + ROUTE_W), row(LANES), const((8, LANES))],
        out_shape=[jax.ShapeDtypeStruct((N, D), F32), jax.ShapeDtypeStruct((N, D + ROUTE_W), F32),
                   jax.ShapeDtypeStruct((N, LANES), F32), jax.ShapeDtypeStruct((8, LANES), F32)],
        scratch_shapes=[pltpu.VMEM((8, LANES), F32), pltpu.VMEM((slabs, tm, LANES), F32),
                        pltpu.VMEM((slabs, tm, LANES), F32), pltpu.VMEM((tm, LANES), F32),
                        pltpu.VMEM((tm, LANES), F32), pltpu.VMEM((tm, ATT_WIDTH), BF16),
                        pltpu.VMEM((tm, X_WIDTH), BF16)],
        compiler_params=_cparams(("arbitrary",)),
        name="post_mix",
    )(x2d, outs[0].reshape(N, ATT_WIDTH), outs[1], outs[2], lses[0].reshape(N, LANES), lses[1], lses[2],
      y_m, w_out.astype(BF16), g_cross.reshape(1, D), (w_q_x * (X_HEAD_DIM ** -0.5)).astype(BF16),
      k_mem, v_mem, w_o_x.astype(BF16), g_ffn.reshape(1, D), wr_cat, br,
      jnp.asarray(np.tril(np.ones((tm, tm), np.float32), -1), BF16))


def _tile_row(ref, row):
    return ref.at[row >> 3, pl.ds(row & (SUBLANES - 1), 1), :]


def _dispatch_kernel(pad_lo_ref, pad_hi_ref, nvalid_ref, dest_ref, h_ref, xs_ref, zero_scr, sem, zsem,
                     *, tm, nblk):
    def copy(i, u):
        return pltpu.make_async_copy(h_ref.at[i, pl.ds(u, 1), :],
                                     _tile_row(xs_ref, dest_ref[0, 0, i * SUBLANES + u]), sem)

    def start(i, carry):
        for u in range(SUBLANES):
            copy(i, u).start()
        return carry

    def wait(i, carry):
        for u in range(SUBLANES):
            copy(i, u).wait()
        return carry

    lax.fori_loop(0, tm // SUBLANES, start, 0)

    @pl.when(pl.program_id(0) == 0)
    def _():
        zero_scr[...] = jnp.zeros_like(zero_scr)
        groups = MOE_ROWS // SUBLANES

        def pad_copy(r):
            return pltpu.make_async_copy(zero_scr.at[0, pl.ds(0, 1), :], _tile_row(xs_ref, r), zsem)

        def tail_copy(blk):
            return pltpu.make_async_copy(zero_scr, xs_ref.at[pl.ds(blk * groups, groups), :, :], zsem)

        for q in range(N_BUCKETS):
            lax.fori_loop(pad_lo_ref[q], pad_hi_ref[q], lambda r, c: (pad_copy(r).start(), c)[1], 0)
        lax.fori_loop(nvalid_ref[0], nblk, lambda blk, c: (tail_copy(blk).start(), c)[1], 0)
        for q in range(N_BUCKETS):
            lax.fori_loop(pad_lo_ref[q], pad_hi_ref[q], lambda r, c: (pad_copy(r).wait(), c)[1], 0)
        lax.fori_loop(nvalid_ref[0], nblk, lambda blk, c: (tail_copy(blk).wait(), c)[1], 0)

    lax.fori_loop(0, tm // SUBLANES, wait, 0)


def _dispatch(hx, dest, pad_lo, pad_hi, nvalid, n_rows, tm):
    N, W = hx.shape
    grid_spec = pltpu.PrefetchScalarGridSpec(
        num_scalar_prefetch=3,
        grid=(N // tm,),
        in_specs=[pl.BlockSpec((1, 1, tm), lambda i, lo, hi, nv: (i, 0, 0), memory_space=pltpu.SMEM),
                  pl.BlockSpec((tm // SUBLANES, SUBLANES, W), lambda i, lo, hi, nv: (i, 0, 0))],
        out_specs=pl.BlockSpec(memory_space=pl.ANY),
        scratch_shapes=[pltpu.VMEM((MOE_ROWS // SUBLANES, SUBLANES, W), F32), pltpu.SemaphoreType.DMA(()),
                        pltpu.SemaphoreType.DMA(())],
    )
    return pl.pallas_call(
        functools.partial(_dispatch_kernel, tm=tm, nblk=n_rows // MOE_ROWS),
        grid_spec=grid_spec,
        out_shape=jax.ShapeDtypeStruct((n_rows // SUBLANES, SUBLANES, W), F32),
        compiler_params=_cparams(("arbitrary",)),
        name="moe_dispatch",
    )(pad_lo, pad_hi, nvalid, dest.reshape(N // tm, 1, tm),
      hx.reshape(N // SUBLANES, SUBLANES, W)).reshape(n_rows, W)


def _expert_kernel(blk_a_ref, blk_b_ref, nvalid_ref, x_ref, w1a_ref, w3a_ref, w2a_ref, w1b_ref, w3b_ref,
                   w2b_ref, y_ref):
    del blk_a_ref, blk_b_ref
    D = y_ref.shape[1]

    def ffn(xb, w1_ref, w3_ref, w2_ref):
        a = _dot(xb, w1_ref[0])
        b = _dot(xb, w3_ref[0])
        return _dot((a * _sigmoid(a) * b).astype(BF16), w2_ref[0])

    @pl.when(pl.program_id(0) < nvalid_ref[0])
    def _():
        xb = x_ref[:, :D].astype(BF16)
        wa = x_ref[:, D + 2:D + 3]
        wb = x_ref[:, D + 3:D + 4]
        y_ref[...] = (wa * ffn(xb, w1a_ref, w3a_ref, w2a_ref)
                      + wb * ffn(xb, w1b_ref, w3b_ref, w2b_ref))

    @pl.when(pl.program_id(0) >= nvalid_ref[0])
    def _():
        y_ref[...] = jnp.zeros_like(y_ref)


def _experts(xs, blk_a, blk_b, nvalid, w1, w3, w2):
    P, W = xs.shape
    D = W - ROUTE_W
    nblk = P // MOE_ROWS
    F = w1.shape[-1]
    up_a = pl.BlockSpec((1, D, F), lambda i, ba, bb, nv: (ba[i], 0, 0))
    up_b = pl.BlockSpec((1, D, F), lambda i, ba, bb, nv: (bb[i], 0, 0))
    grid_spec = pltpu.PrefetchScalarGridSpec(
        num_scalar_prefetch=3,
        grid=(nblk,),
        in_specs=[pl.BlockSpec((MOE_ROWS, W),
                               lambda i, ba, bb, nv: (jnp.maximum(jnp.minimum(i, nv[0] - 1), 0), 0)),
                  up_a, up_a, pl.BlockSpec((1, F, D), lambda i, ba, bb, nv: (ba[i], 0, 0)),
                  up_b, up_b, pl.BlockSpec((1, F, D), lambda i, ba, bb, nv: (bb[i], 0, 0))],
        out_specs=pl.BlockSpec((MOE_ROWS, D), lambda i, ba, bb, nv: (i, 0)),
    )
    w1b, w3b, w2b = w1.astype(BF16), w3.astype(BF16), w2.astype(BF16)
    return pl.pallas_call(
        _expert_kernel,
        grid_spec=grid_spec,
        out_shape=jax.ShapeDtypeStruct((P, D), F32),
        compiler_params=_cparams(("arbitrary",)),
        name="moe_experts",
    )(blk_a, blk_b, nvalid, xs, w1b, w3b, w2b, w1b, w3b, w2b)


def _combine_kernel(dest_ref, x_ref, g_ref, ys_ref, o_ref, buf, sem, *, tm):
    def copy(i, u):
        return pltpu.make_async_copy(_tile_row(ys_ref, dest_ref[0, 0, i * SUBLANES + u]),
                                     buf.at[i, pl.ds(u, 1), :], sem)

    def start(i, carry):
        for u in range(SUBLANES):
            copy(i, u).start()
        return carry

    def wait(i, carry):
        for u in range(SUBLANES):
            copy(i, u).wait()
        return carry

    lax.fori_loop(0, tm // SUBLANES, start, 0)
    lax.fori_loop(0, tm // SUBLANES, wait, 0)
    o_ref[...] = _rms(x_ref[...] + buf[...].reshape(o_ref.shape), g_ref[...])


def _combine(x2, dest, ys, g_final, tm):
    N, D = x2.shape
    return pl.pallas_call(
        functools.partial(_combine_kernel, tm=tm),
        grid=(N // tm,),
        in_specs=[pl.BlockSpec((1, 1, tm), lambda i: (i, 0, 0), memory_space=pltpu.SMEM),
                  pl.BlockSpec((tm, D), lambda i: (i, 0)),
                  pl.BlockSpec((1, D), lambda i: (0, 0)),
                  pl.BlockSpec(memory_space=pl.ANY)],
        out_specs=pl.BlockSpec((tm, D), lambda i: (i, 0)),
        out_shape=jax.ShapeDtypeStruct((N, D), F32),
        scratch_shapes=[pltpu.VMEM((tm // SUBLANES, SUBLANES, D), F32), pltpu.SemaphoreType.DMA(())],
        compiler_params=_cparams(("arbitrary",)),
        name="moe_combine",
    )(dest.reshape(N // tm, 1, tm), x2, g_final.reshape(1, D),
      ys.reshape(ys.shape[0] // SUBLANES, SUBLANES, D))


def kernel(x, mem, positions, g_mix, w_in, conv_w, conv_b, w_q_m, w_k_m, b_i, b_f, g_mhn, skip_m, w_out, g_cross, g_mem, w_q_x, w_kv_x, w_o_x, g_ffn, w_router_g, b_router_g, w_router_e, b_router_e, w1, w3, w2, g_final):
    B, S, D = x.shape
    N = B * S
    depth = g_mix.shape[0]
    tm_in = 512
    tm_post = 512
    tm_moe = 1024
    assert all(window // d == ATT_BLOCK and CLASSES % d == 0 for window, d in DILATED_CONFIGS)
    assert B % M_BATCH == 0 and S % tm_in == 0 and S % tm_post == 0 and N % tm_moe == 0
    assert depth == 1
    for l in range(depth):
        x2d = x.reshape(N, D)
        pos = positions.astype(F32).reshape(N, 1)
        k_mem, v_mem = _mem_kv(mem, g_mem[l], w_kv_x[l])
        q, k, v, qc, kc, vc, mu, mvt, mo, gtt = _in_proj(
            x2d, pos, g_mix[l], w_in[l], b_i[l], b_f[l], tm_in, B)
        q, k, v = (t.reshape(B, 1, S, ATT_WIDTH) for t in (q, k, v))
        outs, lses = zip(*(_attention_config(*(qkv + (d,)))
                           for qkv, (_, d) in zip(((q, k, v), (qc, kc, vc), (qc, kc, vc)), DILATED_CONFIGS)))
        y_m = _mlstm(mu.reshape(B, S, M_WIDTH), mvt, mo.reshape(B, S, M_WIDTH),
                     gtt, conv_w[l], conv_b[l], w_q_m[l], w_k_m[l],
                     g_mhn[l], skip_m[l]).reshape(N, M_WIDTH)
        x2, hx, route, cnt = _post_mix(x2d, outs, lses, y_m, w_out[l], g_cross[l], w_q_x[l], k_mem,
                                       v_mem, w_o_x[l], g_ffn[l], w_router_g[l], b_router_g[l],
                                       w_router_e[l], b_router_e[l], tm_post, B)

        rank = route[:, 4].astype(jnp.int32)
        bucket = route[:, 5].astype(jnp.int32)
        counts = cnt[0, :N_BUCKETS].astype(jnp.int32)
        padded = ((counts + MOE_ROWS - 1) // MOE_ROWS) * MOE_ROWS
        pends = jnp.cumsum(padded)
        pstarts = pends - padded
        onehot = bucket[:, None] == jnp.arange(N_BUCKETS, dtype=jnp.int32)
        dest = jnp.sum(jnp.where(onehot, pstarts, 0), axis=-1) + rank
        n_rows = N + N_BUCKETS * MOE_ROWS
        nblk = n_rows // MOE_ROWS
        blk_start = jnp.arange(nblk, dtype=jnp.int32) * MOE_ROWS
        blk_bucket = jnp.minimum(jnp.sum(pends[None, :] <= blk_start[:, None], axis=1), N_BUCKETS - 1)
        base = (np.arange(N_BUCKETS) // len(PAIRS)) * EXPERTS_PER_GROUP
        expert_a = jnp.asarray(base + np.array([p[0] for p in PAIRS] * N_GROUPS), jnp.int32)
        expert_b = jnp.asarray(base + np.array([p[1] for p in PAIRS] * N_GROUPS), jnp.int32)
        blk_a = jnp.take(expert_a, blk_bucket).astype(jnp.int32)
        blk_b = jnp.take(expert_b, blk_bucket).astype(jnp.int32)
        nvalid = (pends[-1] // MOE_ROWS).reshape(1).astype(jnp.int32)

        xs = _dispatch(hx, dest, (pstarts + counts).astype(jnp.int32), pends.astype(jnp.int32),
                       nvalid, n_rows, tm_moe)
        ys = _experts(xs, blk_a, blk_b, nvalid, w1[l], w3[l], w2[l])
        x = _combine(x2, dest, ys, g_final, tm_moe).reshape(B, S, D)
    return x
```

```python
import functools

import jax
import jax.numpy as jnp
import numpy as np
from jax import lax
from jax.experimental import pallas as pl
from jax.experimental.pallas import tpu as pltpu

F32 = jnp.float32
BF16 = jnp.bfloat16

EPS = 1e-6
LANES = 128
SUBLANES = 8
ATT_HEAD_DIM = 64
ATT_WIDTH = 512
DILATED_CONFIGS = ((128, 1), (512, 4), (2048, 16))
CLASSES = 16
ATT_BLOCK = 128
ROPE_THETA = 500000.0
ROPE_DIM = ATT_HEAD_DIM // 4
M_WIDTH = 512
M_HEADS = 4
M_HEAD_DIM = 128
CONV_WIDTH = 4
M_CHUNK = 128
M_BATCH = 4
X_HEADS = 4
X_HEAD_DIM = 64
X_WIDTH = X_HEADS * X_HEAD_DIM
N_GROUPS = 4
EXPERTS_PER_GROUP = 4
N_EXPERTS = 16
TOP_K = 2
EXPERT_FF = 512
PAIRS = tuple((a, b) for a in range(EXPERTS_PER_GROUP) for b in range(a + 1, EXPERTS_PER_GROUP))
N_BUCKETS = N_GROUPS * len(PAIRS)
MOE_ROWS = 512
ROUTE_W = LANES
VMEM_LIMIT = 56 * 1024 * 1024

NEG_INF = float("-inf")


def _cparams(sem):
    return pltpu.CompilerParams(dimension_semantics=sem, vmem_limit_bytes=VMEM_LIMIT)


def _rms(x, g):
    return x * lax.rsqrt(jnp.mean(x * x, axis=-1, keepdims=True) + EPS) * g


def _dot(a, b):
    return jnp.dot(a, b, preferred_element_type=F32)


def _dot_nt(a, b):
    return lax.dot_general(a, b, (((1,), (1,)), ((), ())), preferred_element_type=F32)


def _dot_tn(a, b):
    return lax.dot_general(a, b, (((0,), (0,)), ((), ())), preferred_element_type=F32)


def _hi_lo(a):
    hi = a.astype(BF16)
    return hi, (a - hi.astype(F32)).astype(BF16)


def _log_sigmoid(x):
    return jnp.minimum(x, 0.0) - jnp.log(1.0 + jnp.exp(-jnp.abs(x)))


def _sigmoid(x):
    return 0.5 * jnp.tanh(0.5 * x) + 0.5


def _mem_kv_kernel(mem_ref, g_ref, w_ref, k_ref, v_ref):
    h = _rms(mem_ref[0], g_ref[...]).astype(BF16)
    kv = _dot(h, w_ref[...])
    k_ref[0] = kv[:, :X_WIDTH].astype(BF16)
    v_ref[0] = kv[:, X_WIDTH:].astype(BF16)


def _mem_kv(mem, g_mem, w_kv):
    B, M, D = mem.shape
    return pl.pallas_call(
        _mem_kv_kernel,
        grid=(B,),
        in_specs=[pl.BlockSpec((1, M, D), lambda b: (b, 0, 0)),
                  pl.BlockSpec((1, D), lambda b: (0, 0)),
                  pl.BlockSpec((D, 2 * X_WIDTH), lambda b: (0, 0))],
        out_specs=[pl.BlockSpec((1, M, X_WIDTH), lambda b: (b, 0, 0)),
                   pl.BlockSpec((1, M, X_WIDTH), lambda b: (b, 0, 0))],
        out_shape=[jax.ShapeDtypeStruct((B, M, X_WIDTH), BF16)] * 2,
        compiler_params=_cparams(("arbitrary",)),
        name="mem_kv",
    )(mem, g_mem.reshape(1, D), w_kv.astype(BF16))


def _in_proj_kernel(x_ref, pos_ref, g_ref, wqkv_ref, wm_ref, wmvt_ref, wgt_ref, bgt_ref,
                    invf_ref, sgn_ref, spread_ref, perm_ref, q_ref, k_ref, v_ref, qc_ref, kc_ref, vc_ref,
                    mu_ref, mvt_ref, mo_ref, gtt_ref, *, tm):
    hb = _rms(x_ref[...], g_ref[...]).astype(BF16)
    qkv = _dot(hb, wqkv_ref[...])
    slots = tm // pos_ref.shape[0]
    ang = pos_ref[...] * invf_ref[...]
    cs_hi, cs_lo = _hi_lo(jnp.where(sgn_ref[0:1, :] > 0.0, jnp.sin(ang), jnp.cos(ang)))
    spread = [_dot(cs_hi, spread_ref[i]) + _dot(cs_lo, spread_ref[i]) for i in range(slots)]
    cos = jnp.concatenate([sp[:, :LANES] for sp in spread], axis=0) + sgn_ref[1:2, :]
    sin = jnp.concatenate([sp[:, LANES:] for sp in spread], axis=0)
    half = ROPE_DIM // 2
    first_half = sgn_ref[2:3, :] > 0.0

    def emit_classes(nat_ref, cls_ref):
        rows = tm // CLASSES
        by_class = _dot(perm_ref[...], nat_ref[...])
        for r in range(CLASSES):
            cls_ref[0, r] = by_class[r * rows:(r + 1) * rows].astype(BF16)

    for which, refs in ((0, (q_ref, qc_ref)), (1, (k_ref, kc_ref))):
        for g in range(ATT_WIDTH // LANES):
            cols = slice(g * LANES, (g + 1) * LANES)
            t = qkv[:, which * ATT_WIDTH + g * LANES: which * ATT_WIDTH + (g + 1) * LANES]
            partner = jnp.where(first_half, pltpu.roll(t, LANES - half, 1), pltpu.roll(t, half, 1))
            refs[0][:, cols] = (t * cos + partner * sin).astype(BF16)
        emit_classes(*refs)
    v_ref[...] = qkv[:, 2 * ATT_WIDTH:].astype(BF16)
    emit_classes(v_ref, vc_ref)
    mm = _dot(hb, wm_ref[...])
    mu_ref[...] = mm[:, :M_WIDTH].astype(BF16)
    mo_ref[...] = mm[:, M_WIDTH:].astype(BF16)
    mvt = _dot_nt(wmvt_ref[...], hb)
    for c in range(tm // M_CHUNK):
        mvt_ref[c] = mvt[:, c * M_CHUNK:(c + 1) * M_CHUNK].astype(BF16)
    gtt_ref[0] = _dot_nt(wgt_ref[...], hb) + bgt_ref[...]


def _in_proj(x2d, pos, g_mix, w_in, b_i, b_f, tm, B):
    N, D = x2d.shape
    S = N // B
    tps = S // tm
    A = ATT_WIDTH
    wq = w_in[:, :A] * (ATT_HEAD_DIM ** -0.5)
    wqkv = jnp.concatenate([wq, w_in[:, A:3 * A]], axis=1).astype(BF16)
    o = 3 * A
    wm = jnp.concatenate([w_in[:, o:o + M_WIDTH], w_in[:, o + 2 * M_WIDTH:o + 3 * M_WIDTH]],
                         axis=1).astype(BF16)
    wmvt = w_in[:, o + M_WIDTH:o + 2 * M_WIDTH].T.astype(BF16)
    wgates = w_in[:, 3 * A + 3 * M_WIDTH:]
    wgt = wgates.T.astype(BF16)
    bgt = jnp.concatenate([b_i, b_f]).astype(F32).reshape(2 * M_HEADS, 1)
    half = ROPE_DIM // 2
    slots = LANES // ROPE_DIM
    lane = np.arange(LANES)
    c = lane % ROPE_DIM
    j = lane % ATT_HEAD_DIM
    inv_freq = ROPE_THETA ** (-jnp.arange(0, ROPE_DIM, 2, dtype=F32) / ROPE_DIM)
    invf = inv_freq[c % half].reshape(1, LANES).astype(F32)
    sgn = np.zeros((8, LANES), np.float32)
    sgn[0] = c >= half
    sgn[1] = j >= ROPE_DIM
    sgn[2] = j < half
    spread = np.zeros((slots, LANES, 2 * LANES), np.float32)
    for i in range(slots):
        for f in range(half):
            spread[i, ROPE_DIM * i + f, lane[(j < ROPE_DIM) & (j % half == f)]] = 1.0
            spread[i, ROPE_DIM * i + half + f, LANES + lane[j == f]] = -1.0
            spread[i, ROPE_DIM * i + half + f, LANES + lane[j == half + f]] = 1.0
    posp = pos.reshape(N // tm, slots, tm // slots).transpose(0, 2, 1)
    posp = jnp.repeat(posp, ROPE_DIM, axis=2).reshape(N // slots, LANES)
    tok = np.arange(tm)
    perm = np.zeros((tm, tm), np.float32)
    perm[(tok % CLASSES) * (tm // CLASSES) + tok // CLASSES, tok] = 1.0
    const = lambda shape: pl.BlockSpec(shape, lambda i: (0,) * len(shape))
    row = lambda w: pl.BlockSpec((tm, w), lambda i: (i, 0))
    cls = pl.BlockSpec((1, CLASSES, tm // CLASSES, A), lambda i: (i // tps, 0, i % tps, 0))
    cls_shape = jax.ShapeDtypeStruct((B, CLASSES, S // CLASSES, A), BF16)
    return pl.pallas_call(
        functools.partial(_in_proj_kernel, tm=tm),
        grid=(N // tm,),
        in_specs=[row(D), pl.BlockSpec((tm // slots, LANES), lambda i: (i, 0)), const((1, D)),
                  const((D, 3 * A)), const((D, 2 * M_WIDTH)),
                  const((M_WIDTH, D)), const((2 * M_HEADS, D)), const((2 * M_HEADS, 1)),
                  const((1, LANES)), const((8, LANES)), const((slots, LANES, 2 * LANES)),
                  const((tm, tm))],
        out_specs=[row(A), row(A), row(A), cls, cls, cls, row(M_WIDTH),
                   pl.BlockSpec((tm // M_CHUNK, M_WIDTH, M_CHUNK), lambda i: (i, 0, 0)), row(M_WIDTH),
                   pl.BlockSpec((1, 2 * M_HEADS, tm), lambda i: (i // tps, 0, i % tps))],
        out_shape=[jax.ShapeDtypeStruct((N, A), BF16)] * 3 + [cls_shape] * 3
        + [jax.ShapeDtypeStruct((N, M_WIDTH), BF16),
           jax.ShapeDtypeStruct((N // M_CHUNK, M_WIDTH, M_CHUNK), BF16),
           jax.ShapeDtypeStruct((N, M_WIDTH), BF16),
           jax.ShapeDtypeStruct((B, 2 * M_HEADS, S), F32)],
        compiler_params=_cparams(("arbitrary",)),
        name="in_proj",
    )(x2d, posp, g_mix.reshape(1, D), wqkv, wm, wmvt, wgt, bgt, invf, jnp.asarray(sgn),
      jnp.asarray(spread, BF16), jnp.asarray(perm, BF16))


def _attn_kernel(q_ref, kc_ref, kp_ref, vc_ref, vp_ref, o_ref, l_ref, kbuf, vbuf, bias_scr, *, qb, nc):
    blk = ATT_BLOCK
    piece = blk // nc
    nsub = qb // blk
    first = pl.program_id(2) == 0
    for c in range(nc):
        kbuf[c * piece:(c + 1) * piece, :] = kp_ref[c]
        vbuf[c * piece:(c + 1) * piece, :] = vp_ref[c]
        for sub in range(nsub):
            dst = slice((sub + 1) * blk + c * piece, (sub + 1) * blk + (c + 1) * piece)
            src = slice(sub * piece, (sub + 1) * piece)
            kbuf[dst, :] = kc_ref[c, src, :]
            vbuf[dst, :] = vc_ref[c, src, :]

    def pos(p):
        p = p & (blk - 1)
        return nc * (p & (piece - 1)) + (p >> (piece.bit_length() - 1))

    qi = lax.broadcasted_iota(jnp.int32, (blk, 2 * blk), 0)
    ki = lax.broadcasted_iota(jnp.int32, (blk, 2 * blk), 1)
    dist = pos(qi) - pos(ki) + jnp.where(ki < blk, blk, 0)
    band = (dist >= 0) & (dist <= blk)
    band_first = band & ((ki >= blk) | jnp.logical_not(first))
    bias_scr[0] = jnp.where(band_first, 0.0, NEG_INF)
    bias_scr[1] = jnp.where(band, 0.0, NEG_INF)
    lane = lax.broadcasted_iota(jnp.int32, (1, LANES), 1)
    lane_full = lax.broadcasted_iota(jnp.int32, (blk, LANES), 1)
    low = lane < ATT_HEAD_DIM
    for sub in range(nsub):
        bias = bias_scr.at[min(sub, 1)]
        prow = slice(sub * piece, (sub + 1) * piece)
        krow = slice(sub * blk, (sub + 2) * blk)
        lse_all = jnp.zeros((blk, LANES), F32)
        for g in range(ATT_WIDTH // LANES):
            cols = slice(g * LANES, (g + 1) * LANES)
            qs = jnp.concatenate([q_ref[c, prow, cols] for c in range(nc)], axis=0)
            ks = kbuf[krow, cols]
            vs = vbuf[krow, cols]
            acc = jnp.zeros((blk, LANES), F32)
            for hh in range(2):
                hm = low if hh == 0 else jnp.logical_not(low)
                s = _dot_nt(jnp.where(hm, qs, jnp.zeros_like(qs)), ks) + bias[...]
                m = jnp.max(s, axis=-1, keepdims=True)
                p = jnp.exp(s - m)
                l = jnp.sum(p, axis=-1, keepdims=True)
                pv = _dot(p.astype(BF16), jnp.where(hm, vs, jnp.zeros_like(vs)))
                acc = acc + pv * (1.0 / l)
                lse_all = jnp.where(lane_full == 2 * g + hh, m + jnp.log(l), lse_all)
            acc = acc.astype(BF16)
            for c in range(nc):
                o_ref[c, prow, cols] = acc[c * piece:(c + 1) * piece]
        for c in range(nc):
            l_ref[c, prow, :] = lse_all[c * piece:(c + 1) * piece]


def _attention_config(q, k, v, d):
    B, C, L, W = q.shape
    nc = C // d
    qb = min(512, L * nc)
    rows = qb // nc
    piece = ATT_BLOCK // nc
    nsub = qb // ATT_BLOCK
    view = lambda t: t.reshape(B, nc, d, L, t.shape[-1])
    cur = lambda w: pl.BlockSpec((None, nc, None, rows, w), lambda b, r, j: (b, 0, r, j, 0))
    prev = pl.BlockSpec((None, nc, None, piece, W),
                        lambda b, r, j: (b, 0, r, jnp.maximum(j * nsub - 1, 0), 0))
    o, lse = pl.pallas_call(
        functools.partial(_attn_kernel, qb=qb, nc=nc),
        grid=(B, d, L // rows),
        in_specs=[cur(W), cur(W), prev, cur(W), prev],
        out_specs=[cur(W), cur(LANES)],
        out_shape=[jax.ShapeDtypeStruct((B, nc, d, L, W), BF16),
                   jax.ShapeDtypeStruct((B, nc, d, L, LANES), F32)],
        scratch_shapes=[pltpu.VMEM((qb + ATT_BLOCK, W), BF16), pltpu.VMEM((qb + ATT_BLOCK, W), BF16),
                        pltpu.VMEM((2, ATT_BLOCK, 2 * ATT_BLOCK), F32)],
        compiler_params=_cparams(("arbitrary", "arbitrary", "arbitrary")),
        name=f"attention_d{d}",
    )(view(q), view(k), view(k), view(v), view(v))
    return o.reshape(B, C, L, W), lse.reshape(B, C, L, LANES)


def _mlstm_kernel(mu_ref, mvt_ref, mo_ref, gtt_ref, cw_ref, cb_ref, wqk_ref, gn_ref,
                  sk_ref, y_ref, c_scr, n_scr, m_scr, ext_scr):
    L = M_CHUNK
    H = M_HEADS
    HD = M_HEAD_DIM

    @pl.when(pl.program_id(1) == 0)
    def _():
        c_scr[...] = jnp.zeros_like(c_scr)
        n_scr[...] = jnp.zeros_like(n_scr)
        m_scr[...] = jnp.zeros_like(m_scr)
        for bb in range(M_BATCH):
            ext_scr[bb, 0:8, :] = jnp.zeros((8, M_WIDTH), F32)

    ri = lax.broadcasted_iota(jnp.int32, (L, L), 0)
    ci = lax.broadcasted_iota(jnp.int32, (L, L), 1)
    causal_t = ri <= ci
    triu = jnp.where(causal_t, 1.0, 0.0).astype(BF16)
    scale = HD ** -0.5

    for bb in range(M_BATCH):
        mu = mu_ref[bb].astype(F32)
        ext_scr[bb, 8:8 + L, :] = mu
        conv = cb_ref[...]
        for jj in range(CONV_WIDTH):
            lo = 8 - (CONV_WIDTH - 1) + jj
            conv = conv + ext_scr[bb, lo:lo + L, :] * cw_ref[jj:jj + 1, :]
        ext_scr[bb, 0:8, :] = mu[L - 8:, :]
        cs = conv * _sigmoid(conv)
        cb16 = cs.astype(BF16)

        gtt = gtt_ref[bb]
        hi_r, lo_r = _hi_lo(_log_sigmoid(gtt))
        b_rows = _dot(hi_r, triu) + _dot(lo_r, triu)
        key_rows = gtt - pltpu.roll(b_rows, H, 0)
        key_cols = jnp.concatenate([key_rows, jnp.zeros((L - 2 * H, L), F32)], axis=0).T

        for hd in range(H):
            cols = slice(hd * HD, (hd + 1) * HD)
            st = bb * H + hd
            qk = _dot(cb16[:, cols], wqk_ref[hd])
            qb = qk[:, :HD].astype(BF16)
            kb = (qk[:, HD:] * scale).astype(BF16)
            vt = mvt_ref[bb, cols, :]
            ig_r = gtt[hd:hd + 1, :]
            b_r = b_rows[H + hd:H + hd + 1, :]
            key_c = key_cols[:, hd:hd + 1]
            b_last = b_r[:, L - 1:L]
            m_prev = m_scr[st:st + 1, 0:1]
            ct_prev = c_scr[st]
            n_prev = n_scr[st]

            dlog = jnp.where(causal_t, b_r + key_c, NEG_INF)
            m_intra = jnp.max(dlog, axis=0, keepdims=True)
            m_inter = b_r + m_prev
            m_t = jnp.maximum(m_inter, m_intra)
            inter_w = jnp.exp(m_inter - m_t)
            st_w = _dot_nt(kb, qb) * jnp.exp(dlog - m_t)
            num = _dot(vt, st_w.astype(BF16)) + inter_w * _dot_nt(ct_prev.astype(BF16), qb)
            den = (jnp.sum(st_w, axis=0, keepdims=True)
                   + inter_w * _dot_nt(n_prev.astype(BF16), qb)[0:1, :])
            ht = num * (1.0 / jnp.maximum(jnp.abs(den), jnp.exp(-m_t)))
            ht = ht * lax.rsqrt(jnp.mean(ht * ht, axis=0, keepdims=True) + EPS)
            hn = ht.T * gn_ref[:, cols]
            y = _sigmoid(mo_ref[bb, :, cols].astype(F32)) * (hn + sk_ref[:, cols] * cs[:, cols])
            y_ref[bb, :, cols] = y.astype(BF16)

            g_r = b_last - b_r + ig_r
            m_loc = jnp.max(g_r, axis=1, keepdims=True)
            wk_r = jnp.exp(g_r - m_loc)
            c_loc = _dot((vt.astype(F32) * wk_r).astype(BF16), kb)
            n_loc = _dot(jnp.broadcast_to(wk_r, (8, L)).astype(BF16), kb)
            m_new = jnp.maximum(b_last + m_prev, m_loc)
            a = jnp.exp(b_last + m_prev - m_new)
            cc = jnp.exp(m_loc - m_new)
            c_scr[st] = a * ct_prev + cc * c_loc
            n_scr[st] = a * n_prev + cc * n_loc
            m_scr[st:st + 1, :] = jnp.broadcast_to(m_new, (1, LANES))


def _mlstm(mu, mvt, mo, gtt, conv_w, conv_b, w_q_m, w_k_m, g_mhn, skip_m):
    B, S, W = mu.shape
    L = M_CHUNK
    nc = S // L
    nb = M_BATCH
    tok = pl.BlockSpec((nb, L, W), lambda b, c: (b, c, 0))
    const = lambda shape: pl.BlockSpec(shape, lambda b, c: (0,) * len(shape))
    wqk = jnp.concatenate([w_q_m, w_k_m], axis=-1).astype(BF16)
    return pl.pallas_call(
        _mlstm_kernel,
        grid=(B // nb, nc),
        in_specs=[tok, pl.BlockSpec((nb, None, W, L), lambda b, c: (b, c, 0, 0)), tok,
                  pl.BlockSpec((nb, 2 * M_HEADS, L), lambda b, c: (b, 0, c)),
                  const((CONV_WIDTH, W)), const((1, W)),
                  const((M_HEADS, M_HEAD_DIM, 2 * M_HEAD_DIM)), const((1, W)), const((1, W))],
        out_specs=tok,
        out_shape=jax.ShapeDtypeStruct((B, S, W), BF16),
        scratch_shapes=[pltpu.VMEM((nb * M_HEADS, M_HEAD_DIM, M_HEAD_DIM), F32),
                        pltpu.VMEM((nb * M_HEADS, 8, M_HEAD_DIM), F32),
                        pltpu.VMEM((nb * M_HEADS, LANES), F32),
                        pltpu.VMEM((nb, 8 + L, W), F32)],
        compiler_params=_cparams(("arbitrary", "arbitrary")),
        name="mlstm",
    )(mu, mvt.reshape(B, nc, W, L), mo, gtt, conv_w.astype(F32), conv_b.reshape(1, W).astype(F32), wqk,
      g_mhn.reshape(1, W).astype(F32), skip_m.reshape(1, W).astype(F32))


def _post_mix_kernel(x_ref, o1_ref, o4_ref, o16_ref, l1_ref, l4_ref, l16_ref, ym_ref, wo_ref, gc_ref,
                     wqx_ref, km_ref, vm_ref, wox_ref, gf_ref, wr_ref, br_ref, before_ref,
                     x2_ref, hx_ref, route_ref, cnt_ref,
                     run_scr, o4_scr, o16_scr, l4_scr, l16_scr, ya_scr, ox_scr, *, tm):
    @pl.when(pl.program_id(0) == 0)
    def _():
        run_scr[...] = jnp.zeros_like(run_scr)

    rows = tm // CLASSES
    for src, dst in ((o4_ref, o4_scr), (o16_ref, o16_scr)):
        for r in range(CLASSES):
            for g in range(ATT_WIDTH // LANES):
                dst[g, pl.ds(r, rows, stride=CLASSES), :] = src[r, :, g * LANES:(g + 1) * LANES].astype(F32)
    for src, dst in ((l4_ref, l4_scr), (l16_ref, l16_scr)):
        for r in range(CLASSES):
            dst[pl.ds(r, rows, stride=CLASSES), :] = src[r]

    lane1 = lax.broadcasted_iota(jnp.int32, (1, LANES), 1)
    low = lane1 < ATT_HEAD_DIM

    l1, l2, l3 = l1_ref[...], l4_scr[...], l16_scr[...]
    mx = jnp.maximum(jnp.maximum(l1, l2), l3)
    e1, e2, e3 = jnp.exp(l1 - mx), jnp.exp(l2 - mx), jnp.exp(l3 - mx)
    inv = 1.0 / (e1 + e2 + e3)
    wts = (e1 * inv, e2 * inv, e3 * inv)
    for g in range(ATT_WIDTH // LANES):
        cols = slice(g * LANES, (g + 1) * LANES)
        o_slabs = (o1_ref[:, cols].astype(F32), o4_scr[g], o16_scr[g])
        ya = jnp.zeros((tm, LANES), F32)
        for c in range(3):
            w = jnp.where(low, wts[c][:, 2 * g:2 * g + 1], wts[c][:, 2 * g + 1:2 * g + 2])
            ya = ya + w * o_slabs[c]
        ya_scr[:, cols] = ya.astype(BF16)
    x1 = x_ref[...] + (_dot(ya_scr[...], wo_ref[:ATT_WIDTH, :]) + _dot(ym_ref[...], wo_ref[ATT_WIDTH:, :]))

    h2 = _rms(x1, gc_ref[...]).astype(BF16)
    qx = _dot(h2, wqx_ref[...]).astype(BF16)
    for g in range(X_WIDTH // LANES):
        cols = slice(g * LANES, (g + 1) * LANES)
        qs = qx[:, cols]
        ks = km_ref[0, :, cols]
        vs = vm_ref[0, :, cols]
        acc = jnp.zeros((tm, LANES), F32)
        for hh in range(2):
            hm = low if hh == 0 else jnp.logical_not(low)
            s = _dot_nt(jnp.where(hm, qs, jnp.zeros_like(qs)), ks)
            m = jnp.max(s, axis=-1, keepdims=True)
            p = jnp.exp(s - m)
            l = jnp.sum(p, axis=-1, keepdims=True)
            acc = acc + _dot(p.astype(BF16), jnp.where(hm, vs, jnp.zeros_like(vs))) * (1.0 / l)
        ox_scr[:, cols] = acc.astype(BF16)
    x2 = x1 + _dot(ox_scr[...], wox_ref[...])
    x2_ref[...] = x2

    h3 = _rms(x2, gf_ref[...])
    h_hi, h_lo = _hi_lo(h3)
    t = _dot(h_hi, wr_ref[...])
    logits = t[:, :LANES] + t[:, LANES:] + _dot(h_lo, wr_ref[:, :LANES]) + br_ref[...]

    lane = lax.broadcasted_iota(jnp.int32, (tm, LANES), 1).astype(F32)
    far = float(LANES)
    gmask = lane < N_GROUPS
    gl = jnp.where(gmask, logits, NEG_INF)
    gmax = jnp.max(gl, axis=-1, keepdims=True)
    gidx = jnp.min(jnp.where(gl == gmax, lane, far), axis=-1, keepdims=True)
    gsum = jnp.sum(jnp.where(gmask, jnp.exp(gl - gmax), 0.0), axis=-1, keepdims=True)
    g_w = 1.0 / gsum
    lo_lane = N_GROUPS + gidx * EXPERTS_PER_GROUP
    emask = (lane >= lo_lane) & (lane < lo_lane + EXPERTS_PER_GROUP)
    el = jnp.where(emask, logits, NEG_INF)
    t1 = jnp.max(el, axis=-1, keepdims=True)
    i1 = jnp.min(jnp.where(el == t1, lane, far), axis=-1, keepdims=True)
    el2 = jnp.where(lane == i1, NEG_INF, el)
    t2 = jnp.max(el2, axis=-1, keepdims=True)
    i2 = jnp.min(jnp.where(el2 == t2, lane, far), axis=-1, keepdims=True)
    ee = jnp.exp(t2 - t1)
    w1 = g_w / (1.0 + ee)
    w2 = w1 * ee

    first = i1 < i2
    ia = jnp.minimum(i1, i2)
    ib = jnp.maximum(i1, i2)
    wa = jnp.where(first, w1, w2)
    wb = jnp.where(first, w2, w1)
    la = ia - lo_lane
    lb = ib - lo_lane
    pair = la * (EXPERTS_PER_GROUP - 1) - la * (la - 1.0) * 0.5 + (lb - la - 1.0)
    bucket = gidx * len(PAIRS) + pair

    hit = lane == bucket
    cnt = jnp.where(hit, 1.0, 0.0)
    prefix = _dot(before_ref[...], cnt.astype(BF16)) + run_scr[0:1, :]
    rank = jnp.sum(jnp.where(hit, prefix, 0.0), axis=-1, keepdims=True)
    total = run_scr[0:1, :] + jnp.sum(cnt, axis=0, keepdims=True)
    run_scr[...] = jnp.broadcast_to(total, run_scr.shape)
    cnt_ref[...] = jnp.broadcast_to(total, cnt_ref.shape)

    fields = (ia - N_GROUPS, ib - N_GROUPS, wa, wb, rank, bucket)
    route = jnp.zeros((tm, LANES), F32)
    for idx, val in enumerate(fields):
        route = jnp.where(lane == float(idx), val, route)
    route_ref[...] = route
    hx_ref[:, :x2_ref.shape[1]] = h3
    hx_ref[:, x2_ref.shape[1]:] = route


def _post_mix(x2d, outs, lses, y_m, w_out, g_cross, w_q_x, k_mem, v_mem, w_o_x, g_ffn,
              w_router_g, b_router_g, w_router_e, b_router_e, tm, B):
    N, D = x2d.shape
    S = N // B
    tps = S // tm
    M = k_mem.shape[1]
    wr = jnp.pad(jnp.concatenate([w_router_g, w_router_e], axis=1).astype(F32),
                 ((0, 0), (0, LANES - N_GROUPS - N_EXPERTS)))
    wr_hi = wr.astype(BF16)
    wr_cat = jnp.concatenate([wr_hi, (wr - wr_hi.astype(F32)).astype(BF16)], axis=1)
    br = jnp.pad(jnp.concatenate([b_router_g, b_router_e]).astype(F32),
                 (0, LANES - N_GROUPS - N_EXPERTS)).reshape(1, LANES)
    const = lambda shape: pl.BlockSpec(shape, lambda i: (0,) * len(shape))
    row = lambda w: pl.BlockSpec((tm, w), lambda i: (i, 0))
    cls = lambda w: pl.BlockSpec((None, CLASSES, tm // CLASSES, w), lambda i: (i // tps, 0, i % tps, 0))
    memspec = pl.BlockSpec((1, M, X_WIDTH), lambda i: (i // tps, 0, 0))
    slabs = ATT_WIDTH // LANES
    return pl.pallas_call(
        functools.partial(_post_mix_kernel, tm=tm),
        grid=(N // tm,),
        in_specs=[row(D), row(ATT_WIDTH), cls(ATT_WIDTH), cls(ATT_WIDTH), row(LANES),
                  cls(LANES), cls(LANES), row(M_WIDTH), const((D, D)), const((1, D)),
                  const((D, X_WIDTH)), memspec, memspec, const((X_WIDTH, D)), const((1, D)),
                  const((D, 2 * LANES)), const((1, LANES)), const((tm, tm))],
        out_specs=[row(D), row(D + ROUTE_W), row(LANES), const((8, LANES))],
        out_shape=[jax.ShapeDtypeStruct((N, D), F32), jax.ShapeDtypeStruct((N, D + ROUTE_W), F32),
                   jax.ShapeDtypeStruct((N, LANES), F32), jax.ShapeDtypeStruct((8, LANES), F32)],
        scratch_shapes=[pltpu.VMEM((8, LANES), F32), pltpu.VMEM((slabs, tm, LANES), F32),
                        pltpu.VMEM((slabs, tm, LANES), F32), pltpu.VMEM((tm, LANES), F32),
                        pltpu.VMEM((tm, LANES), F32), pltpu.VMEM((tm, ATT_WIDTH), BF16),
                        pltpu.VMEM((tm, X_WIDTH), BF16)],
        compiler_params=_cparams(("arbitrary",)),
        name="post_mix",
    )(x2d, outs[0].reshape(N, ATT_WIDTH), outs[1], outs[2], lses[0].reshape(N, LANES), lses[1], lses[2],
      y_m, w_out.astype(BF16), g_cross.reshape(1, D), (w_q_x * (X_HEAD_DIM ** -0.5)).astype(BF16),
      k_mem, v_mem, w_o_x.astype(BF16), g_ffn.reshape(1, D), wr_cat, br,
      jnp.asarray(np.tril(np.ones((tm, tm), np.float32), -1), BF16))


def _tile_row(ref, row):
    return ref.at[row >> 3, pl.ds(row & (SUBLANES - 1), 1), :]


def _dispatch_kernel(pad_lo_ref, pad_hi_ref, nvalid_ref, dest_ref, h_ref, xs_ref, zero_scr, sem, zsem,
                     *, tm, nblk):
    def copy(i, u):
        return pltpu.make_async_copy(h_ref.at[i, pl.ds(u, 1), :],
                                     _tile_row(xs_ref, dest_ref[0, 0, i * SUBLANES + u]), sem)

    def start(i, carry):
        for u in range(SUBLANES):
            copy(i, u).start()
        return carry

    lax.fori_loop(0, tm // SUBLANES, start, 0)

    @pl.when(pl.program_id(0) == 0)
    def _():
        zero_scr[...] = jnp.zeros_like(zero_scr)
        groups = MOE_ROWS // SUBLANES

        def pad_copy(r):
            return pltpu.make_async_copy(zero_scr.at[0, pl.ds(0, 1), :], _tile_row(xs_ref, r), zsem)

        def tail_copy(blk):
            return pltpu.make_async_copy(zero_scr, xs_ref.at[pl.ds(blk * groups, groups), :, :], zsem)

        for q in range(N_BUCKETS):
            lax.fori_loop(pad_lo_ref[q], pad_hi_ref[q], lambda r, c: (pad_copy(r).start(), c)[1], 0)
        lax.fori_loop(nvalid_ref[0], nblk, lambda blk, c: (tail_copy(blk).start(), c)[1], 0)
        for q in range(N_BUCKETS):
            lax.fori_loop(pad_lo_ref[q], pad_hi_ref[q], lambda r, c: (pad_copy(r).wait(), c)[1], 0)
        lax.fori_loop(nvalid_ref[0], nblk, lambda blk, c: (tail_copy(blk).wait(), c)[1], 0)

    pltpu.make_async_copy(h_ref, xs_ref.at[pl.ds(0, tm // SUBLANES), :, :], sem).wait()


def _dispatch(hx, dest, pad_lo, pad_hi, nvalid, n_rows, tm):
    N, W = hx.shape
    grid_spec = pltpu.PrefetchScalarGridSpec(
        num_scalar_prefetch=3,
        grid=(N // tm,),
        in_specs=[pl.BlockSpec((1, 1, tm), lambda i, lo, hi, nv: (i, 0, 0), memory_space=pltpu.SMEM),
                  pl.BlockSpec((tm // SUBLANES, SUBLANES, W), lambda i, lo, hi, nv: (i, 0, 0))],
        out_specs=pl.BlockSpec(memory_space=pl.ANY),
        scratch_shapes=[pltpu.VMEM((MOE_ROWS // SUBLANES, SUBLANES, W), F32), pltpu.SemaphoreType.DMA(()),
                        pltpu.SemaphoreType.DMA(())],
    )
    return pl.pallas_call(
        functools.partial(_dispatch_kernel, tm=tm, nblk=n_rows // MOE_ROWS),
        grid_spec=grid_spec,
        out_shape=jax.ShapeDtypeStruct((n_rows // SUBLANES, SUBLANES, W), F32),
        compiler_params=_cparams(("arbitrary",)),
        name="moe_dispatch",
    )(pad_lo, pad_hi, nvalid, dest.reshape(N // tm, 1, tm),
      hx.reshape(N // SUBLANES, SUBLANES, W)).reshape(n_rows, W)


def _expert_kernel(blk_a_ref, blk_b_ref, nvalid_ref, x_ref, w1a_ref, w3a_ref, w2a_ref, w1b_ref, w3b_ref,
                   w2b_ref, y_ref):
    del blk_a_ref, blk_b_ref
    D = y_ref.shape[1]

    def ffn(xb, w1_ref, w3_ref, w2_ref):
        a = _dot(xb, w1_ref[0])
        b = _dot(xb, w3_ref[0])
        return _dot((a * _sigmoid(a) * b).astype(BF16), w2_ref[0])

    @pl.when(pl.program_id(0) < nvalid_ref[0])
    def _():
        xb = x_ref[:, :D].astype(BF16)
        wa = x_ref[:, D + 2:D + 3]
        wb = x_ref[:, D + 3:D + 4]
        y_ref[...] = (wa * ffn(xb, w1a_ref, w3a_ref, w2a_ref)
                      + wb * ffn(xb, w1b_ref, w3b_ref, w2b_ref))

    @pl.when(pl.program_id(0) >= nvalid_ref[0])
    def _():
        y_ref[...] = jnp.zeros_like(y_ref)


def _experts(xs, blk_a, blk_b, nvalid, w1, w3, w2):
    P, W = xs.shape
    D = W - ROUTE_W
    nblk = P // MOE_ROWS
    F = w1.shape[-1]
    up_a = pl.BlockSpec((1, D, F), lambda i, ba, bb, nv: (ba[i], 0, 0))
    up_b = pl.BlockSpec((1, D, F), lambda i, ba, bb, nv: (bb[i], 0, 0))
    grid_spec = pltpu.PrefetchScalarGridSpec(
        num_scalar_prefetch=3,
        grid=(nblk,),
        in_specs=[pl.BlockSpec((MOE_ROWS, W),
                               lambda i, ba, bb, nv: (jnp.maximum(jnp.minimum(i, nv[0] - 1), 0), 0)),
                  up_a, up_a, pl.BlockSpec((1, F, D), lambda i, ba, bb, nv: (ba[i], 0, 0)),
                  up_b, up_b, pl.BlockSpec((1, F, D), lambda i, ba, bb, nv: (bb[i], 0, 0))],
        out_specs=pl.BlockSpec((MOE_ROWS, D), lambda i, ba, bb, nv: (i, 0)),
    )
    w1b, w3b, w2b = w1.astype(BF16), w3.astype(BF16), w2.astype(BF16)
    return pl.pallas_call(
        _expert_kernel,
        grid_spec=grid_spec,
        out_shape=jax.ShapeDtypeStruct((P, D), F32),
        compiler_params=_cparams(("arbitrary",)),
        name="moe_experts",
    )(blk_a, blk_b, nvalid, xs, w1b, w3b, w2b, w1b, w3b, w2b)


def _combine_kernel(dest_ref, dest_next_ref, x_ref, g_ref, ys_ref, o_ref, buf, sem, *, tm, nsteps):
    step = pl.program_id(0)
    slot = step % 2
    groups = tm // SUBLANES

    def request(d_ref, s):
        def body(i, carry):
            for u in range(SUBLANES):
                pltpu.make_async_copy(_tile_row(ys_ref, d_ref[0, 0, i * SUBLANES + u]),
                                      buf.at[s, i, pl.ds(u, 1), :], sem.at[s]).start()
            return carry
        lax.fori_loop(0, groups, body, 0)

    @pl.when(step == 0)
    def _():
        request(dest_ref, 0)

    @pl.when(step + 1 < nsteps)
    def _():
        request(dest_next_ref, 1 - slot)

    pltpu.make_async_copy(ys_ref.at[pl.ds(0, groups), :, :], buf.at[slot], sem.at[slot]).wait()
    o_ref[...] = _rms(x_ref[...] + buf[slot].reshape(o_ref.shape), g_ref[...])


def _combine(x2, dest, ys, g_final, tm):
    N, D = x2.shape
    nsteps = N // tm
    dest3 = dest.reshape(nsteps, 1, tm)
    return pl.pallas_call(
        functools.partial(_combine_kernel, tm=tm, nsteps=nsteps),
        grid=(nsteps,),
        in_specs=[pl.BlockSpec((1, 1, tm), lambda i: (i, 0, 0), memory_space=pltpu.SMEM),
                  pl.BlockSpec((1, 1, tm), lambda i: (jnp.minimum(i + 1, nsteps - 1), 0, 0),
                               memory_space=pltpu.SMEM),
                  pl.BlockSpec((tm, D), lambda i: (i, 0)),
                  pl.BlockSpec((1, D), lambda i: (0, 0)),
                  pl.BlockSpec(memory_space=pl.ANY)],
        out_specs=pl.BlockSpec((tm, D), lambda i: (i, 0)),
        out_shape=jax.ShapeDtypeStruct((N, D), F32),
        scratch_shapes=[pltpu.VMEM((2, tm // SUBLANES, SUBLANES, D), F32),
                        pltpu.SemaphoreType.DMA((2,))],
        compiler_params=_cparams(("arbitrary",)),
        name="moe_combine",
    )(dest3, dest3, x2, g_final.reshape(1, D), ys.reshape(ys.shape[0] // SUBLANES, SUBLANES, D))


def kernel(x, mem, positions, g_mix, w_in, conv_w, conv_b, w_q_m, w_k_m, b_i, b_f, g_mhn, skip_m, w_out, g_cross, g_mem, w_q_x, w_kv_x, w_o_x, g_ffn, w_router_g, b_router_g, w_router_e, b_router_e, w1, w3, w2, g_final):
    B, S, D = x.shape
    N = B * S
    depth = g_mix.shape[0]
    tm_in = 512
    tm_post = 512
    tm_moe = 1024
    assert all(window // d == ATT_BLOCK and CLASSES % d == 0 for window, d in DILATED_CONFIGS)
    assert B % M_BATCH == 0 and S % tm_in == 0 and S % tm_post == 0 and N % tm_moe == 0
    assert depth == 1
    for l in range(depth):
        x2d = x.reshape(N, D)
        pos = positions.astype(F32).reshape(N, 1)
        k_mem, v_mem = _mem_kv(mem, g_mem[l], w_kv_x[l])
        q, k, v, qc, kc, vc, mu, mvt, mo, gtt = _in_proj(
            x2d, pos, g_mix[l], w_in[l], b_i[l], b_f[l], tm_in, B)
        q, k, v = (t.reshape(B, 1, S, ATT_WIDTH) for t in (q, k, v))
        outs, lses = zip(*(_attention_config(*(qkv + (d,)))
                           for qkv, (_, d) in zip(((q, k, v), (qc, kc, vc), (qc, kc, vc)), DILATED_CONFIGS)))
        y_m = _mlstm(mu.reshape(B, S, M_WIDTH), mvt, mo.reshape(B, S, M_WIDTH),
                     gtt, conv_w[l], conv_b[l], w_q_m[l], w_k_m[l],
                     g_mhn[l], skip_m[l]).reshape(N, M_WIDTH)
        x2, hx, route, cnt = _post_mix(x2d, outs, lses, y_m, w_out[l], g_cross[l], w_q_x[l], k_mem,
                                       v_mem, w_o_x[l], g_ffn[l], w_router_g[l], b_router_g[l],
                                       w_router_e[l], b_router_e[l], tm_post, B)

        rank = route[:, 4].astype(jnp.int32)
        bucket = route[:, 5].astype(jnp.int32)
        counts = cnt[0, :N_BUCKETS].astype(jnp.int32)
        padded = ((counts + MOE_ROWS - 1) // MOE_ROWS) * MOE_ROWS
        pends = jnp.cumsum(padded)
        pstarts = pends - padded
        onehot = bucket[:, None] == jnp.arange(N_BUCKETS, dtype=jnp.int32)
        dest = jnp.sum(jnp.where(onehot, pstarts, 0), axis=-1) + rank
        n_rows = N + N_BUCKETS * MOE_ROWS
        nblk = n_rows // MOE_ROWS
        blk_start = jnp.arange(nblk, dtype=jnp.int32) * MOE_ROWS
        blk_bucket = jnp.minimum(jnp.sum(pends[None, :] <= blk_start[:, None], axis=1), N_BUCKETS - 1)
        base = (np.arange(N_BUCKETS) // len(PAIRS)) * EXPERTS_PER_GROUP
        expert_a = jnp.asarray(base + np.array([p[0] for p in PAIRS] * N_GROUPS), jnp.int32)
        expert_b = jnp.asarray(base + np.array([p[1] for p in PAIRS] * N_GROUPS), jnp.int32)
        blk_a = jnp.take(expert_a, blk_bucket).astype(jnp.int32)
        blk_b = jnp.take(expert_b, blk_bucket).astype(jnp.int32)
        nvalid = (pends[-1] // MOE_ROWS).reshape(1).astype(jnp.int32)

        xs = _dispatch(hx, dest, (pstarts + counts).astype(jnp.int32), pends.astype(jnp.int32),
                       nvalid, n_rows, tm_moe)
        ys = _experts(xs, blk_a, blk_b, nvalid, w1[l], w3[l], w2[l])
        x = _combine(x2, dest, ys, g_final, tm_moe).reshape(B, S, D)
    return x
```

```python
import functools

import jax
import jax.numpy as jnp
import numpy as np
from jax import lax
from jax.experimental import pallas as pl
from jax.experimental.pallas import tpu as pltpu

F32 = jnp.float32
BF16 = jnp.bfloat16

EPS = 1e-6
LANES = 128
SUBLANES = 8
ATT_HEAD_DIM = 64
ATT_WIDTH = 512
DILATED_CONFIGS = ((128, 1), (512, 4), (2048, 16))
CLASSES = 16
ATT_BLOCK = 128
ROPE_THETA = 500000.0
ROPE_DIM = ATT_HEAD_DIM // 4
M_WIDTH = 512
M_HEADS = 4
M_HEAD_DIM = 128
CONV_WIDTH = 4
M_CHUNK = 128
M_BATCH = 4
X_HEADS = 4
X_HEAD_DIM = 64
X_WIDTH = X_HEADS * X_HEAD_DIM
N_GROUPS = 4
EXPERTS_PER_GROUP = 4
N_EXPERTS = 16
TOP_K = 2
EXPERT_FF = 512
PAIRS = tuple((a, b) for a in range(EXPERTS_PER_GROUP) for b in range(a + 1, EXPERTS_PER_GROUP))
N_BUCKETS = N_GROUPS * len(PAIRS)
MOE_ROWS = 512
ROUTE_W = LANES
POST_SPLIT = 1
VMEM_LIMIT = 56 * 1024 * 1024

NEG_INF = float("-inf")


def _cparams(sem):
    return pltpu.CompilerParams(dimension_semantics=sem, vmem_limit_bytes=VMEM_LIMIT)


def _rms(x, g):
    return x * lax.rsqrt(jnp.mean(x * x, axis=-1, keepdims=True) + EPS) * g


def _dot(a, b):
    return jnp.dot(a, b, preferred_element_type=F32)


def _dot_nt(a, b):
    return lax.dot_general(a, b, (((1,), (1,)), ((), ())), preferred_element_type=F32)


def _dot_tn(a, b):
    return lax.dot_general(a, b, (((0,), (0,)), ((), ())), preferred_element_type=F32)


def _hi_lo(a):
    hi = a.astype(BF16)
    return hi, (a - hi.astype(F32)).astype(BF16)


def _log_sigmoid(x):
    return jnp.minimum(x, 0.0) - jnp.log(1.0 + jnp.exp(-jnp.abs(x)))


def _sigmoid(x):
    return 0.5 * jnp.tanh(0.5 * x) + 0.5


def _mem_kv_kernel(mem_ref, g_ref, w_ref, k_ref, v_ref):
    h = _rms(mem_ref[0], g_ref[...]).astype(BF16)
    kv = _dot(h, w_ref[...])
    k_ref[0] = kv[:, :X_WIDTH].astype(BF16)
    v_ref[0] = kv[:, X_WIDTH:].astype(BF16)


def _mem_kv(mem, g_mem, w_kv):
    B, M, D = mem.shape
    return pl.pallas_call(
        _mem_kv_kernel,
        grid=(B,),
        in_specs=[pl.BlockSpec((1, M, D), lambda b: (b, 0, 0)),
                  pl.BlockSpec((1, D), lambda b: (0, 0)),
                  pl.BlockSpec((D, 2 * X_WIDTH), lambda b: (0, 0))],
        out_specs=[pl.BlockSpec((1, M, X_WIDTH), lambda b: (b, 0, 0)),
                   pl.BlockSpec((1, M, X_WIDTH), lambda b: (b, 0, 0))],
        out_shape=[jax.ShapeDtypeStruct((B, M, X_WIDTH), BF16)] * 2,
        compiler_params=_cparams(("arbitrary",)),
        name="mem_kv",
    )(mem, g_mem.reshape(1, D), w_kv.astype(BF16))


def _in_proj_kernel(x_ref, pos_ref, g_ref, wqkv_ref, wm_ref, wmvt_ref, wgt_ref, bgt_ref,
                    invf_ref, sgn_ref, spread_ref, perm_ref,
                    q_ref, k_ref, v_ref, qc_ref, kc_ref, vc_ref,
                    mu_ref, mvt_ref, mo_ref, gtt_ref, *, tm):
    hb = _rms(x_ref[...], g_ref[...]).astype(BF16)
    qkv = _dot(hb, wqkv_ref[...])
    slots = tm // pos_ref.shape[0]
    ang = pos_ref[...] * invf_ref[...]
    cs_hi, cs_lo = _hi_lo(jnp.where(sgn_ref[0:1, :] > 0.0, jnp.sin(ang), jnp.cos(ang)))
    spread = [_dot(cs_hi, spread_ref[i]) + _dot(cs_lo, spread_ref[i]) for i in range(slots)]
    cos = jnp.concatenate([sp[:, :LANES] for sp in spread], axis=0) + sgn_ref[1:2, :]
    sin = jnp.concatenate([sp[:, LANES:] for sp in spread], axis=0)
    half = ROPE_DIM // 2
    first_half = sgn_ref[2:3, :] > 0.0

    def emit_classes(nat_ref, cls_ref):
        rows = tm // CLASSES
        by_class = _dot(perm_ref[...], nat_ref[...])
        for r in range(CLASSES):
            cls_ref[0, r] = by_class[r * rows:(r + 1) * rows].astype(BF16)

    for which, refs in ((0, (q_ref, qc_ref)), (1, (k_ref, kc_ref))):
        for g in range(ATT_WIDTH // LANES):
            cols = slice(g * LANES, (g + 1) * LANES)
            t = qkv[:, which * ATT_WIDTH + g * LANES: which * ATT_WIDTH + (g + 1) * LANES]
            partner = jnp.where(first_half, pltpu.roll(t, LANES - half, 1), pltpu.roll(t, half, 1))
            refs[0][:, cols] = (t * cos + partner * sin).astype(BF16)
        emit_classes(*refs)
    v_ref[...] = qkv[:, 2 * ATT_WIDTH:].astype(BF16)
    emit_classes(v_ref, vc_ref)
    mm = _dot(hb, wm_ref[...])
    mu_ref[...] = mm[:, :M_WIDTH].astype(BF16)
    mo_ref[...] = mm[:, M_WIDTH:].astype(BF16)
    mvt = _dot_nt(wmvt_ref[...], hb)
    for c in range(tm // M_CHUNK):
        mvt_ref[c] = mvt[:, c * M_CHUNK:(c + 1) * M_CHUNK].astype(BF16)
    gtt_ref[0] = _dot_nt(wgt_ref[...], hb) + bgt_ref[...]


def _in_proj(x2d, pos, g_mix, w_in, b_i, b_f, tm, B):
    N, D = x2d.shape
    S = N // B
    tps = S // tm
    A = ATT_WIDTH
    wq = w_in[:, :A] * (ATT_HEAD_DIM ** -0.5)
    wqkv = jnp.concatenate([wq, w_in[:, A:3 * A]], axis=1).astype(BF16)
    o = 3 * A
    wm = jnp.concatenate([w_in[:, o:o + M_WIDTH], w_in[:, o + 2 * M_WIDTH:o + 3 * M_WIDTH]],
                         axis=1).astype(BF16)
    wmvt = w_in[:, o + M_WIDTH:o + 2 * M_WIDTH].T.astype(BF16)
    wgates = w_in[:, 3 * A + 3 * M_WIDTH:]
    wgt = wgates.T.astype(BF16)
    bgt = jnp.concatenate([b_i, b_f]).astype(F32).reshape(2 * M_HEADS, 1)
    half = ROPE_DIM // 2
    slots = LANES // ROPE_DIM
    lane = np.arange(LANES)
    c = lane % ROPE_DIM
    j = lane % ATT_HEAD_DIM
    inv_freq = ROPE_THETA ** (-jnp.arange(0, ROPE_DIM, 2, dtype=F32) / ROPE_DIM)
    invf = inv_freq[c % half].reshape(1, LANES).astype(F32)
    sgn = np.zeros((8, LANES), np.float32)
    sgn[0] = c >= half
    sgn[1] = j >= ROPE_DIM
    sgn[2] = j < half
    spread = np.zeros((slots, LANES, 2 * LANES), np.float32)
    for i in range(slots):
        for f in range(half):
            spread[i, ROPE_DIM * i + f, lane[(j < ROPE_DIM) & (j % half == f)]] = 1.0
            spread[i, ROPE_DIM * i + half + f, LANES + lane[j == f]] = -1.0
            spread[i, ROPE_DIM * i + half + f, LANES + lane[j == half + f]] = 1.0
    posp = pos.reshape(N // tm, slots, tm // slots).transpose(0, 2, 1)
    posp = jnp.repeat(posp, ROPE_DIM, axis=2).reshape(N // slots, LANES)
    tok = np.arange(tm)
    perm = np.zeros((tm, tm), np.float32)
    perm[(tok % CLASSES) * (tm // CLASSES) + tok // CLASSES, tok] = 1.0
    const = lambda shape: pl.BlockSpec(shape, lambda i: (0,) * len(shape))
    row = lambda w: pl.BlockSpec((tm, w), lambda i: (i, 0))
    cls = pl.BlockSpec((1, CLASSES, tm // CLASSES, A), lambda i: (i // tps, 0, i % tps, 0))
    cls_shape = jax.ShapeDtypeStruct((B, CLASSES, S // CLASSES, A), BF16)
    return pl.pallas_call(
        functools.partial(_in_proj_kernel, tm=tm),
        grid=(N // tm,),
        in_specs=[row(D), pl.BlockSpec((tm // slots, LANES), lambda i: (i, 0)), const((1, D)),
                  const((D, 3 * A)), const((D, 2 * M_WIDTH)),
                  const((M_WIDTH, D)), const((2 * M_HEADS, D)), const((2 * M_HEADS, 1)),
                  const((1, LANES)), const((8, LANES)), const((slots, LANES, 2 * LANES)),
                  const((tm, tm))],
        out_specs=[row(A), row(A), row(A), cls, cls, cls, row(M_WIDTH),
                   pl.BlockSpec((tm // M_CHUNK, M_WIDTH, M_CHUNK), lambda i: (i, 0, 0)), row(M_WIDTH),
                   pl.BlockSpec((1, 2 * M_HEADS, tm), lambda i: (i // tps, 0, i % tps))],
        out_shape=[jax.ShapeDtypeStruct((N, A), BF16)] * 3 + [cls_shape] * 3
        + [jax.ShapeDtypeStruct((N, M_WIDTH), BF16),
           jax.ShapeDtypeStruct((N // M_CHUNK, M_WIDTH, M_CHUNK), BF16),
           jax.ShapeDtypeStruct((N, M_WIDTH), BF16),
           jax.ShapeDtypeStruct((B, 2 * M_HEADS, S), F32)],
        compiler_params=_cparams(("arbitrary",)),
        name="in_proj",
    )(x2d, posp, g_mix.reshape(1, D), wqkv, wm, wmvt, wgt, bgt, invf, jnp.asarray(sgn),
      jnp.asarray(spread, BF16), jnp.asarray(perm, BF16))


def _attn_kernel(q_ref, kc_ref, kp_ref, vc_ref, vp_ref, o_ref, l_ref, kbuf, vbuf, bias_scr, *, qb, nc):
    blk = ATT_BLOCK
    piece = blk // nc
    nsub = qb // blk
    first = pl.program_id(2) == 0
    pair_low = (lax.broadcasted_iota(jnp.int32, (1, ATT_WIDTH), 1) & (LANES - 1)) < ATT_HEAD_DIM

    def put_v(dst, v):
        zero = jnp.zeros_like(v)
        vbuf[0, dst, :] = jnp.where(pair_low, v, zero)
        vbuf[1, dst, :] = jnp.where(pair_low, zero, v)

    for c in range(nc):
        kbuf[c * piece:(c + 1) * piece, :] = kp_ref[c]
        put_v(slice(c * piece, (c + 1) * piece), vp_ref[c])
        for sub in range(nsub):
            dst = slice((sub + 1) * blk + c * piece, (sub + 1) * blk + (c + 1) * piece)
            src = slice(sub * piece, (sub + 1) * piece)
            kbuf[dst, :] = kc_ref[c, src, :]
            put_v(dst, vc_ref[c, src, :])

    def pos(p):
        p = p & (blk - 1)
        return nc * (p & (piece - 1)) + (p >> (piece.bit_length() - 1))

    qi = lax.broadcasted_iota(jnp.int32, (blk, 2 * blk), 0)
    ki = lax.broadcasted_iota(jnp.int32, (blk, 2 * blk), 1)
    dist = pos(qi) - pos(ki) + jnp.where(ki < blk, blk, 0)
    band = (dist >= 0) & (dist <= blk)
    band_first = band & ((ki >= blk) | jnp.logical_not(first))
    bias_scr[0] = jnp.where(band_first, 0.0, NEG_INF)
    bias_scr[1] = jnp.where(band, 0.0, NEG_INF)
    lane = lax.broadcasted_iota(jnp.int32, (1, LANES), 1)
    lane_full = lax.broadcasted_iota(jnp.int32, (blk, LANES), 1)
    low = lane < ATT_HEAD_DIM
    for sub in range(nsub):
        bias = bias_scr.at[min(sub, 1)]
        prow = slice(sub * piece, (sub + 1) * piece)
        krow = slice(sub * blk, (sub + 2) * blk)
        hds = range(2 * (ATT_WIDTH // LANES))
        col = [slice((h // 2) * LANES, (h // 2 + 1) * LANES) for h in hds]
        qs = [jnp.concatenate([q_ref[c, prow, col[h]] for c in range(nc)], axis=0) for h in hds]
        qs = [jnp.where(low if h % 2 == 0 else jnp.logical_not(low), qs[h], jnp.zeros_like(qs[h]))
              for h in hds]
        s = [_dot_nt(qs[h], kbuf[krow, col[h]]) + bias[...] for h in hds]
        m = [jnp.max(s[h], axis=-1, keepdims=True) for h in hds]
        p = [jnp.exp(s[h] - m[h]) for h in hds]
        l = [jnp.sum(p[h], axis=-1, keepdims=True) for h in hds]
        pv = [_dot(p[h].astype(BF16), vbuf[h % 2, krow, col[h]]) * (1.0 / l[h]) for h in hds]
        m_all = jnp.zeros((blk, LANES), F32)
        l_all = jnp.ones((blk, LANES), F32)
        for h in hds:
            m_all = jnp.where(lane_full == h, m[h], m_all)
            l_all = jnp.where(lane_full == h, l[h], l_all)
        for g in range(ATT_WIDTH // LANES):
            acc = (pv[2 * g] + pv[2 * g + 1]).astype(BF16)
            for c in range(nc):
                o_ref[c, prow, col[2 * g]] = acc[c * piece:(c + 1) * piece]
        lse_all = m_all + jnp.log(l_all)
        for c in range(nc):
            l_ref[c, prow, :] = lse_all[c * piece:(c + 1) * piece]


def _attention_config(q, k, v, d):
    B, C, L, W = q.shape
    nc = C // d
    qb = min(512, L * nc)
    rows = qb // nc
    piece = ATT_BLOCK // nc
    nsub = qb // ATT_BLOCK
    view = lambda t: t.reshape(B, nc, d, L, t.shape[-1])
    cur = lambda w: pl.BlockSpec((None, nc, None, rows, w), lambda b, r, j: (b, 0, r, j, 0))
    prev = pl.BlockSpec((None, nc, None, piece, W),
                        lambda b, r, j: (b, 0, r, jnp.maximum(j * nsub - 1, 0), 0))
    o, lse = pl.pallas_call(
        functools.partial(_attn_kernel, qb=qb, nc=nc),
        grid=(B, d, L // rows),
        in_specs=[cur(W), cur(W), prev, cur(W), prev],
        out_specs=[cur(W), cur(LANES)],
        out_shape=[jax.ShapeDtypeStruct((B, nc, d, L, W), BF16),
                   jax.ShapeDtypeStruct((B, nc, d, L, LANES), F32)],
        scratch_shapes=[pltpu.VMEM((qb + ATT_BLOCK, W), BF16), pltpu.VMEM((2, qb + ATT_BLOCK, W), BF16),
                        pltpu.VMEM((2, ATT_BLOCK, 2 * ATT_BLOCK), F32)],
        compiler_params=_cparams(("arbitrary", "arbitrary", "arbitrary")),
        name=f"attention_d{d}",
    )(view(q), view(k), view(k), view(v), view(v))
    return o.reshape(B, C, L, W), lse.reshape(B, C, L, LANES)


def _mlstm_kernel(mu_ref, mvt_ref, mo_ref, gtt_ref, cw_ref, cb_ref, wqk_ref, gn_ref,
                  sk_ref, y_ref, c_scr, n_scr, m_scr, ext_scr):
    L = M_CHUNK
    H = M_HEADS
    HD = M_HEAD_DIM

    @pl.when(pl.program_id(1) == 0)
    def _():
        c_scr[...] = jnp.zeros_like(c_scr)
        n_scr[...] = jnp.zeros_like(n_scr)
        m_scr[...] = jnp.zeros_like(m_scr)
        for bb in range(M_BATCH):
            ext_scr[bb, 0:8, :] = jnp.zeros((8, M_WIDTH), F32)

    ri = lax.broadcasted_iota(jnp.int32, (L, L), 0)
    ci = lax.broadcasted_iota(jnp.int32, (L, L), 1)
    causal_t = ri <= ci
    triu = jnp.where(causal_t, 1.0, 0.0).astype(BF16)
    scale = HD ** -0.5

    bbs = range(M_BATCH)
    cs, gtt, b_rows, key_cols = [], [], [], []
    for bb in bbs:
        mu = mu_ref[bb].astype(F32)
        ext_scr[bb, 8:8 + L, :] = mu
        conv = cb_ref[...]
        for jj in range(CONV_WIDTH):
            lo = 8 - (CONV_WIDTH - 1) + jj
            conv = conv + ext_scr[bb, lo:lo + L, :] * cw_ref[jj:jj + 1, :]
        ext_scr[bb, 0:8, :] = mu[L - 8:, :]
        cs.append(conv * _sigmoid(conv))
    cb16 = [cs[bb].astype(BF16) for bb in bbs]
    for bb in bbs:
        gtt.append(gtt_ref[bb])
        hi_r, lo_r = _hi_lo(_log_sigmoid(gtt[bb]))
        b_rows.append(_dot(hi_r, triu) + _dot(lo_r, triu))
        key_rows = gtt[bb] - pltpu.roll(b_rows[bb], H, 0)
        key_cols.append(jnp.concatenate([key_rows, jnp.zeros((L - 2 * H, L), F32)], axis=0).T)

    prs = [(bb, hd) for bb in bbs for hd in range(H)]
    ids = range(len(prs))
    col = [slice(hd * HD, (hd + 1) * HD) for _, hd in prs]
    qk = [_dot(cb16[bb][:, col[i]], wqk_ref[hd]) for i, (bb, hd) in enumerate(prs)]
    qb = [qk[i][:, :HD].astype(BF16) for i in ids]
    kb = [(qk[i][:, HD:] * scale).astype(BF16) for i in ids]
    vt = [mvt_ref[bb, col[i], :] for i, (bb, _) in enumerate(prs)]
    b_r = [b_rows[bb][H + hd:H + hd + 1, :] for bb, hd in prs]
    b_last = [b_r[i][:, L - 1:L] for i in ids]
    m_prev = [m_scr[i:i + 1, 0:1] for i in ids]
    ct_prev = [c_scr[i] for i in ids]
    n_prev = [n_scr[i] for i in ids]

    dlog = [jnp.where(causal_t, b_r[i] + key_cols[bb][:, hd:hd + 1], NEG_INF)
            for i, (bb, hd) in enumerate(prs)]
    m_inter = [b_r[i] + m_prev[i] for i in ids]
    m_t = [jnp.maximum(m_inter[i], jnp.max(dlog[i], axis=0, keepdims=True)) for i in ids]
    inter_w = [jnp.exp(m_inter[i] - m_t[i]) for i in ids]
    st_w = [_dot_nt(kb[i], qb[i]) * jnp.exp(dlog[i] - m_t[i]) for i in ids]
    num = [_dot(vt[i], st_w[i].astype(BF16)) + inter_w[i] * _dot_nt(ct_prev[i].astype(BF16), qb[i])
           for i in ids]
    den = [jnp.sum(st_w[i], axis=0, keepdims=True)
           + inter_w[i] * _dot_nt(n_prev[i].astype(BF16), qb[i])[0:1, :] for i in ids]
    ht = [num[i] * (1.0 / jnp.maximum(jnp.abs(den[i]), jnp.exp(-m_t[i]))) for i in ids]
    ht = [ht[i] * lax.rsqrt(jnp.mean(ht[i] * ht[i], axis=0, keepdims=True) + EPS) for i in ids]
    for i, (bb, _) in enumerate(prs):
        hn = ht[i].T * gn_ref[:, col[i]]
        y = _sigmoid(mo_ref[bb, :, col[i]].astype(F32)) * (hn + sk_ref[:, col[i]] * cs[bb][:, col[i]])
        y_ref[bb, :, col[i]] = y.astype(BF16)

    g_r = [b_last[i] - b_r[i] + gtt[bb][hd:hd + 1, :] for i, (bb, hd) in enumerate(prs)]
    m_loc = [jnp.max(g_r[i], axis=1, keepdims=True) for i in ids]
    wk_r = [jnp.exp(g_r[i] - m_loc[i]) for i in ids]
    c_loc = [_dot((vt[i].astype(F32) * wk_r[i]).astype(BF16), kb[i]) for i in ids]
    n_loc = [_dot(jnp.broadcast_to(wk_r[i], (8, L)).astype(BF16), kb[i]) for i in ids]
    for i in ids:
        m_new = jnp.maximum(b_last[i] + m_prev[i], m_loc[i])
        a = jnp.exp(b_last[i] + m_prev[i] - m_new)
        cc = jnp.exp(m_loc[i] - m_new)
        c_scr[i] = a * ct_prev[i] + cc * c_loc[i]
        n_scr[i] = a * n_prev[i] + cc * n_loc[i]
        m_scr[i:i + 1, :] = jnp.broadcast_to(m_new, (1, LANES))


def _mlstm(mu, mvt, mo, gtt, conv_w, conv_b, w_q_m, w_k_m, g_mhn, skip_m):
    B, S, W = mu.shape
    L = M_CHUNK
    nc = S // L
    nb = M_BATCH
    tok = pl.BlockSpec((nb, L, W), lambda b, c: (b, c, 0))
    const = lambda shape: pl.BlockSpec(shape, lambda b, c: (0,) * len(shape))
    wqk = jnp.concatenate([w_q_m, w_k_m], axis=-1).astype(BF16)
    return pl.pallas_call(
        _mlstm_kernel,
        grid=(B // nb, nc),
        in_specs=[tok, pl.BlockSpec((nb, None, W, L), lambda b, c: (b, c, 0, 0)), tok,
                  pl.BlockSpec((nb, 2 * M_HEADS, L), lambda b, c: (b, 0, c)),
                  const((CONV_WIDTH, W)), const((1, W)),
                  const((M_HEADS, M_HEAD_DIM, 2 * M_HEAD_DIM)), const((1, W)), const((1, W))],
        out_specs=tok,
        out_shape=jax.ShapeDtypeStruct((B, S, W), BF16),
        scratch_shapes=[pltpu.VMEM((nb * M_HEADS, M_HEAD_DIM, M_HEAD_DIM), F32),
                        pltpu.VMEM((nb * M_HEADS, 8, M_HEAD_DIM), F32),
                        pltpu.VMEM((nb * M_HEADS, LANES), F32),
                        pltpu.VMEM((nb, 8 + L, W), F32)],
        compiler_params=_cparams(("arbitrary", "arbitrary")),
        name="mlstm",
    )(mu, mvt.reshape(B, nc, W, L), mo, gtt, conv_w.astype(F32), conv_b.reshape(1, W).astype(F32), wqk,
      g_mhn.reshape(1, W).astype(F32), skip_m.reshape(1, W).astype(F32))


def _post_mix_kernel(x_ref, o1_ref, o4_ref, o16_ref, l1_ref, l4_ref, l16_ref, ym_ref, wo_ref, gc_ref,
                     wqx_ref, km_ref, vm_ref, wox_ref, gf_ref, wr_ref, br_ref, before_ref, unperm_ref,
                     x2_ref, hx_ref, route_ref, cnt_ref,
                     run_scr, l4_scr, l16_scr, ya_scr, ox_scr, *, tm):
    @pl.when(pl.program_id(0) == 0)
    def _():
        run_scr[...] = jnp.zeros_like(run_scr)

    rows = tm // CLASSES
    o4_tok = _dot(unperm_ref[...], o4_ref[...].reshape(tm, ATT_WIDTH))
    o16_tok = _dot(unperm_ref[...], o16_ref[...].reshape(tm, ATT_WIDTH))
    for src, dst in ((l4_ref, l4_scr), (l16_ref, l16_scr)):
        for r in range(CLASSES):
            dst[pl.ds(r, rows, stride=CLASSES), :] = src[r]

    lane1 = lax.broadcasted_iota(jnp.int32, (1, LANES), 1)
    low = lane1 < ATT_HEAD_DIM

    sub_rows = tm // POST_SPLIT
    logit_parts = []
    for part in range(POST_SPLIT):
        rs = slice(part * sub_rows, (part + 1) * sub_rows)

        l1, l2, l3 = l1_ref[rs, :], l4_scr[rs, :], l16_scr[rs, :]
        mx = jnp.maximum(jnp.maximum(l1, l2), l3)
        e1, e2, e3 = jnp.exp(l1 - mx), jnp.exp(l2 - mx), jnp.exp(l3 - mx)
        inv = 1.0 / (e1 + e2 + e3)
        wts = (e1 * inv, e2 * inv, e3 * inv)
        slabs = range(ATT_WIDTH // LANES)
        acol = [slice(g * LANES, (g + 1) * LANES) for g in slabs]
        wsl = [[jnp.where(low, wts[c][:, 2 * g:2 * g + 1], wts[c][:, 2 * g + 1:2 * g + 2])
                for c in range(3)] for g in slabs]
        ya = [wsl[g][0] * o1_ref[rs, acol[g]].astype(F32) + wsl[g][1] * o4_tok[rs, acol[g]]
              + wsl[g][2] * o16_tok[rs, acol[g]] for g in slabs]
        for g in slabs:
            ya_scr[rs, acol[g]] = ya[g].astype(BF16)
        x1 = x_ref[rs, :] + (_dot(ya_scr[rs, :], wo_ref[:ATT_WIDTH, :])
                             + _dot(ym_ref[rs, :], wo_ref[ATT_WIDTH:, :]))

        h2 = _rms(x1, gc_ref[...]).astype(BF16)
        qx = _dot(h2, wqx_ref[...]).astype(BF16)
        hds = range(X_HEADS)
        xcol = [slice((h // 2) * LANES, (h // 2 + 1) * LANES) for h in hds]
        hmask = [low if h % 2 == 0 else jnp.logical_not(low) for h in hds]
        qs = [jnp.where(hmask[h], qx[:, xcol[h]], jnp.zeros((sub_rows, LANES), BF16)) for h in hds]
        vs = [jnp.where(hmask[h], vm_ref[0, :, xcol[h]], jnp.zeros((km_ref.shape[1], LANES), BF16))
              for h in hds]
        s = [_dot_nt(qs[h], km_ref[0, :, xcol[h]]) for h in hds]
        m = [jnp.max(s[h], axis=-1, keepdims=True) for h in hds]
        p = [jnp.exp(s[h] - m[h]) for h in hds]
        l = [jnp.sum(p[h], axis=-1, keepdims=True) for h in hds]
        pv = [_dot(p[h].astype(BF16), vs[h]) * (1.0 / l[h]) for h in hds]
        for g in range(X_WIDTH // LANES):
            ox_scr[rs, xcol[2 * g]] = (pv[2 * g] + pv[2 * g + 1]).astype(BF16)
        x2 = x1 + _dot(ox_scr[rs, :], wox_ref[...])
        x2_ref[rs, :] = x2

        h3 = _rms(x2, gf_ref[...])
        hx_ref[rs, :x2_ref.shape[1]] = h3
        h_hi, h_lo = _hi_lo(h3)
        t = _dot(h_hi, wr_ref[...])
        logit_parts.append(t[:, :LANES] + t[:, LANES:] + _dot(h_lo, wr_ref[:, :LANES]) + br_ref[...])
    logits = jnp.concatenate(logit_parts, axis=0)

    lane = lax.broadcasted_iota(jnp.int32, (tm, LANES), 1).astype(F32)
    far = float(LANES)
    gmask = lane < N_GROUPS
    gl = jnp.where(gmask, logits, NEG_INF)
    gmax = jnp.max(gl, axis=-1, keepdims=True)
    gidx = jnp.min(jnp.where(gl == gmax, lane, far), axis=-1, keepdims=True)
    gsum = jnp.sum(jnp.where(gmask, jnp.exp(gl - gmax), 0.0), axis=-1, keepdims=True)
    g_w = 1.0 / gsum
    lo_lane = N_GROUPS + gidx * EXPERTS_PER_GROUP
    emask = (lane >= lo_lane) & (lane < lo_lane + EXPERTS_PER_GROUP)
    el = jnp.where(emask, logits, NEG_INF)
    t1 = jnp.max(el, axis=-1, keepdims=True)
    i1 = jnp.min(jnp.where(el == t1, lane, far), axis=-1, keepdims=True)
    el2 = jnp.where(lane == i1, NEG_INF, el)
    t2 = jnp.max(el2, axis=-1, keepdims=True)
    i2 = jnp.min(jnp.where(el2 == t2, lane, far), axis=-1, keepdims=True)
    ee = jnp.exp(t2 - t1)
    w1 = g_w / (1.0 + ee)
    w2 = w1 * ee

    first = i1 < i2
    ia = jnp.minimum(i1, i2)
    ib = jnp.maximum(i1, i2)
    wa = jnp.where(first, w1, w2)
    wb = jnp.where(first, w2, w1)
    la = ia - lo_lane
    lb = ib - lo_lane
    pair = la * (EXPERTS_PER_GROUP - 1) - la * (la - 1.0) * 0.5 + (lb - la - 1.0)
    bucket = gidx * len(PAIRS) + pair

    hit = lane == bucket
    cnt = jnp.where(hit, 1.0, 0.0)
    prefix = _dot(before_ref[...], cnt.astype(BF16)) + run_scr[0:1, :]
    rank = jnp.sum(jnp.where(hit, prefix, 0.0), axis=-1, keepdims=True)
    total = run_scr[0:1, :] + jnp.sum(cnt, axis=0, keepdims=True)
    run_scr[...] = jnp.broadcast_to(total, run_scr.shape)
    cnt_ref[...] = jnp.broadcast_to(total, cnt_ref.shape)

    fields = (ia - N_GROUPS, ib - N_GROUPS, wa, wb, rank, bucket)
    route = jnp.zeros((tm, LANES), F32)
    for idx, val in enumerate(fields):
        route = jnp.where(lane == float(idx), val, route)
    route_ref[...] = route
    hx_ref[:, x2_ref.shape[1]:] = route


def _post_mix(x2d, outs, lses, y_m, w_out, g_cross, w_q_x, k_mem, v_mem, w_o_x, g_ffn,
              w_router_g, b_router_g, w_router_e, b_router_e, tm, B):
    N, D = x2d.shape
    S = N // B
    tps = S // tm
    M = k_mem.shape[1]
    wr = jnp.pad(jnp.concatenate([w_router_g, w_router_e], axis=1).astype(F32),
                 ((0, 0), (0, LANES - N_GROUPS - N_EXPERTS)))
    wr_hi = wr.astype(BF16)
    wr_cat = jnp.concatenate([wr_hi, (wr - wr_hi.astype(F32)).astype(BF16)], axis=1)
    br = jnp.pad(jnp.concatenate([b_router_g, b_router_e]).astype(F32),
                 (0, LANES - N_GROUPS - N_EXPERTS)).reshape(1, LANES)
    const = lambda shape: pl.BlockSpec(shape, lambda i: (0,) * len(shape))
    row = lambda w: pl.BlockSpec((tm, w), lambda i: (i, 0))
    cls = lambda w: pl.BlockSpec((None, CLASSES, tm // CLASSES, w), lambda i: (i // tps, 0, i % tps, 0))
    memspec = pl.BlockSpec((1, M, X_WIDTH), lambda i: (i // tps, 0, 0))
    tok = np.arange(tm)
    unperm = np.zeros((tm, tm), np.float32)
    unperm[tok, (tok % CLASSES) * (tm // CLASSES) + tok // CLASSES] = 1.0
    return pl.pallas_call(
        functools.partial(_post_mix_kernel, tm=tm),
        grid=(N // tm,),
        in_specs=[row(D), row(ATT_WIDTH), cls(ATT_WIDTH), cls(ATT_WIDTH), row(LANES),
                  cls(LANES), cls(LANES), row(M_WIDTH), const((D, D)), const((1, D)),
                  const((D, X_WIDTH)), memspec, memspec, const((X_WIDTH, D)), const((1, D)),
                  const((D, 2 * LANES)), const((1, LANES)), const((tm, tm)), const((tm, tm))],
        out_specs=[row(D), row(D + ROUTE_W), row(LANES), const((8, LANES))],
        out_shape=[jax.ShapeDtypeStruct((N, D), F32), jax.ShapeDtypeStruct((N, D + ROUTE_W), F32),
                   jax.ShapeDtypeStruct((N, LANES), F32), jax.ShapeDtypeStruct((8, LANES), F32)],
        scratch_shapes=[pltpu.VMEM((8, LANES), F32), pltpu.VMEM((tm, LANES), F32),
                        pltpu.VMEM((tm, LANES), F32), pltpu.VMEM((tm, ATT_WIDTH), BF16),
                        pltpu.VMEM((tm, X_WIDTH), BF16)],
        compiler_params=_cparams(("arbitrary",)),
        name="post_mix",
    )(x2d, outs[0].reshape(N, ATT_WIDTH), outs[1], outs[2], lses[0].reshape(N, LANES), lses[1], lses[2],
      y_m, w_out.astype(BF16), g_cross.reshape(1, D), (w_q_x * (X_HEAD_DIM ** -0.5)).astype(BF16),
      k_mem, v_mem, w_o_x.astype(BF16), g_ffn.reshape(1, D), wr_cat, br,
      jnp.asarray(np.tril(np.ones((tm, tm), np.float32), -1), BF16),
      jnp.asarray(unperm, BF16))


def _tile_row(ref, row):
    return ref.at[row >> 3, pl.ds(row & (SUBLANES - 1), 1), :]


def _dispatch_kernel(pad_lo_ref, pad_hi_ref, nvalid_ref, dest_ref, h_ref, xs_ref, zero_scr, sem, zsem,
                     *, tm, nblk):
    def copy(i, u):
        return pltpu.make_async_copy(h_ref.at[i, pl.ds(u, 1), :],
                                     _tile_row(xs_ref, dest_ref[0, 0, i * SUBLANES + u]), sem)

    def start(i, carry):
        for u in range(SUBLANES):
            copy(i, u).start()
        return carry

    lax.fori_loop(0, tm // SUBLANES, start, 0)

    @pl.when(pl.program_id(0) == 0)
    def _():
        zero_scr[...] = jnp.zeros_like(zero_scr)
        groups = MOE_ROWS // SUBLANES

        def pad_copy(r):
            return pltpu.make_async_copy(zero_scr.at[0, pl.ds(0, 1), :], _tile_row(xs_ref, r), zsem)

        def tail_copy(blk):
            return pltpu.make_async_copy(zero_scr, xs_ref.at[pl.ds(blk * groups, groups), :, :], zsem)

        for q in range(N_BUCKETS):
            lax.fori_loop(pad_lo_ref[q], pad_hi_ref[q], lambda r, c: (pad_copy(r).start(), c)[1], 0)
        lax.fori_loop(nvalid_ref[0], nblk, lambda blk, c: (tail_copy(blk).start(), c)[1], 0)
        for q in range(N_BUCKETS):
            lax.fori_loop(pad_lo_ref[q], pad_hi_ref[q], lambda r, c: (pad_copy(r).wait(), c)[1], 0)
        lax.fori_loop(nvalid_ref[0], nblk, lambda blk, c: (tail_copy(blk).wait(), c)[1], 0)

    pltpu.make_async_copy(h_ref, xs_ref.at[pl.ds(0, tm // SUBLANES), :, :], sem).wait()


def _dispatch(hx, dest, pad_lo, pad_hi, nvalid, n_rows, tm):
    N, W = hx.shape
    grid_spec = pltpu.PrefetchScalarGridSpec(
        num_scalar_prefetch=3,
        grid=(N // tm,),
        in_specs=[pl.BlockSpec((1, 1, tm), lambda i, lo, hi, nv: (i, 0, 0), memory_space=pltpu.SMEM),
                  pl.BlockSpec((tm // SUBLANES, SUBLANES, W), lambda i, lo, hi, nv: (i, 0, 0))],
        out_specs=pl.BlockSpec(memory_space=pl.ANY),
        scratch_shapes=[pltpu.VMEM((MOE_ROWS // SUBLANES, SUBLANES, W), F32), pltpu.SemaphoreType.DMA(()),
                        pltpu.SemaphoreType.DMA(())],
    )
    return pl.pallas_call(
        functools.partial(_dispatch_kernel, tm=tm, nblk=n_rows // MOE_ROWS),
        grid_spec=grid_spec,
        out_shape=jax.ShapeDtypeStruct((n_rows // SUBLANES, SUBLANES, W), F32),
        compiler_params=_cparams(("arbitrary",)),
        name="moe_dispatch",
    )(pad_lo, pad_hi, nvalid, dest.reshape(N // tm, 1, tm),
      hx.reshape(N // SUBLANES, SUBLANES, W)).reshape(n_rows, W)


def _expert_kernel(blk_a_ref, blk_b_ref, nvalid_ref, x_ref, w1a_ref, w3a_ref, w2a_ref, w1b_ref, w3b_ref,
                   w2b_ref, y_ref):
    del blk_a_ref, blk_b_ref
    D = y_ref.shape[1]

    @pl.when(pl.program_id(0) < nvalid_ref[0])
    def _():
        xb = x_ref[:, :D].astype(BF16)
        wa = x_ref[:, D + 2:D + 3]
        wb = x_ref[:, D + 3:D + 4]
        up = [_dot(xb, w_ref[0]) for w_ref in (w1a_ref, w3a_ref, w1b_ref, w3b_ref)]
        hid = [(up[2 * e] * _sigmoid(up[2 * e]) * up[2 * e + 1]).astype(BF16) for e in range(2)]
        y_ref[...] = wa * _dot(hid[0], w2a_ref[0]) + wb * _dot(hid[1], w2b_ref[0])

    @pl.when(pl.program_id(0) >= nvalid_ref[0])
    def _():
        y_ref[...] = jnp.zeros_like(y_ref)


def _experts(xs, blk_a, blk_b, nvalid, w1, w3, w2):
    P, W = xs.shape
    D = W - ROUTE_W
    nblk = P // MOE_ROWS
    F = w1.shape[-1]
    up_a = pl.BlockSpec((1, D, F), lambda i, ba, bb, nv: (ba[i], 0, 0))
    up_b = pl.BlockSpec((1, D, F), lambda i, ba, bb, nv: (bb[i], 0, 0))
    grid_spec = pltpu.PrefetchScalarGridSpec(
        num_scalar_prefetch=3,
        grid=(nblk,),
        in_specs=[pl.BlockSpec((MOE_ROWS, W),
                               lambda i, ba, bb, nv: (jnp.maximum(jnp.minimum(i, nv[0] - 1), 0), 0)),
                  up_a, up_a, pl.BlockSpec((1, F, D), lambda i, ba, bb, nv: (ba[i], 0, 0)),
                  up_b, up_b, pl.BlockSpec((1, F, D), lambda i, ba, bb, nv: (bb[i], 0, 0))],
        out_specs=pl.BlockSpec((MOE_ROWS, D), lambda i, ba, bb, nv: (i, 0)),
    )
    w1b, w3b, w2b = w1.astype(BF16), w3.astype(BF16), w2.astype(BF16)
    return pl.pallas_call(
        _expert_kernel,
        grid_spec=grid_spec,
        out_shape=jax.ShapeDtypeStruct((P, D), F32),
        compiler_params=_cparams(("arbitrary",)),
        name="moe_experts",
    )(blk_a, blk_b, nvalid, xs, w1b, w3b, w2b, w1b, w3b, w2b)


def _combine_kernel(dest_ref, dest_next_ref, x_ref, g_ref, ys_ref, o_ref, buf, sem, *, tm, nsteps):
    step = pl.program_id(0)
    slot = step % 2
    groups = tm // SUBLANES

    def request(d_ref, s):
        def body(i, carry):
            for u in range(SUBLANES):
                pltpu.make_async_copy(_tile_row(ys_ref, d_ref[0, 0, i * SUBLANES + u]),
                                      buf.at[s, i, pl.ds(u, 1), :], sem.at[s]).start()
            return carry
        lax.fori_loop(0, groups, body, 0)

    @pl.when(step == 0)
    def _():
        request(dest_ref, 0)

    @pl.when(step + 1 < nsteps)
    def _():
        request(dest_next_ref, 1 - slot)

    pltpu.make_async_copy(ys_ref.at[pl.ds(0, groups), :, :], buf.at[slot], sem.at[slot]).wait()
    o_ref[...] = _rms(x_ref[...] + buf[slot].reshape(o_ref.shape), g_ref[...])


def _combine(x2, dest, ys, g_final, tm):
    N, D = x2.shape
    nsteps = N // tm
    dest3 = dest.reshape(nsteps, 1, tm)
    return pl.pallas_call(
        functools.partial(_combine_kernel, tm=tm, nsteps=nsteps),
        grid=(nsteps,),
        in_specs=[pl.BlockSpec((1, 1, tm), lambda i: (i, 0, 0), memory_space=pltpu.SMEM),
                  pl.BlockSpec((1, 1, tm), lambda i: (jnp.minimum(i + 1, nsteps - 1), 0, 0),
                               memory_space=pltpu.SMEM),
                  pl.BlockSpec((tm, D), lambda i: (i, 0)),
                  pl.BlockSpec((1, D), lambda i: (0, 0)),
                  pl.BlockSpec(memory_space=pl.ANY)],
        out_specs=pl.BlockSpec((tm, D), lambda i: (i, 0)),
        out_shape=jax.ShapeDtypeStruct((N, D), F32),
        scratch_shapes=[pltpu.VMEM((2, tm // SUBLANES, SUBLANES, D), F32),
                        pltpu.SemaphoreType.DMA((2,))],
        compiler_params=_cparams(("arbitrary",)),
        name="moe_combine",
    )(dest3, dest3, x2, g_final.reshape(1, D), ys.reshape(ys.shape[0] // SUBLANES, SUBLANES, D))


def kernel(x, mem, positions, g_mix, w_in, conv_w, conv_b, w_q_m, w_k_m, b_i, b_f, g_mhn, skip_m, w_out, g_cross, g_mem, w_q_x, w_kv_x, w_o_x, g_ffn, w_router_g, b_router_g, w_router_e, b_router_e, w1, w3, w2, g_final):
    B, S, D = x.shape
    N = B * S
    depth = g_mix.shape[0]
    tm_in = 512
    tm_post = 512
    tm_dispatch = 2048
    tm_combine = 1024
    assert all(window // d == ATT_BLOCK and CLASSES % d == 0 for window, d in DILATED_CONFIGS)
    assert B % M_BATCH == 0 and S % tm_in == 0 and S % tm_post == 0
    assert N % tm_dispatch == 0 and N % tm_combine == 0
    assert depth == 1
    for l in range(depth):
        x2d = x.reshape(N, D)
        pos = positions.astype(F32).reshape(N, 1)
        k_mem, v_mem = _mem_kv(mem, g_mem[l], w_kv_x[l])
        q, k, v, qc, kc, vc, mu, mvt, mo, gtt = _in_proj(
            x2d, pos, g_mix[l], w_in[l], b_i[l], b_f[l], tm_in, B)
        q, k, v = (t.reshape(B, 1, S, ATT_WIDTH) for t in (q, k, v))
        outs, lses = zip(*(_attention_config(*(qkv + (d,)))
                           for qkv, (_, d) in zip(((q, k, v), (qc, kc, vc), (qc, kc, vc)), DILATED_CONFIGS)))
        y_m = _mlstm(mu.reshape(B, S, M_WIDTH), mvt, mo.reshape(B, S, M_WIDTH), gtt, conv_w[l], conv_b[l],
                     w_q_m[l], w_k_m[l], g_mhn[l], skip_m[l]).reshape(N, M_WIDTH)
        x2, hx, route, cnt = _post_mix(x2d, outs, lses, y_m, w_out[l], g_cross[l], w_q_x[l], k_mem,
                                       v_mem, w_o_x[l], g_ffn[l], w_router_g[l], b_router_g[l],
                                       w_router_e[l], b_router_e[l], tm_post, B)

        rank = route[:, 4].astype(jnp.int32)
        bucket = route[:, 5].astype(jnp.int32)
        counts = cnt[0, :N_BUCKETS].astype(jnp.int32)
        padded = ((counts + MOE_ROWS - 1) // MOE_ROWS) * MOE_ROWS
        pends = jnp.cumsum(padded)
        pstarts = pends - padded
        onehot = bucket[:, None] == jnp.arange(N_BUCKETS, dtype=jnp.int32)
        dest = jnp.sum(jnp.where(onehot, pstarts, 0), axis=-1) + rank
        n_rows = N + N_BUCKETS * MOE_ROWS
        nblk = n_rows // MOE_ROWS
        blk_start = jnp.arange(nblk, dtype=jnp.int32) * MOE_ROWS
        blk_bucket = jnp.minimum(jnp.sum(pends[None, :] <= blk_start[:, None], axis=1), N_BUCKETS - 1)
        base = (np.arange(N_BUCKETS) // len(PAIRS)) * EXPERTS_PER_GROUP
        expert_a = jnp.asarray(base + np.array([p[0] for p in PAIRS] * N_GROUPS), jnp.int32)
        expert_b = jnp.asarray(base + np.array([p[1] for p in PAIRS] * N_GROUPS), jnp.int32)
        blk_a = jnp.take(expert_a, blk_bucket).astype(jnp.int32)
        blk_b = jnp.take(expert_b, blk_bucket).astype(jnp.int32)
        nvalid = (pends[-1] // MOE_ROWS).reshape(1).astype(jnp.int32)

        xs = _dispatch(hx, dest, (pstarts + counts).astype(jnp.int32), pends.astype(jnp.int32),
                       nvalid, n_rows, tm_dispatch)
        ys = _experts(xs, blk_a, blk_b, nvalid, w1[l], w3[l], w2[l])
        x = _combine(x2, dest, ys, g_final, tm_combine).reshape(B, S, D)
    return x
```

```python
import functools

import jax
import jax.numpy as jnp
import numpy as np
from jax import lax
from jax.experimental import pallas as pl
from jax.experimental.pallas import tpu as pltpu

F32 = jnp.float32
BF16 = jnp.bfloat16

EPS = 1e-6
LANES = 128
SUBLANES = 8
ATT_HEAD_DIM = 64
ATT_WIDTH = 512
DILATED_CONFIGS = ((128, 1), (512, 4), (2048, 16))
CLASSES = 16
ATT_BLOCK = 128
ATT_GROUP = 1
ROPE_THETA = 500000.0
ROPE_DIM = ATT_HEAD_DIM // 4
M_WIDTH = 512
M_HEADS = 4
M_HEAD_DIM = 128
CONV_WIDTH = 4
M_CHUNK = 128
M_BATCH = 4
X_HEADS = 4
X_HEAD_DIM = 64
X_WIDTH = X_HEADS * X_HEAD_DIM
N_GROUPS = 4
EXPERTS_PER_GROUP = 4
N_EXPERTS = 16
TOP_K = 2
EXPERT_FF = 512
PAIRS = tuple((a, b) for a in range(EXPERTS_PER_GROUP) for b in range(a + 1, EXPERTS_PER_GROUP))
N_BUCKETS = N_GROUPS * len(PAIRS)
MOE_ROWS = 512
ROUTE_W = LANES
POST_SPLIT = 1
POST_ROWS = 64
VMEM_LIMIT = 56 * 1024 * 1024

NEG_INF = float("-inf")


def _cparams(sem):
    return pltpu.CompilerParams(dimension_semantics=sem, vmem_limit_bytes=VMEM_LIMIT)


def _rms(x, g):
    return x * lax.rsqrt(jnp.mean(x * x, axis=-1, keepdims=True) + EPS) * g


def _dot(a, b):
    return jnp.dot(a, b, preferred_element_type=F32)


def _dot_nt(a, b):
    return lax.dot_general(a, b, (((1,), (1,)), ((), ())), preferred_element_type=F32)


def _dot_tn(a, b):
    return lax.dot_general(a, b, (((0,), (0,)), ((), ())), preferred_element_type=F32)


def _hi_lo(a):
    hi = a.astype(BF16)
    return hi, (a - hi.astype(F32)).astype(BF16)


def _log_sigmoid(x):
    return jnp.minimum(x, 0.0) - jnp.log(1.0 + jnp.exp(-jnp.abs(x)))


def _sigmoid(x):
    return 0.5 * jnp.tanh(0.5 * x) + 0.5


def _mem_kv_kernel(mem_ref, g_ref, w_ref, k_ref, v_ref):
    h = _rms(mem_ref[0], g_ref[...]).astype(BF16)
    kv = _dot(h, w_ref[...])
    k_ref[0] = kv[:, :X_WIDTH].astype(BF16)
    v_ref[0] = kv[:, X_WIDTH:].astype(BF16)


def _mem_kv(mem, g_mem, w_kv):
    B, M, D = mem.shape
    return pl.pallas_call(
        _mem_kv_kernel,
        grid=(B,),
        in_specs=[pl.BlockSpec((1, M, D), lambda b: (b, 0, 0)),
                  pl.BlockSpec((1, D), lambda b: (0, 0)),
                  pl.BlockSpec((D, 2 * X_WIDTH), lambda b: (0, 0))],
        out_specs=[pl.BlockSpec((1, M, X_WIDTH), lambda b: (b, 0, 0)),
                   pl.BlockSpec((1, M, X_WIDTH), lambda b: (b, 0, 0))],
        out_shape=[jax.ShapeDtypeStruct((B, M, X_WIDTH), BF16)] * 2,
        compiler_params=_cparams(("arbitrary",)),
        name="mem_kv",
    )(mem, g_mem.reshape(1, D), w_kv.astype(BF16))


def _in_proj_kernel(x_ref, pos_ref, g_ref, wqkv_ref, wm_ref, wmvt_ref, bgt_ref,
                    invf_ref, sgn_ref, spread_ref, perm_ref,
                    q_ref, k_ref, v_ref, qc_ref, kc_ref, vc_ref,
                    mu_ref, mvt_ref, mo_ref, gtt_ref, *, tm):
    hb = _rms(x_ref[...], g_ref[...]).astype(BF16)
    qkv = _dot(hb, wqkv_ref[...])
    slots = tm // pos_ref.shape[0]
    ang = pos_ref[...] * invf_ref[...]
    rows = pos_ref.shape[0]
    cs_hl = jnp.concatenate(_hi_lo(jnp.where(sgn_ref[0:1, :] > 0.0, jnp.sin(ang), jnp.cos(ang))), axis=0)
    spread = [_dot(cs_hl, spread_ref[i]) for i in range(slots)]
    spread = [sp[:rows] + sp[rows:] for sp in spread]
    cos = jnp.concatenate([sp[:, :LANES] for sp in spread], axis=0) + sgn_ref[1:2, :]
    sin = jnp.concatenate([sp[:, LANES:] for sp in spread], axis=0)
    half = ROPE_DIM // 2
    first_half = sgn_ref[2:3, :] > 0.0

    def emit_classes(nat_ref, cls_ref):
        rows = tm // CLASSES
        by_class = _dot(perm_ref[...], nat_ref[...])
        for r in range(CLASSES):
            cls_ref[0, r] = by_class[r * rows:(r + 1) * rows].astype(BF16)

    v_ref[...] = qkv[:, 2 * ATT_WIDTH:].astype(BF16)
    mm = _dot(hb, wm_ref[...])
    for which, refs in ((0, (q_ref, qc_ref)), (1, (k_ref, kc_ref))):
        for g in range(ATT_WIDTH // LANES):
            cols = slice(g * LANES, (g + 1) * LANES)
            t = qkv[:, which * ATT_WIDTH + g * LANES: which * ATT_WIDTH + (g + 1) * LANES]
            partner = jnp.where(first_half, pltpu.roll(t, LANES - half, 1), pltpu.roll(t, half, 1))
            refs[0][:, cols] = (t * cos + partner * sin).astype(BF16)
    mu_ref[...] = mm[:, :M_WIDTH].astype(BF16)
    mo_ref[...] = mm[:, M_WIDTH:].astype(BF16)
    mvt = _dot_nt(wmvt_ref[...], hb)
    emit_classes(v_ref, vc_ref)
    for c in range(tm // M_CHUNK):
        mvt_ref[c] = mvt[:M_WIDTH, c * M_CHUNK:(c + 1) * M_CHUNK].astype(BF16)
    emit_classes(q_ref, qc_ref)
    gtt_ref[0] = mvt[M_WIDTH:M_WIDTH + 2 * M_HEADS, :] + bgt_ref[...]
    emit_classes(k_ref, kc_ref)


def _in_proj(x2d, pos, g_mix, w_in, b_i, b_f, tm, B):
    N, D = x2d.shape
    S = N // B
    tps = S // tm
    A = ATT_WIDTH
    wq = w_in[:, :A] * (ATT_HEAD_DIM ** -0.5)
    wqkv = jnp.concatenate([wq, w_in[:, A:3 * A]], axis=1).astype(BF16)
    o = 3 * A
    wm = jnp.concatenate([w_in[:, o:o + M_WIDTH], w_in[:, o + 2 * M_WIDTH:o + 3 * M_WIDTH]],
                         axis=1).astype(BF16)
    wgates = w_in[:, 3 * A + 3 * M_WIDTH:]
    wmvt = jnp.pad(jnp.concatenate([w_in[:, o + M_WIDTH:o + 2 * M_WIDTH], wgates], axis=1).T,
                   ((0, 2 * SUBLANES - 2 * M_HEADS), (0, 0))).astype(BF16)
    bgt = jnp.concatenate([b_i, b_f]).astype(F32).reshape(2 * M_HEADS, 1)
    half = ROPE_DIM // 2
    slots = LANES // ROPE_DIM
    lane = np.arange(LANES)
    c = lane % ROPE_DIM
    j = lane % ATT_HEAD_DIM
    inv_freq = ROPE_THETA ** (-jnp.arange(0, ROPE_DIM, 2, dtype=F32) / ROPE_DIM)
    invf = inv_freq[c % half].reshape(1, LANES).astype(F32)
    sgn = np.zeros((8, LANES), np.float32)
    sgn[0] = c >= half
    sgn[1] = j >= ROPE_DIM
    sgn[2] = j < half
    spread = np.zeros((slots, LANES, 2 * LANES), np.float32)
    for i in range(slots):
        for f in range(half):
            spread[i, ROPE_DIM * i + f, lane[(j < ROPE_DIM) & (j % half == f)]] = 1.0
            spread[i, ROPE_DIM * i + half + f, LANES + lane[j == f]] = -1.0
            spread[i, ROPE_DIM * i + half + f, LANES + lane[j == half + f]] = 1.0
    posp = pos.reshape(N // tm, slots, tm // slots).transpose(0, 2, 1)
    posp = jnp.repeat(posp, ROPE_DIM, axis=2).reshape(N // slots, LANES)
    tok = np.arange(tm)
    perm = np.zeros((tm, tm), np.float32)
    perm[(tok % CLASSES) * (tm // CLASSES) + tok // CLASSES, tok] = 1.0
    const = lambda shape: pl.BlockSpec(shape, lambda i: (0,) * len(shape))
    row = lambda w: pl.BlockSpec((tm, w), lambda i: (i, 0))
    cls = pl.BlockSpec((1, CLASSES, tm // CLASSES, A), lambda i: (i // tps, 0, i % tps, 0))
    cls_shape = jax.ShapeDtypeStruct((B, CLASSES, S // CLASSES, A), BF16)
    return pl.pallas_call(
        functools.partial(_in_proj_kernel, tm=tm),
        grid=(N // tm,),
        in_specs=[row(D), pl.BlockSpec((tm // slots, LANES), lambda i: (i, 0)), const((1, D)),
                  const((D, 3 * A)), const((D, 2 * M_WIDTH)),
                  const((M_WIDTH + 2 * SUBLANES, D)), const((2 * M_HEADS, 1)),
                  const((1, LANES)), const((8, LANES)), const((slots, LANES, 2 * LANES)),
                  const((tm, tm))],
        out_specs=[row(A), row(A), row(A), cls, cls, cls, row(M_WIDTH),
                   pl.BlockSpec((tm // M_CHUNK, M_WIDTH, M_CHUNK), lambda i: (i, 0, 0)), row(M_WIDTH),
                   pl.BlockSpec((1, 2 * M_HEADS, tm), lambda i: (i // tps, 0, i % tps))],
        out_shape=[jax.ShapeDtypeStruct((N, A), BF16)] * 3 + [cls_shape] * 3
        + [jax.ShapeDtypeStruct((N, M_WIDTH), BF16),
           jax.ShapeDtypeStruct((N // M_CHUNK, M_WIDTH, M_CHUNK), BF16),
           jax.ShapeDtypeStruct((N, M_WIDTH), BF16),
           jax.ShapeDtypeStruct((B, 2 * M_HEADS, S), F32)],
        compiler_params=_cparams(("arbitrary",)),
        name="in_proj",
    )(x2d, posp, g_mix.reshape(1, D), wqkv, wm, wmvt, bgt, invf, jnp.asarray(sgn),
      jnp.asarray(spread, BF16), jnp.asarray(perm, BF16))


def _attn_kernel(q_ref, kc_ref, kp_ref, vc_ref, vp_ref, o_ref, l_ref, kbuf, vbuf, bias_scr, *, qb, nc):
    blk = ATT_BLOCK
    piece = blk // nc
    nsub = qb // blk
    first = pl.program_id(2) == 0
    pair_low = (lax.broadcasted_iota(jnp.int32, (1, ATT_WIDTH), 1) & (LANES - 1)) < ATT_HEAD_DIM

    def put_v(dst, v):
        zero = jnp.zeros_like(v)
        vbuf[0, dst, :] = jnp.where(pair_low, v, zero)
        vbuf[1, dst, :] = jnp.where(pair_low, zero, v)

    for c in range(nc):
        kbuf[c * piece:(c + 1) * piece, :] = kp_ref[c]
        put_v(slice(c * piece, (c + 1) * piece), vp_ref[c])
        for sub in range(nsub):
            dst = slice((sub + 1) * blk + c * piece, (sub + 1) * blk + (c + 1) * piece)
            src = slice(sub * piece, (sub + 1) * piece)
            kbuf[dst, :] = kc_ref[c, src, :]
            put_v(dst, vc_ref[c, src, :])

    def pos(p):
        p = p & (blk - 1)
        return nc * (p & (piece - 1)) + (p >> (piece.bit_length() - 1))

    qi = lax.broadcasted_iota(jnp.int32, (blk, 2 * blk), 0)
    ki = lax.broadcasted_iota(jnp.int32, (blk, 2 * blk), 1)
    dist = pos(qi) - pos(ki) + jnp.where(ki < blk, blk, 0)
    band = (dist >= 0) & (dist <= blk)
    band_first = band & ((ki >= blk) | jnp.logical_not(first))
    bias_scr[0] = jnp.where(band_first, 0.0, NEG_INF)
    bias_scr[1] = jnp.where(band, 0.0, NEG_INF)
    lane = lax.broadcasted_iota(jnp.int32, (1, LANES), 1)
    lane_full = lax.broadcasted_iota(jnp.int32, (blk, LANES), 1)
    low = lane < ATT_HEAD_DIM
    nh = 2 * (ATT_WIDTH // LANES)
    for sub0 in range(0, nsub, ATT_GROUP):
        subs = range(sub0, min(sub0 + ATT_GROUP, nsub))
        chains = [(sub, h) for sub in subs for h in range(nh)]
        ids = range(len(chains))
        col = [slice((h // 2) * LANES, (h // 2 + 1) * LANES) for _, h in chains]
        prow = [slice(sub * piece, (sub + 1) * piece) for sub, _ in chains]
        krow = [slice(sub * blk, (sub + 2) * blk) for sub, _ in chains]
        qs = [jnp.concatenate([q_ref[c, prow[i], col[i]] for c in range(nc)], axis=0) for i in ids]
        qs = [jnp.where(low if h % 2 == 0 else jnp.logical_not(low), qs[i], jnp.zeros_like(qs[i]))
              for i, (_, h) in enumerate(chains)]
        s = [_dot_nt(qs[i], kbuf[krow[i], col[i]]) + bias_scr[min(sub, 1)]
             for i, (sub, _) in enumerate(chains)]
        m = [jnp.max(s[i], axis=-1, keepdims=True) for i in ids]
        p = [jnp.exp(s[i] - m[i]) for i in ids]
        l = [jnp.sum(p[i], axis=-1, keepdims=True) for i in ids]
        pv = [_dot(p[i].astype(BF16), vbuf[h % 2, krow[i], col[i]]) * (1.0 / l[i])
              for i, (_, h) in enumerate(chains)]
        for n, sub in enumerate(subs):
            base = n * nh
            m_all = jnp.zeros((blk, LANES), F32)
            l_all = jnp.ones((blk, LANES), F32)
            for h in range(nh):
                m_all = jnp.where(lane_full == h, m[base + h], m_all)
                l_all = jnp.where(lane_full == h, l[base + h], l_all)
            for g in range(ATT_WIDTH // LANES):
                acc = (pv[base + 2 * g] + pv[base + 2 * g + 1]).astype(BF16)
                for c in range(nc):
                    o_ref[c, prow[base], col[base + 2 * g]] = acc[c * piece:(c + 1) * piece]
            lse_all = m_all + jnp.log(l_all)
            for c in range(nc):
                l_ref[c, prow[base], :] = lse_all[c * piece:(c + 1) * piece]


def _attention_config(q, k, v, d):
    B, C, L, W = q.shape
    nc = C // d
    qb = min(512, L * nc)
    rows = qb // nc
    piece = ATT_BLOCK // nc
    nsub = qb // ATT_BLOCK
    view = lambda t: t.reshape(B, nc, d, L, t.shape[-1])
    cur = lambda w: pl.BlockSpec((None, nc, None, rows, w), lambda b, r, j: (b, 0, r, j, 0))
    prev = pl.BlockSpec((None, nc, None, piece, W),
                        lambda b, r, j: (b, 0, r, jnp.maximum(j * nsub - 1, 0), 0))
    o, lse = pl.pallas_call(
        functools.partial(_attn_kernel, qb=qb, nc=nc),
        grid=(B, d, L // rows),
        in_specs=[cur(W), cur(W), prev, cur(W), prev],
        out_specs=[cur(W), cur(LANES)],
        out_shape=[jax.ShapeDtypeStruct((B, nc, d, L, W), BF16),
                   jax.ShapeDtypeStruct((B, nc, d, L, LANES), F32)],
        scratch_shapes=[pltpu.VMEM((qb + ATT_BLOCK, W), BF16), pltpu.VMEM((2, qb + ATT_BLOCK, W), BF16),
                        pltpu.VMEM((2, ATT_BLOCK, 2 * ATT_BLOCK), F32)],
        compiler_params=_cparams(("arbitrary", "arbitrary", "arbitrary")),
        name=f"attention_d{d}",
    )(view(q), view(k), view(k), view(v), view(v))
    return o.reshape(B, C, L, W), lse.reshape(B, C, L, LANES)


def _mlstm_kernel(mu_ref, mvt_ref, mo_ref, gtt_ref, cw_ref, cb_ref, wqk_ref, gn_ref,
                  sk_ref, y_ref, c_scr, n_scr, m_scr, ext_scr):
    L = M_CHUNK
    H = M_HEADS
    HD = M_HEAD_DIM

    @pl.when(pl.program_id(1) == 0)
    def _():
        c_scr[...] = jnp.zeros_like(c_scr)
        n_scr[...] = jnp.zeros_like(n_scr)
        m_scr[...] = jnp.zeros_like(m_scr)
        for bb in range(M_BATCH):
            ext_scr[bb, 0:8, :] = jnp.zeros((8, M_WIDTH), F32)

    ri = lax.broadcasted_iota(jnp.int32, (L, L), 0)
    ci = lax.broadcasted_iota(jnp.int32, (L, L), 1)
    causal_t = ri <= ci
    triu = jnp.where(causal_t, 1.0, 0.0).astype(BF16)
    scale = HD ** -0.5

    bbs = range(M_BATCH)
    cs, gtt, b_rows, key_cols = [], [], [], []
    for bb in bbs:
        mu = mu_ref[bb].astype(F32)
        ext_scr[bb, 8:8 + L, :] = mu
        conv = cb_ref[...]
        for jj in range(CONV_WIDTH):
            lo = 8 - (CONV_WIDTH - 1) + jj
            conv = conv + ext_scr[bb, lo:lo + L, :] * cw_ref[jj:jj + 1, :]
        ext_scr[bb, 0:8, :] = mu[L - 8:, :]
        cs.append(conv * _sigmoid(conv))
    cb16 = [cs[bb].astype(BF16) for bb in bbs]
    for bb in bbs:
        gtt.append(gtt_ref[bb])
        hi_r, lo_r = _hi_lo(_log_sigmoid(gtt[bb]))
        b_rows.append(_dot(hi_r, triu) + _dot(lo_r, triu))
        key_rows = gtt[bb] - pltpu.roll(b_rows[bb], H, 0)
        key_cols.append(jnp.concatenate([key_rows, jnp.zeros((L - 2 * H, L), F32)], axis=0).T)

    prs = [(bb, hd) for bb in bbs for hd in range(H)]
    ids = range(len(prs))
    col = [slice(hd * HD, (hd + 1) * HD) for _, hd in prs]
    qk = [_dot(cb16[bb][:, col[i]], wqk_ref[hd]) for i, (bb, hd) in enumerate(prs)]
    qb = [qk[i][:, :HD].astype(BF16) for i in ids]
    kb = [(qk[i][:, HD:] * scale).astype(BF16) for i in ids]
    vt = [mvt_ref[bb, col[i], :] for i, (bb, _) in enumerate(prs)]
    b_r = [b_rows[bb][H + hd:H + hd + 1, :] for bb, hd in prs]
    b_last = [b_r[i][:, L - 1:L] for i in ids]
    m_prev = [m_scr[i:i + 1, 0:1] for i in ids]
    ct_prev = [c_scr[i] for i in ids]
    n_prev = [n_scr[i] for i in ids]

    dlog = [jnp.where(causal_t, b_r[i] + key_cols[bb][:, hd:hd + 1], NEG_INF)
            for i, (bb, hd) in enumerate(prs)]
    m_inter = [b_r[i] + m_prev[i] for i in ids]
    m_t = [jnp.maximum(m_inter[i], jnp.max(dlog[i], axis=0, keepdims=True)) for i in ids]
    inter_w = [jnp.exp(m_inter[i] - m_t[i]) for i in ids]
    st_w = [_dot_nt(kb[i], qb[i]) * jnp.exp(dlog[i] - m_t[i]) for i in ids]
    num = [_dot(vt[i], st_w[i].astype(BF16)) + inter_w[i] * _dot_nt(ct_prev[i].astype(BF16), qb[i])
           for i in ids]
    den = [jnp.sum(st_w[i], axis=0, keepdims=True)
           + inter_w[i] * _dot_nt(n_prev[i].astype(BF16), qb[i])[0:1, :] for i in ids]
    ht = [num[i] * (1.0 / jnp.maximum(jnp.abs(den[i]), jnp.exp(-m_t[i]))) for i in ids]
    ht = [ht[i] * lax.rsqrt(jnp.mean(ht[i] * ht[i], axis=0, keepdims=True) + EPS) for i in ids]
    for i, (bb, _) in enumerate(prs):
        hn = ht[i].T * gn_ref[:, col[i]]
        y = _sigmoid(mo_ref[bb, :, col[i]].astype(F32)) * (hn + sk_ref[:, col[i]] * cs[bb][:, col[i]])
        y_ref[bb, :, col[i]] = y.astype(BF16)

    g_r = [b_last[i] - b_r[i] + gtt[bb][hd:hd + 1, :] for i, (bb, hd) in enumerate(prs)]
    m_loc = [jnp.max(g_r[i], axis=1, keepdims=True) for i in ids]
    wk_r = [jnp.exp(g_r[i] - m_loc[i]) for i in ids]
    c_loc = [_dot((vt[i].astype(F32) * wk_r[i]).astype(BF16), kb[i]) for i in ids]
    n_loc = [_dot(jnp.broadcast_to(wk_r[i], (8, L)).astype(BF16), kb[i]) for i in ids]
    for i in ids:
        m_new = jnp.maximum(b_last[i] + m_prev[i], m_loc[i])
        a = jnp.exp(b_last[i] + m_prev[i] - m_new)
        cc = jnp.exp(m_loc[i] - m_new)
        c_scr[i] = a * ct_prev[i] + cc * c_loc[i]
        n_scr[i] = a * n_prev[i] + cc * n_loc[i]
        m_scr[i:i + 1, :] = jnp.broadcast_to(m_new, (1, LANES))


def _mlstm(mu, mvt, mo, gtt, conv_w, conv_b, w_q_m, w_k_m, g_mhn, skip_m):
    B, S, W = mu.shape
    L = M_CHUNK
    nc = S // L
    nb = M_BATCH
    tok = pl.BlockSpec((nb, L, W), lambda b, c: (b, c, 0))
    const = lambda shape: pl.BlockSpec(shape, lambda b, c: (0,) * len(shape))
    wqk = jnp.concatenate([w_q_m, w_k_m], axis=-1).astype(BF16)
    return pl.pallas_call(
        _mlstm_kernel,
        grid=(B // nb, nc),
        in_specs=[tok, pl.BlockSpec((nb, None, W, L), lambda b, c: (b, c, 0, 0)), tok,
                  pl.BlockSpec((nb, 2 * M_HEADS, L), lambda b, c: (b, 0, c)),
                  const((CONV_WIDTH, W)), const((1, W)),
                  const((M_HEADS, M_HEAD_DIM, 2 * M_HEAD_DIM)), const((1, W)), const((1, W))],
        out_specs=tok,
        out_shape=jax.ShapeDtypeStruct((B, S, W), BF16),
        scratch_shapes=[pltpu.VMEM((nb * M_HEADS, M_HEAD_DIM, M_HEAD_DIM), F32),
                        pltpu.VMEM((nb * M_HEADS, 8, M_HEAD_DIM), F32),
                        pltpu.VMEM((nb * M_HEADS, LANES), F32),
                        pltpu.VMEM((nb, 8 + L, W), F32)],
        compiler_params=_cparams(("arbitrary", "arbitrary")),
        name="mlstm",
    )(mu, mvt.reshape(B, nc, W, L), mo, gtt, conv_w.astype(F32), conv_b.reshape(1, W).astype(F32), wqk,
      g_mhn.reshape(1, W).astype(F32), skip_m.reshape(1, W).astype(F32))


def _post_mix_kernel(x_ref, o1_ref, o4_ref, o16_ref, l1_ref, l4_ref, l16_ref, ym_ref, wo_ref, gc_ref,
                     wqx_ref, km_ref, vm_ref, wox_ref, gf_ref, wr_ref, br_ref, before_ref, unperm_ref,
                     x2_ref, hx_ref, route_ref, cnt_ref,
                     run_scr, l4_scr, l16_scr, ya_scr, ox_scr, hb_scr, hl_scr, *, tm):
    @pl.when(pl.program_id(0) == 0)
    def _():
        run_scr[...] = jnp.zeros_like(run_scr)

    rows = tm // CLASSES
    o4_tok = _dot(unperm_ref[...], o4_ref[...].reshape(tm, ATT_WIDTH))
    o16_tok = _dot(unperm_ref[...], o16_ref[...].reshape(tm, ATT_WIDTH))
    for src, dst in ((l4_ref, l4_scr), (l16_ref, l16_scr)):
        for r in range(CLASSES):
            dst[pl.ds(r, rows, stride=CLASSES), :] = src[r]

    lane1 = lax.broadcasted_iota(jnp.int32, (1, LANES), 1)
    low = lane1 < ATT_HEAD_DIM

    sub_rows = tm // POST_SPLIT
    parts = range(POST_SPLIT)
    row_sl = [slice(part * sub_rows, (part + 1) * sub_rows) for part in parts]

    def row_blocks(rs):
        return [slice(r, r + POST_ROWS) for r in range(rs.start, rs.stop, POST_ROWS)]

    def mix_stage(rs):
        l1, l2, l3 = l1_ref[rs, :], l4_scr[rs, :], l16_scr[rs, :]
        mx = jnp.maximum(jnp.maximum(l1, l2), l3)
        e1, e2, e3 = jnp.exp(l1 - mx), jnp.exp(l2 - mx), jnp.exp(l3 - mx)
        inv = 1.0 / (e1 + e2 + e3)
        wts = (e1 * inv, e2 * inv, e3 * inv)
        slabs = range(ATT_WIDTH // LANES)
        acol = [slice(g * LANES, (g + 1) * LANES) for g in slabs]
        wsl = [[jnp.where(low, wts[c][:, 2 * g:2 * g + 1], wts[c][:, 2 * g + 1:2 * g + 2])
                for c in range(3)] for g in slabs]
        ya = [wsl[g][0] * o1_ref[rs, acol[g]].astype(F32) + wsl[g][1] * o4_tok[rs, acol[g]]
              + wsl[g][2] * o16_tok[rs, acol[g]] for g in slabs]
        for g in slabs:
            ya_scr[rs, acol[g]] = ya[g].astype(BF16)
        mix = _dot(ya_scr[rs, :], wo_ref[:ATT_WIDTH, :]) + _dot(ym_ref[rs, :], wo_ref[ATT_WIDTH:, :])
        for k, rb in enumerate(row_blocks(rs)):
            x1 = x_ref[rb, :] + mix[k * POST_ROWS:(k + 1) * POST_ROWS, :]
            x2_ref[rb, :] = x1
            hb_scr[rb, :] = _rms(x1, gc_ref[...]).astype(BF16)

    def cross_stage(rs):
        qx = _dot(hb_scr[rs, :], wqx_ref[...]).astype(BF16)
        hds = range(X_HEADS)
        xcol = [slice((h // 2) * LANES, (h // 2 + 1) * LANES) for h in hds]
        hmask = [low if h % 2 == 0 else jnp.logical_not(low) for h in hds]
        qs = [jnp.where(hmask[h], qx[:, xcol[h]], jnp.zeros((sub_rows, LANES), BF16)) for h in hds]
        vs = [jnp.where(hmask[h], vm_ref[0, :, xcol[h]], jnp.zeros((km_ref.shape[1], LANES), BF16))
              for h in hds]
        s = [_dot_nt(qs[h], km_ref[0, :, xcol[h]]) for h in hds]
        m = [jnp.max(s[h], axis=-1, keepdims=True) for h in hds]
        p = [jnp.exp(s[h] - m[h]) for h in hds]
        l = [jnp.sum(p[h], axis=-1, keepdims=True) for h in hds]
        pv = [_dot(p[h].astype(BF16), vs[h]) * (1.0 / l[h]) for h in hds]
        for g in range(X_WIDTH // LANES):
            ox_scr[rs, xcol[2 * g]] = (pv[2 * g] + pv[2 * g + 1]).astype(BF16)
        upd = _dot(ox_scr[rs, :], wox_ref[...])
        for k, rb in enumerate(row_blocks(rs)):
            x2 = x2_ref[rb, :] + upd[k * POST_ROWS:(k + 1) * POST_ROWS, :]
            x2_ref[rb, :] = x2
            h3 = _rms(x2, gf_ref[...])
            hx_ref[rb, :x2_ref.shape[1]] = h3
            hb_scr[rb, :], hl_scr[rb, :] = _hi_lo(h3)

    def router_stage(rs):
        t = _dot(hb_scr[rs, :], wr_ref[...])
        return t[:, :LANES] + t[:, LANES:] + _dot(hl_scr[rs, :], wr_ref[:, :LANES]) + br_ref[...]

    for p in parts:
        mix_stage(row_sl[p])
    for p in parts:
        cross_stage(row_sl[p])
    picks = [_pick_experts(router_stage(row_sl[p])) for p in parts]
    ia, ib, wa, wb, bucket = (jnp.concatenate([pk[f] for pk in picks], axis=0) for f in range(5))
    lane = lax.broadcasted_iota(jnp.int32, (tm, LANES), 1).astype(F32)

    hit = lane == bucket
    cnt = jnp.where(hit, 1.0, 0.0)
    prefix = _dot(before_ref[...], cnt.astype(BF16)) + run_scr[0:1, :]
    rank = jnp.sum(jnp.where(hit, prefix, 0.0), axis=-1, keepdims=True)
    total = run_scr[0:1, :] + jnp.sum(cnt, axis=0, keepdims=True)
    run_scr[...] = jnp.broadcast_to(total, run_scr.shape)
    cnt_ref[...] = jnp.broadcast_to(total, cnt_ref.shape)

    fields = (ia - N_GROUPS, ib - N_GROUPS, wa, wb, rank, bucket)
    route = jnp.zeros((tm, LANES), F32)
    for idx, val in enumerate(fields):
        route = jnp.where(lane == float(idx), val, route)
    route_ref[...] = route
    hx_ref[:, x2_ref.shape[1]:] = route


def _pick_experts(logits):
    lane = lax.broadcasted_iota(jnp.int32, logits.shape, 1).astype(F32)
    far = float(LANES)
    gmask = lane < N_GROUPS
    gl = jnp.where(gmask, logits, NEG_INF)
    gmax = jnp.max(gl, axis=-1, keepdims=True)
    gidx = jnp.min(jnp.where(gl == gmax, lane, far), axis=-1, keepdims=True)
    gsum = jnp.sum(jnp.where(gmask, jnp.exp(gl - gmax), 0.0), axis=-1, keepdims=True)
    g_w = 1.0 / gsum
    lo_lane = N_GROUPS + gidx * EXPERTS_PER_GROUP
    emask = (lane >= lo_lane) & (lane < lo_lane + EXPERTS_PER_GROUP)
    el = jnp.where(emask, logits, NEG_INF)
    t1 = jnp.max(el, axis=-1, keepdims=True)
    i1 = jnp.min(jnp.where(el == t1, lane, far), axis=-1, keepdims=True)
    el2 = jnp.where(lane == i1, NEG_INF, el)
    t2 = jnp.max(el2, axis=-1, keepdims=True)
    i2 = jnp.min(jnp.where(el2 == t2, lane, far), axis=-1, keepdims=True)
    ee = jnp.exp(t2 - t1)
    w1 = g_w / (1.0 + ee)
    w2 = w1 * ee

    first = i1 < i2
    ia = jnp.minimum(i1, i2)
    ib = jnp.maximum(i1, i2)
    wa = jnp.where(first, w1, w2)
    wb = jnp.where(first, w2, w1)
    la = ia - lo_lane
    lb = ib - lo_lane
    pair = la * (EXPERTS_PER_GROUP - 1) - la * (la - 1.0) * 0.5 + (lb - la - 1.0)
    return ia, ib, wa, wb, gidx * len(PAIRS) + pair


def _post_mix(x2d, outs, lses, y_m, w_out, g_cross, w_q_x, k_mem, v_mem, w_o_x, g_ffn,
              w_router_g, b_router_g, w_router_e, b_router_e, tm, B):
    N, D = x2d.shape
    S = N // B
    tps = S // tm
    M = k_mem.shape[1]
    wr = jnp.pad(jnp.concatenate([w_router_g, w_router_e], axis=1).astype(F32),
                 ((0, 0), (0, LANES - N_GROUPS - N_EXPERTS)))
    wr_hi = wr.astype(BF16)
    wr_cat = jnp.concatenate([wr_hi, (wr - wr_hi.astype(F32)).astype(BF16)], axis=1)
    br = jnp.pad(jnp.concatenate([b_router_g, b_router_e]).astype(F32),
                 (0, LANES - N_GROUPS - N_EXPERTS)).reshape(1, LANES)
    const = lambda shape: pl.BlockSpec(shape, lambda i: (0,) * len(shape))
    row = lambda w: pl.BlockSpec((tm, w), lambda i: (i, 0))
    cls = lambda w: pl.BlockSpec((None, CLASSES, tm // CLASSES, w), lambda i: (i // tps, 0, i % tps, 0))
    memspec = pl.BlockSpec((1, M, X_WIDTH), lambda i: (i // tps, 0, 0))
    tok = np.arange(tm)
    unperm = np.zeros((tm, tm), np.float32)
    unperm[tok, (tok % CLASSES) * (tm // CLASSES) + tok // CLASSES] = 1.0
    return pl.pallas_call(
        functools.partial(_post_mix_kernel, tm=tm),
        grid=(N // tm,),
        in_specs=[row(D), row(ATT_WIDTH), cls(ATT_WIDTH), cls(ATT_WIDTH), row(LANES),
                  cls(LANES), cls(LANES), row(M_WIDTH), const((D, D)), const((1, D)),
                  const((D, X_WIDTH)), memspec, memspec, const((X_WIDTH, D)), const((1, D)),
                  const((D, 2 * LANES)), const((1, LANES)), const((tm, tm)), const((tm, tm))],
        out_specs=[row(D), row(D + ROUTE_W), row(LANES), const((8, LANES))],
        out_shape=[jax.ShapeDtypeStruct((N, D), F32), jax.ShapeDtypeStruct((N, D + ROUTE_W), F32),
                   jax.ShapeDtypeStruct((N, LANES), F32), jax.ShapeDtypeStruct((8, LANES), F32)],
        scratch_shapes=[pltpu.VMEM((8, LANES), F32), pltpu.VMEM((tm, LANES), F32),
                        pltpu.VMEM((tm, LANES), F32), pltpu.VMEM((tm, ATT_WIDTH), BF16),
                        pltpu.VMEM((tm, X_WIDTH), BF16), pltpu.VMEM((tm, D), BF16),
                        pltpu.VMEM((tm, D), BF16)],
        compiler_params=_cparams(("arbitrary",)),
        name="post_mix",
    )(x2d, outs[0].reshape(N, ATT_WIDTH), outs[1], outs[2], lses[0].reshape(N, LANES), lses[1], lses[2],
      y_m, w_out.astype(BF16), g_cross.reshape(1, D), (w_q_x * (X_HEAD_DIM ** -0.5)).astype(BF16),
      k_mem, v_mem, w_o_x.astype(BF16), g_ffn.reshape(1, D), wr_cat, br,
      jnp.asarray(np.tril(np.ones((tm, tm), np.float32), -1), BF16),
      jnp.asarray(unperm, BF16))


def _tile_row(ref, row):
    return ref.at[row >> 3, pl.ds(row & (SUBLANES - 1), 1), :]


def _dispatch_kernel(pad_lo_ref, pad_hi_ref, nvalid_ref, dest_ref, h_ref, xs_ref, zero_scr, sem, zsem,
                     *, tm, nblk):
    def copy(i, u):
        return pltpu.make_async_copy(h_ref.at[i, pl.ds(u, 1), :],
                                     _tile_row(xs_ref, dest_ref[0, 0, i * SUBLANES + u]), sem)

    def start(i, carry):
        for u in range(SUBLANES):
            copy(i, u).start()
        return carry

    lax.fori_loop(0, tm // SUBLANES, start, 0)

    @pl.when(pl.program_id(0) == 0)
    def _():
        zero_scr[...] = jnp.zeros_like(zero_scr)
        groups = MOE_ROWS // SUBLANES

        def pad_copy(r):
            return pltpu.make_async_copy(zero_scr.at[0, pl.ds(0, 1), :], _tile_row(xs_ref, r), zsem)

        def tail_copy(blk):
            return pltpu.make_async_copy(zero_scr, xs_ref.at[pl.ds(blk * groups, groups), :, :], zsem)

        for q in range(N_BUCKETS):
            lax.fori_loop(pad_lo_ref[q], pad_hi_ref[q], lambda r, c: (pad_copy(r).start(), c)[1], 0)
        lax.fori_loop(nvalid_ref[0], nblk, lambda blk, c: (tail_copy(blk).start(), c)[1], 0)
        for q in range(N_BUCKETS):
            lax.fori_loop(pad_lo_ref[q], pad_hi_ref[q], lambda r, c: (pad_copy(r).wait(), c)[1], 0)
        lax.fori_loop(nvalid_ref[0], nblk, lambda blk, c: (tail_copy(blk).wait(), c)[1], 0)

    pltpu.make_async_copy(h_ref, xs_ref.at[pl.ds(0, tm // SUBLANES), :, :], sem).wait()


def _dispatch(hx, dest, pad_lo, pad_hi, nvalid, n_rows, tm):
    N, W = hx.shape
    grid_spec = pltpu.PrefetchScalarGridSpec(
        num_scalar_prefetch=3,
        grid=(N // tm,),
        in_specs=[pl.BlockSpec((1, 1, tm), lambda i, lo, hi, nv: (i, 0, 0), memory_space=pltpu.SMEM),
                  pl.BlockSpec((tm // SUBLANES, SUBLANES, W), lambda i, lo, hi, nv: (i, 0, 0))],
        out_specs=pl.BlockSpec(memory_space=pl.ANY),
        scratch_shapes=[pltpu.VMEM((MOE_ROWS // SUBLANES, SUBLANES, W), F32), pltpu.SemaphoreType.DMA(()),
                        pltpu.SemaphoreType.DMA(())],
    )
    return pl.pallas_call(
        functools.partial(_dispatch_kernel, tm=tm, nblk=n_rows // MOE_ROWS),
        grid_spec=grid_spec,
        out_shape=jax.ShapeDtypeStruct((n_rows // SUBLANES, SUBLANES, W), F32),
        compiler_params=_cparams(("arbitrary",)),
        name="moe_dispatch",
    )(pad_lo, pad_hi, nvalid, dest.reshape(N // tm, 1, tm),
      hx.reshape(N // SUBLANES, SUBLANES, W)).reshape(n_rows, W)


def _expert_kernel(blk_a_ref, blk_b_ref, nvalid_ref, x_ref, w1a_ref, w3a_ref, w2a_ref, w1b_ref, w3b_ref,
                   w2b_ref, y_ref):
    del blk_a_ref, blk_b_ref
    D = y_ref.shape[1]

    @pl.when(pl.program_id(0) < nvalid_ref[0])
    def _():
        xb = x_ref[:, :D].astype(BF16)
        wa = x_ref[:, D + 2:D + 3]
        wb = x_ref[:, D + 3:D + 4]
        up = [_dot(xb, w_ref[0]) for w_ref in (w1a_ref, w3a_ref, w1b_ref, w3b_ref)]
        hid = [(up[2 * e] * _sigmoid(up[2 * e]) * up[2 * e + 1]).astype(BF16) for e in range(2)]
        y_ref[...] = wa * _dot(hid[0], w2a_ref[0]) + wb * _dot(hid[1], w2b_ref[0])

    @pl.when(pl.program_id(0) >= nvalid_ref[0])
    def _():
        y_ref[...] = jnp.zeros_like(y_ref)


def _experts(xs, blk_a, blk_b, nvalid, w1, w3, w2):
    P, W = xs.shape
    D = W - ROUTE_W
    nblk = P // MOE_ROWS
    F = w1.shape[-1]
    up_a = pl.BlockSpec((1, D, F), lambda i, ba, bb, nv: (ba[i], 0, 0))
    up_b = pl.BlockSpec((1, D, F), lambda i, ba, bb, nv: (bb[i], 0, 0))
    grid_spec = pltpu.PrefetchScalarGridSpec(
        num_scalar_prefetch=3,
        grid=(nblk,),
        in_specs=[pl.BlockSpec((MOE_ROWS, W),
                               lambda i, ba, bb, nv: (jnp.maximum(jnp.minimum(i, nv[0] - 1), 0), 0)),
                  up_a, up_a, pl.BlockSpec((1, F, D), lambda i, ba, bb, nv: (ba[i], 0, 0)),
                  up_b, up_b, pl.BlockSpec((1, F, D), lambda i, ba, bb, nv: (bb[i], 0, 0))],
        out_specs=pl.BlockSpec((MOE_ROWS, D), lambda i, ba, bb, nv: (i, 0)),
    )
    w1b, w3b, w2b = w1.astype(BF16), w3.astype(BF16), w2.astype(BF16)
    return pl.pallas_call(
        _expert_kernel,
        grid_spec=grid_spec,
        out_shape=jax.ShapeDtypeStruct((P, D), F32),
        compiler_params=_cparams(("arbitrary",)),
        name="moe_experts",
    )(blk_a, blk_b, nvalid, xs, w1b, w3b, w2b, w1b, w3b, w2b)


def _combine_kernel(dest_ref, dest_next_ref, x_ref, g_ref, ys_ref, o_ref, buf, sem, *, tm, nsteps):
    step = pl.program_id(0)
    slot = step % 2
    groups = tm // SUBLANES

    def request(d_ref, s):
        def body(i, carry):
            for u in range(SUBLANES):
                pltpu.make_async_copy(_tile_row(ys_ref, d_ref[0, 0, i * SUBLANES + u]),
                                      buf.at[s, i, pl.ds(u, 1), :], sem.at[s]).start()
            return carry
        lax.fori_loop(0, groups, body, 0)

    @pl.when(step == 0)
    def _():
        request(dest_ref, 0)

    @pl.when(step + 1 < nsteps)
    def _():
        request(dest_next_ref, 1 - slot)

    pltpu.make_async_copy(ys_ref.at[pl.ds(0, groups), :, :], buf.at[slot], sem.at[slot]).wait()
    o_ref[...] = _rms(x_ref[...] + buf[slot].reshape(o_ref.shape), g_ref[...])


def _combine(x2, dest, ys, g_final, tm):
    N, D = x2.shape
    nsteps = N // tm
    dest3 = dest.reshape(nsteps, 1, tm)
    return pl.pallas_call(
        functools.partial(_combine_kernel, tm=tm, nsteps=nsteps),
        grid=(nsteps,),
        in_specs=[pl.BlockSpec((1, 1, tm), lambda i: (i, 0, 0), memory_space=pltpu.SMEM),
                  pl.BlockSpec((1, 1, tm), lambda i: (jnp.minimum(i + 1, nsteps - 1), 0, 0),
                               memory_space=pltpu.SMEM),
                  pl.BlockSpec((tm, D), lambda i: (i, 0)),
                  pl.BlockSpec((1, D), lambda i: (0, 0)),
                  pl.BlockSpec(memory_space=pl.ANY)],
        out_specs=pl.BlockSpec((tm, D), lambda i: (i, 0)),
        out_shape=jax.ShapeDtypeStruct((N, D), F32),
        scratch_shapes=[pltpu.VMEM((2, tm // SUBLANES, SUBLANES, D), F32),
                        pltpu.SemaphoreType.DMA((2,))],
        compiler_params=_cparams(("arbitrary",)),
        name="moe_combine",
    )(dest3, dest3, x2, g_final.reshape(1, D), ys.reshape(ys.shape[0] // SUBLANES, SUBLANES, D))


def kernel(x, mem, positions, g_mix, w_in, conv_w, conv_b, w_q_m, w_k_m, b_i, b_f, g_mhn, skip_m, w_out, g_cross, g_mem, w_q_x, w_kv_x, w_o_x, g_ffn, w_router_g, b_router_g, w_router_e, b_router_e, w1, w3, w2, g_final):
    B, S, D = x.shape
    N = B * S
    depth = g_mix.shape[0]
    tm_in = 512
    tm_post = 512
    tm_dispatch = 2048
    tm_combine = 1024
    assert all(window // d == ATT_BLOCK and CLASSES % d == 0 for window, d in DILATED_CONFIGS)
    assert B % M_BATCH == 0 and S % tm_in == 0 and S % tm_post == 0
    assert N % tm_dispatch == 0 and N % tm_combine == 0
    assert depth == 1
    for l in range(depth):
        x2d = x.reshape(N, D)
        pos = positions.astype(F32).reshape(N, 1)
        k_mem, v_mem = _mem_kv(mem, g_mem[l], w_kv_x[l])
        q, k, v, qc, kc, vc, mu, mvt, mo, gtt = _in_proj(
            x2d, pos, g_mix[l], w_in[l], b_i[l], b_f[l], tm_in, B)
        q, k, v = (t.reshape(B, 1, S, ATT_WIDTH) for t in (q, k, v))
        outs, lses = zip(*(_attention_config(*(qkv + (d,)))
                           for qkv, (_, d) in zip(((q, k, v), (qc, kc, vc), (qc, kc, vc)), DILATED_CONFIGS)))
        y_m = _mlstm(mu.reshape(B, S, M_WIDTH), mvt, mo.reshape(B, S, M_WIDTH), gtt, conv_w[l], conv_b[l],
                     w_q_m[l], w_k_m[l], g_mhn[l], skip_m[l]).reshape(N, M_WIDTH)
        x2, hx, route, cnt = _post_mix(x2d, outs, lses, y_m, w_out[l], g_cross[l], w_q_x[l], k_mem,
                                       v_mem, w_o_x[l], g_ffn[l], w_router_g[l], b_router_g[l],
                                       w_router_e[l], b_router_e[l], tm_post, B)

        rank = route[:, 4].astype(jnp.int32)
        bucket = route[:, 5].astype(jnp.int32)
        counts = cnt[0, :N_BUCKETS].astype(jnp.int32)
        padded = ((counts + MOE_ROWS - 1) // MOE_ROWS) * MOE_ROWS
        pends = jnp.cumsum(padded)
        pstarts = pends - padded
        onehot = bucket[:, None] == jnp.arange(N_BUCKETS, dtype=jnp.int32)
        dest = jnp.sum(jnp.where(onehot, pstarts, 0), axis=-1) + rank
        n_rows = N + N_BUCKETS * MOE_ROWS
        nblk = n_rows // MOE_ROWS
        blk_start = jnp.arange(nblk, dtype=jnp.int32) * MOE_ROWS
        blk_bucket = jnp.minimum(jnp.sum(pends[None, :] <= blk_start[:, None], axis=1), N_BUCKETS - 1)
        base = (np.arange(N_BUCKETS) // len(PAIRS)) * EXPERTS_PER_GROUP
        expert_a = jnp.asarray(base + np.array([p[0] for p in PAIRS] * N_GROUPS), jnp.int32)
        expert_b = jnp.asarray(base + np.array([p[1] for p in PAIRS] * N_GROUPS), jnp.int32)
        blk_a = jnp.take(expert_a, blk_bucket).astype(jnp.int32)
        blk_b = jnp.take(expert_b, blk_bucket).astype(jnp.int32)
        nvalid = (pends[-1] // MOE_ROWS).reshape(1).astype(jnp.int32)

        xs = _dispatch(hx, dest, (pstarts + counts).astype(jnp.int32), pends.astype(jnp.int32),
                       nvalid, n_rows, tm_dispatch)
        ys = _experts(xs, blk_a, blk_b, nvalid, w1[l], w3[l], w2[l])
        x = _combine(x2, dest, ys, g_final, tm_combine).reshape(B, S, D)
    return x
```

```python
import functools

import jax
import jax.numpy as jnp
import numpy as np
from jax import lax
from jax.experimental import pallas as pl
from jax.experimental.pallas import tpu as pltpu

F32 = jnp.float32
BF16 = jnp.bfloat16

EPS = 1e-6
LANES = 128
SUBLANES = 8
ATT_HEAD_DIM = 64
ATT_WIDTH = 512
DILATED_CONFIGS = ((128, 1), (512, 4), (2048, 16))
CLASSES = 16
ATT_BLOCK = 128
ATT_GROUP = 1
ROPE_THETA = 500000.0
ROPE_DIM = ATT_HEAD_DIM // 4
M_WIDTH = 512
M_HEADS = 4
M_HEAD_DIM = 128
CONV_WIDTH = 4
M_CHUNK = 128
M_BATCH = 4
X_HEADS = 4
X_HEAD_DIM = 64
X_WIDTH = X_HEADS * X_HEAD_DIM
N_GROUPS = 4
EXPERTS_PER_GROUP = 4
N_EXPERTS = 16
TOP_K = 2
EXPERT_FF = 512
PAIRS = tuple((a, b) for a in range(EXPERTS_PER_GROUP) for b in range(a + 1, EXPERTS_PER_GROUP))
N_BUCKETS = N_GROUPS * len(PAIRS)
MOE_ROWS = 512
ROUTE_W = LANES
POST_SPLIT = 1
POST_ROWS = 64
VMEM_LIMIT = 56 * 1024 * 1024

NEG_INF = float("-inf")


def _cparams(sem):
    return pltpu.CompilerParams(dimension_semantics=sem, vmem_limit_bytes=VMEM_LIMIT)


def _rms(x, g):
    return x * lax.rsqrt(jnp.mean(x * x, axis=-1, keepdims=True) + EPS) * g


def _dot(a, b):
    return jnp.dot(a, b, preferred_element_type=F32)


def _dot_nt(a, b):
    return lax.dot_general(a, b, (((1,), (1,)), ((), ())), preferred_element_type=F32)


def _dot_tn(a, b):
    return lax.dot_general(a, b, (((0,), (0,)), ((), ())), preferred_element_type=F32)


def _hi_lo(a):
    hi = a.astype(BF16)
    return hi, (a - hi.astype(F32)).astype(BF16)


def _log_sigmoid(x):
    return jnp.minimum(x, 0.0) - jnp.log(1.0 + jnp.exp(-jnp.abs(x)))


def _sigmoid(x):
    return 0.5 * jnp.tanh(0.5 * x) + 0.5


def _mem_kv_kernel(mem_ref, g_ref, w_ref, k_ref, v_ref):
    h = _rms(mem_ref[0], g_ref[...]).astype(BF16)
    kv = _dot(h, w_ref[...])
    k_ref[0] = kv[:, :X_WIDTH].astype(BF16)
    v_ref[0] = kv[:, X_WIDTH:].astype(BF16)


def _mem_kv(mem, g_mem, w_kv):
    B, M, D = mem.shape
    return pl.pallas_call(
        _mem_kv_kernel,
        grid=(B,),
        in_specs=[pl.BlockSpec((1, M, D), lambda b: (b, 0, 0)),
                  pl.BlockSpec((1, D), lambda b: (0, 0)),
                  pl.BlockSpec((D, 2 * X_WIDTH), lambda b: (0, 0))],
        out_specs=[pl.BlockSpec((1, M, X_WIDTH), lambda b: (b, 0, 0)),
                   pl.BlockSpec((1, M, X_WIDTH), lambda b: (b, 0, 0))],
        out_shape=[jax.ShapeDtypeStruct((B, M, X_WIDTH), BF16)] * 2,
        compiler_params=_cparams(("arbitrary",)),
        name="mem_kv",
    )(mem, g_mem.reshape(1, D), w_kv.astype(BF16))


def _in_proj_kernel(x_ref, pos_ref, g_ref, wqkv_ref, wm_ref, wmvt_ref, bgt_ref,
                    invf_ref, sgn_ref, spread_ref, perm_ref,
                    q_ref, k_ref, v_ref, qc_ref, kc_ref, vc_ref,
                    mu_ref, mvt_ref, mo_ref, gtt_ref, *, tm):
    hb = _rms(x_ref[...], g_ref[...]).astype(BF16)
    qkv = _dot(hb, wqkv_ref[...])
    slots = tm // pos_ref.shape[0]
    ang = pos_ref[...] * invf_ref[...]
    rows = pos_ref.shape[0]
    cs_hl = jnp.concatenate(_hi_lo(jnp.where(sgn_ref[0:1, :] > 0.0, jnp.sin(ang), jnp.cos(ang))), axis=0)
    spread = [_dot(cs_hl, spread_ref[i]) for i in range(slots)]
    spread = [sp[:rows] + sp[rows:] for sp in spread]
    cos = jnp.concatenate([sp[:, :LANES] for sp in spread], axis=0) + sgn_ref[1:2, :]
    sin = jnp.concatenate([sp[:, LANES:] for sp in spread], axis=0)
    half = ROPE_DIM // 2
    first_half = sgn_ref[2:3, :] > 0.0

    def emit_classes(nat_ref, cls_ref):
        rows = tm // CLASSES
        by_class = _dot(perm_ref[...], nat_ref[...])
        for r in range(CLASSES):
            cls_ref[0, r] = by_class[r * rows:(r + 1) * rows].astype(BF16)

    v_ref[...] = qkv[:, 2 * ATT_WIDTH:].astype(BF16)
    mm = _dot(hb, wm_ref[...])
    for which, refs in ((0, (q_ref, qc_ref)), (1, (k_ref, kc_ref))):
        for g in range(ATT_WIDTH // LANES):
            cols = slice(g * LANES, (g + 1) * LANES)
            t = qkv[:, which * ATT_WIDTH + g * LANES: which * ATT_WIDTH + (g + 1) * LANES]
            partner = jnp.where(first_half, pltpu.roll(t, LANES - half, 1), pltpu.roll(t, half, 1))
            refs[0][:, cols] = (t * cos + partner * sin).astype(BF16)
    mu_ref[...] = mm[:, :M_WIDTH].astype(BF16)
    mo_ref[...] = mm[:, M_WIDTH:].astype(BF16)
    mvt = _dot_nt(wmvt_ref[...], hb)
    emit_classes(v_ref, vc_ref)
    for c in range(tm // M_CHUNK):
        mvt_ref[c] = mvt[:M_WIDTH, c * M_CHUNK:(c + 1) * M_CHUNK].astype(BF16)
    emit_classes(q_ref, qc_ref)
    gtt_ref[0] = mvt[M_WIDTH:M_WIDTH + 2 * M_HEADS, :] + bgt_ref[...]
    emit_classes(k_ref, kc_ref)


def _in_proj(x2d, pos, g_mix, w_in, b_i, b_f, tm, B):
    N, D = x2d.shape
    S = N // B
    tps = S // tm
    A = ATT_WIDTH
    wq = w_in[:, :A] * (ATT_HEAD_DIM ** -0.5)
    wqkv = jnp.concatenate([wq, w_in[:, A:3 * A]], axis=1).astype(BF16)
    o = 3 * A
    wm = jnp.concatenate([w_in[:, o:o + M_WIDTH], w_in[:, o + 2 * M_WIDTH:o + 3 * M_WIDTH]],
                         axis=1).astype(BF16)
    wgates = w_in[:, 3 * A + 3 * M_WIDTH:]
    wmvt = jnp.pad(jnp.concatenate([w_in[:, o + M_WIDTH:o + 2 * M_WIDTH], wgates], axis=1).T,
                   ((0, 2 * SUBLANES - 2 * M_HEADS), (0, 0))).astype(BF16)
    bgt = jnp.concatenate([b_i, b_f]).astype(F32).reshape(2 * M_HEADS, 1)
    half = ROPE_DIM // 2
    slots = LANES // ROPE_DIM
    lane = np.arange(LANES)
    c = lane % ROPE_DIM
    j = lane % ATT_HEAD_DIM
    inv_freq = ROPE_THETA ** (-jnp.arange(0, ROPE_DIM, 2, dtype=F32) / ROPE_DIM)
    invf = inv_freq[c % half].reshape(1, LANES).astype(F32)
    sgn = np.zeros((8, LANES), np.float32)
    sgn[0] = c >= half
    sgn[1] = j >= ROPE_DIM
    sgn[2] = j < half
    spread = np.zeros((slots, LANES, 2 * LANES), np.float32)
    for i in range(slots):
        for f in range(half):
            spread[i, ROPE_DIM * i + f, lane[(j < ROPE_DIM) & (j % half == f)]] = 1.0
            spread[i, ROPE_DIM * i + half + f, LANES + lane[j == f]] = -1.0
            spread[i, ROPE_DIM * i + half + f, LANES + lane[j == half + f]] = 1.0
    posp = pos.reshape(N // tm, slots, tm // slots).transpose(0, 2, 1)
    posp = jnp.repeat(posp, ROPE_DIM, axis=2).reshape(N // slots, LANES)
    tok = np.arange(tm)
    perm = np.zeros((tm, tm), np.float32)
    perm[(tok % CLASSES) * (tm // CLASSES) + tok // CLASSES, tok] = 1.0
    const = lambda shape: pl.BlockSpec(shape, lambda i: (0,) * len(shape))
    row = lambda w: pl.BlockSpec((tm, w), lambda i: (i, 0))
    cls = pl.BlockSpec((1, CLASSES, tm // CLASSES, A), lambda i: (i // tps, 0, i % tps, 0))
    cls_shape = jax.ShapeDtypeStruct((B, CLASSES, S // CLASSES, A), BF16)
    return pl.pallas_call(
        functools.partial(_in_proj_kernel, tm=tm),
        grid=(N // tm,),
        in_specs=[row(D), pl.BlockSpec((tm // slots, LANES), lambda i: (i, 0)), const((1, D)),
                  const((D, 3 * A)), const((D, 2 * M_WIDTH)),
                  const((M_WIDTH + 2 * SUBLANES, D)), const((2 * M_HEADS, 1)),
                  const((1, LANES)), const((8, LANES)), const((slots, LANES, 2 * LANES)),
                  const((tm, tm))],
        out_specs=[row(A), row(A), row(A), cls, cls, cls, row(M_WIDTH),
                   pl.BlockSpec((tm // M_CHUNK, M_WIDTH, M_CHUNK), lambda i: (i, 0, 0)), row(M_WIDTH),
                   pl.BlockSpec((1, 2 * M_HEADS, tm), lambda i: (i // tps, 0, i % tps))],
        out_shape=[jax.ShapeDtypeStruct((N, A), BF16)] * 3 + [cls_shape] * 3
        + [jax.ShapeDtypeStruct((N, M_WIDTH), BF16),
           jax.ShapeDtypeStruct((N // M_CHUNK, M_WIDTH, M_CHUNK), BF16),
           jax.ShapeDtypeStruct((N, M_WIDTH), BF16),
           jax.ShapeDtypeStruct((B, 2 * M_HEADS, S), F32)],
        compiler_params=_cparams(("arbitrary",)),
        name="in_proj",
    )(x2d, posp, g_mix.reshape(1, D), wqkv, wm, wmvt, bgt, invf, jnp.asarray(sgn),
      jnp.asarray(spread, BF16), jnp.asarray(perm, BF16))


def _attn_kernel(q_ref, kc_ref, kp_ref, vc_ref, vp_ref, o_ref, l_ref, kbuf, vbuf, bias_scr, *, qb, nc):
    blk = ATT_BLOCK
    piece = blk // nc
    nsub = qb // blk
    first = pl.program_id(2) == 0
    pair_low = (lax.broadcasted_iota(jnp.int32, (1, ATT_WIDTH), 1) & (LANES - 1)) < ATT_HEAD_DIM

    def put_v(dst, v):
        zero = jnp.zeros_like(v)
        vbuf[0, dst, :] = jnp.where(pair_low, v, zero)
        vbuf[1, dst, :] = jnp.where(pair_low, zero, v)

    for c in range(nc):
        kbuf[c * piece:(c + 1) * piece, :] = kp_ref[c]
        put_v(slice(c * piece, (c + 1) * piece), vp_ref[c])
        for sub in range(nsub):
            dst = slice((sub + 1) * blk + c * piece, (sub + 1) * blk + (c + 1) * piece)
            src = slice(sub * piece, (sub + 1) * piece)
            kbuf[dst, :] = kc_ref[c, src, :]
            put_v(dst, vc_ref[c, src, :])

    def pos(p):
        p = p & (blk - 1)
        return nc * (p & (piece - 1)) + (p >> (piece.bit_length() - 1))

    qi = lax.broadcasted_iota(jnp.int32, (blk, 2 * blk), 0)
    ki = lax.broadcasted_iota(jnp.int32, (blk, 2 * blk), 1)
    dist = pos(qi) - pos(ki) + jnp.where(ki < blk, blk, 0)
    band = (dist >= 0) & (dist <= blk)
    band_first = band & ((ki >= blk) | jnp.logical_not(first))
    bias_scr[0] = jnp.where(band_first, 0.0, NEG_INF)
    bias_scr[1] = jnp.where(band, 0.0, NEG_INF)
    lane = lax.broadcasted_iota(jnp.int32, (1, LANES), 1)
    lane_full = lax.broadcasted_iota(jnp.int32, (blk, LANES), 1)
    low = lane < ATT_HEAD_DIM
    nh = 2 * (ATT_WIDTH // LANES)
    for sub0 in range(0, nsub, ATT_GROUP):
        subs = range(sub0, min(sub0 + ATT_GROUP, nsub))
        chains = [(sub, h) for sub in subs for h in range(nh)]
        ids = range(len(chains))
        col = [slice((h // 2) * LANES, (h // 2 + 1) * LANES) for _, h in chains]
        prow = [slice(sub * piece, (sub + 1) * piece) for sub, _ in chains]
        krow = [slice(sub * blk, (sub + 2) * blk) for sub, _ in chains]
        qs = [jnp.concatenate([q_ref[c, prow[i], col[i]] for c in range(nc)], axis=0) for i in ids]
        qs = [jnp.where(low if h % 2 == 0 else jnp.logical_not(low), qs[i], jnp.zeros_like(qs[i]))
              for i, (_, h) in enumerate(chains)]
        s = [_dot_nt(qs[i], kbuf[krow[i], col[i]]) + bias_scr[min(sub, 1)]
             for i, (sub, _) in enumerate(chains)]
        m = [jnp.max(s[i], axis=-1, keepdims=True) for i in ids]
        p = [jnp.exp(s[i] - m[i]) for i in ids]
        l = [jnp.sum(p[i], axis=-1, keepdims=True) for i in ids]
        pv = [_dot(p[i].astype(BF16), vbuf[h % 2, krow[i], col[i]]) * (1.0 / l[i])
              for i, (_, h) in enumerate(chains)]
        for n, sub in enumerate(subs):
            base = n * nh
            m_all = jnp.zeros((blk, LANES), F32)
            l_all = jnp.ones((blk, LANES), F32)
            for h in range(nh):
                m_all = jnp.where(lane_full == h, m[base + h], m_all)
                l_all = jnp.where(lane_full == h, l[base + h], l_all)
            for g in range(ATT_WIDTH // LANES):
                acc = (pv[base + 2 * g] + pv[base + 2 * g + 1]).astype(BF16)
                for c in range(nc):
                    o_ref[c, prow[base], col[base + 2 * g]] = acc[c * piece:(c + 1) * piece]
            lse_all = m_all + jnp.log(l_all)
            for c in range(nc):
                l_ref[c, prow[base], :] = lse_all[c * piece:(c + 1) * piece]


def _attention_config(q, k, v, d):
    B, C, L, W = q.shape
    nc = C // d
    qb = min(512, L * nc)
    rows = qb // nc
    piece = ATT_BLOCK // nc
    nsub = qb // ATT_BLOCK
    view = lambda t: t.reshape(B, nc, d, L, t.shape[-1])
    cur = lambda w: pl.BlockSpec((None, nc, None, rows, w), lambda b, r, j: (b, 0, r, j, 0))
    prev = pl.BlockSpec((None, nc, None, piece, W),
                        lambda b, r, j: (b, 0, r, jnp.maximum(j * nsub - 1, 0), 0))
    o, lse = pl.pallas_call(
        functools.partial(_attn_kernel, qb=qb, nc=nc),
        grid=(B, d, L // rows),
        in_specs=[cur(W), cur(W), prev, cur(W), prev],
        out_specs=[cur(W), cur(LANES)],
        out_shape=[jax.ShapeDtypeStruct((B, nc, d, L, W), BF16),
                   jax.ShapeDtypeStruct((B, nc, d, L, LANES), F32)],
        scratch_shapes=[pltpu.VMEM((qb + ATT_BLOCK, W), BF16), pltpu.VMEM((2, qb + ATT_BLOCK, W), BF16),
                        pltpu.VMEM((2, ATT_BLOCK, 2 * ATT_BLOCK), F32)],
        compiler_params=_cparams(("arbitrary", "arbitrary", "arbitrary")),
        name=f"attention_d{d}",
    )(view(q), view(k), view(k), view(v), view(v))
    return o.reshape(B, C, L, W), lse.reshape(B, C, L, LANES)


def _mlstm_kernel(mu_ref, mvt_ref, mo_ref, gtt_ref, cw_ref, cb_ref, wqk_ref, gn_ref,
                  sk_ref, y_ref, c_scr, n_scr, m_scr, ext_scr):
    L = M_CHUNK
    H = M_HEADS
    HD = M_HEAD_DIM

    @pl.when(pl.program_id(1) == 0)
    def _():
        c_scr[...] = jnp.zeros_like(c_scr)
        n_scr[...] = jnp.zeros_like(n_scr)
        m_scr[...] = jnp.zeros_like(m_scr)
        for bb in range(M_BATCH):
            ext_scr[bb, 0:8, :] = jnp.zeros((8, M_WIDTH), F32)

    ri = lax.broadcasted_iota(jnp.int32, (L, L), 0)
    ci = lax.broadcasted_iota(jnp.int32, (L, L), 1)
    causal_t = ri <= ci
    triu = jnp.where(causal_t, 1.0, 0.0).astype(BF16)
    scale = HD ** -0.5

    bbs = range(M_BATCH)
    cs, gtt, b_rows, key_cols = [], [], [], []
    for bb in bbs:
        mu = mu_ref[bb].astype(F32)
        ext_scr[bb, 8:8 + L, :] = mu
        conv = cb_ref[...]
        for jj in range(CONV_WIDTH):
            lo = 8 - (CONV_WIDTH - 1) + jj
            conv = conv + ext_scr[bb, lo:lo + L, :] * cw_ref[jj:jj + 1, :]
        ext_scr[bb, 0:8, :] = mu[L - 8:, :]
        cs.append(conv * _sigmoid(conv))
    cb16 = [cs[bb].astype(BF16) for bb in bbs]
    for bb in bbs:
        gtt.append(gtt_ref[bb])
        hi_r, lo_r = _hi_lo(_log_sigmoid(gtt[bb]))
        b_rows.append(_dot(hi_r, triu) + _dot(lo_r, triu))
        key_rows = gtt[bb] - pltpu.roll(b_rows[bb], H, 0)
        key_cols.append(jnp.concatenate([key_rows, jnp.zeros((L - 2 * H, L), F32)], axis=0).T)

    prs = [(bb, hd) for bb in bbs for hd in range(H)]
    ids = range(len(prs))
    col = [slice(hd * HD, (hd + 1) * HD) for _, hd in prs]
    qk = [_dot(cb16[bb][:, col[i]], wqk_ref[hd]) for i, (bb, hd) in enumerate(prs)]
    qb = [qk[i][:, :HD].astype(BF16) for i in ids]
    kb = [(qk[i][:, HD:] * scale).astype(BF16) for i in ids]
    vt = [mvt_ref[bb, col[i], :] for i, (bb, _) in enumerate(prs)]
    b_r = [b_rows[bb][H + hd:H + hd + 1, :] for bb, hd in prs]
    b_last = [b_r[i][:, L - 1:L] for i in ids]
    m_prev = [m_scr[i:i + 1, 0:1] for i in ids]
    ct_prev = [c_scr[i] for i in ids]
    n_prev = [n_scr[i] for i in ids]

    dlog = [jnp.where(causal_t, b_r[i] + key_cols[bb][:, hd:hd + 1], NEG_INF)
            for i, (bb, hd) in enumerate(prs)]
    m_inter = [b_r[i] + m_prev[i] for i in ids]
    m_t = [jnp.maximum(m_inter[i], jnp.max(dlog[i], axis=0, keepdims=True)) for i in ids]
    inter_w = [jnp.exp(m_inter[i] - m_t[i]) for i in ids]
    st_w = [_dot_nt(kb[i], qb[i]) * jnp.exp(dlog[i] - m_t[i]) for i in ids]
    num = [_dot(vt[i], st_w[i].astype(BF16)) + inter_w[i] * _dot_nt(ct_prev[i].astype(BF16), qb[i])
           for i in ids]
    den = [jnp.sum(st_w[i], axis=0, keepdims=True)
           + inter_w[i] * _dot_nt(n_prev[i].astype(BF16), qb[i])[0:1, :] for i in ids]
    ht = [num[i] * (1.0 / jnp.maximum(jnp.abs(den[i]), jnp.exp(-m_t[i]))) for i in ids]
    ht = [ht[i] * lax.rsqrt(jnp.mean(ht[i] * ht[i], axis=0, keepdims=True) + EPS) for i in ids]
    for i, (bb, _) in enumerate(prs):
        hn = ht[i].T * gn_ref[:, col[i]]
        y = _sigmoid(mo_ref[bb, :, col[i]].astype(F32)) * (hn + sk_ref[:, col[i]] * cs[bb][:, col[i]])
        y_ref[bb, :, col[i]] = y.astype(BF16)

    g_r = [b_last[i] - b_r[i] + gtt[bb][hd:hd + 1, :] for i, (bb, hd) in enumerate(prs)]
    m_loc = [jnp.max(g_r[i], axis=1, keepdims=True) for i in ids]
    wk_r = [jnp.exp(g_r[i] - m_loc[i]) for i in ids]
    c_loc = [_dot((vt[i].astype(F32) * wk_r[i]).astype(BF16), kb[i]) for i in ids]
    n_loc = [_dot(jnp.broadcast_to(wk_r[i], (8, L)).astype(BF16), kb[i]) for i in ids]
    for i in ids:
        m_new = jnp.maximum(b_last[i] + m_prev[i], m_loc[i])
        a = jnp.exp(b_last[i] + m_prev[i] - m_new)
        cc = jnp.exp(m_loc[i] - m_new)
        c_scr[i] = a * ct_prev[i] + cc * c_loc[i]
        n_scr[i] = a * n_prev[i] + cc * n_loc[i]
        m_scr[i:i + 1, :] = jnp.broadcast_to(m_new, (1, LANES))


def _mlstm(mu, mvt, mo, gtt, conv_w, conv_b, w_q_m, w_k_m, g_mhn, skip_m):
    B, S, W = mu.shape
    L = M_CHUNK
    nc = S // L
    nb = M_BATCH
    tok = pl.BlockSpec((nb, L, W), lambda b, c: (b, c, 0))
    const = lambda shape: pl.BlockSpec(shape, lambda b, c: (0,) * len(shape))
    wqk = jnp.concatenate([w_q_m, w_k_m], axis=-1).astype(BF16)
    return pl.pallas_call(
        _mlstm_kernel,
        grid=(B // nb, nc),
        in_specs=[tok, pl.BlockSpec((nb, None, W, L), lambda b, c: (b, c, 0, 0)), tok,
                  pl.BlockSpec((nb, 2 * M_HEADS, L), lambda b, c: (b, 0, c)),
                  const((CONV_WIDTH, W)), const((1, W)),
                  const((M_HEADS, M_HEAD_DIM, 2 * M_HEAD_DIM)), const((1, W)), const((1, W))],
        out_specs=tok,
        out_shape=jax.ShapeDtypeStruct((B, S, W), BF16),
        scratch_shapes=[pltpu.VMEM((nb * M_HEADS, M_HEAD_DIM, M_HEAD_DIM), F32),
                        pltpu.VMEM((nb * M_HEADS, 8, M_HEAD_DIM), F32),
                        pltpu.VMEM((nb * M_HEADS, LANES), F32),
                        pltpu.VMEM((nb, 8 + L, W), F32)],
        compiler_params=_cparams(("arbitrary", "arbitrary")),
        name="mlstm",
    )(mu, mvt.reshape(B, nc, W, L), mo, gtt, conv_w.astype(F32), conv_b.reshape(1, W).astype(F32), wqk,
      g_mhn.reshape(1, W).astype(F32), skip_m.reshape(1, W).astype(F32))


def _post_mix_kernel(x_ref, o1_ref, o4_ref, o16_ref, l1_ref, l4_ref, l16_ref, ym_ref, wo_ref, gc_ref,
                     wqx_ref, km_ref, vm_ref, wox_ref, gf_ref, wr_ref, br_ref, before_ref, unperm_ref,
                     x2_ref, hx_ref, route_ref, cnt_ref,
                     run_scr, l4_scr, l16_scr, ya_scr, ox_scr, hb_scr, hl_scr, *, tm):
    @pl.when(pl.program_id(0) == 0)
    def _():
        run_scr[...] = jnp.zeros_like(run_scr)

    rows = tm // CLASSES
    o4_tok = _dot(unperm_ref[...], o4_ref[...].reshape(tm, ATT_WIDTH))
    o16_tok = _dot(unperm_ref[...], o16_ref[...].reshape(tm, ATT_WIDTH))
    for src, dst in ((l4_ref, l4_scr), (l16_ref, l16_scr)):
        for r in range(CLASSES):
            dst[pl.ds(r, rows, stride=CLASSES), :] = src[r]

    lane1 = lax.broadcasted_iota(jnp.int32, (1, LANES), 1)
    low = lane1 < ATT_HEAD_DIM

    sub_rows = tm // POST_SPLIT
    parts = range(POST_SPLIT)
    row_sl = [slice(part * sub_rows, (part + 1) * sub_rows) for part in parts]

    def row_blocks(rs):
        return [slice(r, r + POST_ROWS) for r in range(rs.start, rs.stop, POST_ROWS)]

    def mix_stage(rs):
        l1, l2, l3 = l1_ref[rs, :], l4_scr[rs, :], l16_scr[rs, :]
        mx = jnp.maximum(jnp.maximum(l1, l2), l3)
        e1, e2, e3 = jnp.exp(l1 - mx), jnp.exp(l2 - mx), jnp.exp(l3 - mx)
        inv = 1.0 / (e1 + e2 + e3)
        wts = (e1 * inv, e2 * inv, e3 * inv)
        slabs = range(ATT_WIDTH // LANES)
        acol = [slice(g * LANES, (g + 1) * LANES) for g in slabs]
        wsl = [[jnp.where(low, wts[c][:, 2 * g:2 * g + 1], wts[c][:, 2 * g + 1:2 * g + 2])
                for c in range(3)] for g in slabs]
        ya = [wsl[g][0] * o1_ref[rs, acol[g]].astype(F32) + wsl[g][1] * o4_tok[rs, acol[g]]
              + wsl[g][2] * o16_tok[rs, acol[g]] for g in slabs]
        for g in slabs:
            ya_scr[rs, acol[g]] = ya[g].astype(BF16)
        mix = _dot(ya_scr[rs, :], wo_ref[:ATT_WIDTH, :]) + _dot(ym_ref[rs, :], wo_ref[ATT_WIDTH:, :])
        for k, rb in enumerate(row_blocks(rs)):
            x1 = x_ref[rb, :] + mix[k * POST_ROWS:(k + 1) * POST_ROWS, :]
            x2_ref[rb, :] = x1
            hb_scr[rb, :] = _rms(x1, gc_ref[...]).astype(BF16)

    def cross_stage(rs):
        qx = _dot(hb_scr[rs, :], wqx_ref[...]).astype(BF16)
        hds = range(X_HEADS)
        xcol = [slice((h // 2) * LANES, (h // 2 + 1) * LANES) for h in hds]
        hmask = [low if h % 2 == 0 else jnp.logical_not(low) for h in hds]
        qs = [jnp.where(hmask[h], qx[:, xcol[h]], jnp.zeros((sub_rows, LANES), BF16)) for h in hds]
        vs = [jnp.where(hmask[h], vm_ref[0, :, xcol[h]], jnp.zeros((km_ref.shape[1], LANES), BF16))
              for h in hds]
        s = [_dot_nt(qs[h], km_ref[0, :, xcol[h]]) for h in hds]
        m = [jnp.max(s[h], axis=-1, keepdims=True) for h in hds]
        p = [jnp.exp(s[h] - m[h]) for h in hds]
        l = [jnp.sum(p[h], axis=-1, keepdims=True) for h in hds]
        pv = [_dot(p[h].astype(BF16), vs[h]) * (1.0 / l[h]) for h in hds]
        for g in range(X_WIDTH // LANES):
            ox_scr[rs, xcol[2 * g]] = (pv[2 * g] + pv[2 * g + 1]).astype(BF16)
        upd = _dot(ox_scr[rs, :], wox_ref[...])
        for k, rb in enumerate(row_blocks(rs)):
            x2 = x2_ref[rb, :] + upd[k * POST_ROWS:(k + 1) * POST_ROWS, :]
            x2_ref[rb, :] = x2
            h3 = _rms(x2, gf_ref[...])
            hx_ref[rb, :x2_ref.shape[1]] = h3
            hb_scr[rb, :], hl_scr[rb, :] = _hi_lo(h3)

    def router_stage(rs):
        t = _dot(hb_scr[rs, :], wr_ref[...])
        return t[:, :LANES] + t[:, LANES:] + _dot(hl_scr[rs, :], wr_ref[:, :LANES]) + br_ref[...]

    for p in parts:
        mix_stage(row_sl[p])
    for p in parts:
        cross_stage(row_sl[p])
    picks = [_pick_experts(router_stage(row_sl[p])) for p in parts]
    ia, ib, wa, wb, bucket = (jnp.concatenate([pk[f] for pk in picks], axis=0) for f in range(5))
    lane = lax.broadcasted_iota(jnp.int32, (tm, LANES), 1).astype(F32)

    hit = lane == bucket
    cnt = jnp.where(hit, 1.0, 0.0)
    prefix = _dot(before_ref[...], cnt.astype(BF16)) + run_scr[0:1, :]
    rank = jnp.sum(jnp.where(hit, prefix, 0.0), axis=-1, keepdims=True)
    total = run_scr[0:1, :] + jnp.sum(cnt, axis=0, keepdims=True)
    run_scr[...] = jnp.broadcast_to(total, run_scr.shape)
    cnt_ref[...] = jnp.broadcast_to(total, cnt_ref.shape)

    fields = (ia - N_GROUPS, ib - N_GROUPS, wa, wb, rank, bucket)
    route = jnp.zeros((tm, LANES), F32)
    for idx, val in enumerate(fields):
        route = jnp.where(lane == float(idx), val, route)
    route_ref[...] = route
    hx_ref[:, x2_ref.shape[1]:] = route


def _pick_experts(logits):
    lane = lax.broadcasted_iota(jnp.int32, logits.shape, 1).astype(F32)
    far = float(LANES)
    gmask = lane < N_GROUPS
    gl = jnp.where(gmask, logits, NEG_INF)
    gmax = jnp.max(gl, axis=-1, keepdims=True)
    gidx = jnp.min(jnp.where(gl == gmax, lane, far), axis=-1, keepdims=True)
    gsum = jnp.sum(jnp.where(gmask, jnp.exp(gl - gmax), 0.0), axis=-1, keepdims=True)
    g_w = 1.0 / gsum
    lo_lane = N_GROUPS + gidx * EXPERTS_PER_GROUP
    emask = (lane >= lo_lane) & (lane < lo_lane + EXPERTS_PER_GROUP)
    el = jnp.where(emask, logits, NEG_INF)
    t1 = jnp.max(el, axis=-1, keepdims=True)
    i1 = jnp.min(jnp.where(el == t1, lane, far), axis=-1, keepdims=True)
    el2 = jnp.where(lane == i1, NEG_INF, el)
    t2 = jnp.max(el2, axis=-1, keepdims=True)
    i2 = jnp.min(jnp.where(el2 == t2, lane, far), axis=-1, keepdims=True)
    ee = jnp.exp(t2 - t1)
    w1 = g_w / (1.0 + ee)
    w2 = w1 * ee

    first = i1 < i2
    ia = jnp.minimum(i1, i2)
    ib = jnp.maximum(i1, i2)
    wa = jnp.where(first, w1, w2)
    wb = jnp.where(first, w2, w1)
    la = ia - lo_lane
    lb = ib - lo_lane
    pair = la * (EXPERTS_PER_GROUP - 1) - la * (la - 1.0) * 0.5 + (lb - la - 1.0)
    return ia, ib, wa, wb, gidx * len(PAIRS) + pair


def _post_mix(x2d, outs, lses, y_m, w_out, g_cross, w_q_x, k_mem, v_mem, w_o_x, g_ffn,
              w_router_g, b_router_g, w_router_e, b_router_e, tm, B):
    N, D = x2d.shape
    S = N // B
    tps = S // tm
    M = k_mem.shape[1]
    wr = jnp.pad(jnp.concatenate([w_router_g, w_router_e], axis=1).astype(F32),
                 ((0, 0), (0, LANES - N_GROUPS - N_EXPERTS)))
    wr_hi = wr.astype(BF16)
    wr_cat = jnp.concatenate([wr_hi, (wr - wr_hi.astype(F32)).astype(BF16)], axis=1)
    br = jnp.pad(jnp.concatenate([b_router_g, b_router_e]).astype(F32),
                 (0, LANES - N_GROUPS - N_EXPERTS)).reshape(1, LANES)
    const = lambda shape: pl.BlockSpec(shape, lambda i: (0,) * len(shape))
    row = lambda w: pl.BlockSpec((tm, w), lambda i: (i, 0))
    cls = lambda w: pl.BlockSpec((None, CLASSES, tm // CLASSES, w), lambda i: (i // tps, 0, i % tps, 0))
    memspec = pl.BlockSpec((1, M, X_WIDTH), lambda i: (i // tps, 0, 0))
    tok = np.arange(tm)
    unperm = np.zeros((tm, tm), np.float32)
    unperm[tok, (tok % CLASSES) * (tm // CLASSES) + tok // CLASSES] = 1.0
    return pl.pallas_call(
        functools.partial(_post_mix_kernel, tm=tm),
        grid=(N // tm,),
        in_specs=[row(D), row(ATT_WIDTH), cls(ATT_WIDTH), cls(ATT_WIDTH), row(LANES),
                  cls(LANES), cls(LANES), row(M_WIDTH), const((D, D)), const((1, D)),
                  const((D, X_WIDTH)), memspec, memspec, const((X_WIDTH, D)), const((1, D)),
                  const((D, 2 * LANES)), const((1, LANES)), const((tm, tm)), const((tm, tm))],
        out_specs=[row(D), row(D + ROUTE_W), row(LANES), const((8, LANES))],
        out_shape=[jax.ShapeDtypeStruct((N, D), F32), jax.ShapeDtypeStruct((N, D + ROUTE_W), F32),
                   jax.ShapeDtypeStruct((N, LANES), F32), jax.ShapeDtypeStruct((8, LANES), F32)],
        scratch_shapes=[pltpu.VMEM((8, LANES), F32), pltpu.VMEM((tm, LANES), F32),
                        pltpu.VMEM((tm, LANES), F32), pltpu.VMEM((tm, ATT_WIDTH), BF16),
                        pltpu.VMEM((tm, X_WIDTH), BF16), pltpu.VMEM((tm, D), BF16),
                        pltpu.VMEM((tm, D), BF16)],
        compiler_params=_cparams(("arbitrary",)),
        name="post_mix",
    )(x2d, outs[0].reshape(N, ATT_WIDTH), outs[1], outs[2], lses[0].reshape(N, LANES), lses[1], lses[2],
      y_m, w_out.astype(BF16), g_cross.reshape(1, D), (w_q_x * (X_HEAD_DIM ** -0.5)).astype(BF16),
      k_mem, v_mem, w_o_x.astype(BF16), g_ffn.reshape(1, D), wr_cat, br,
      jnp.asarray(np.tril(np.ones((tm, tm), np.float32), -1), BF16),
      jnp.asarray(unperm, BF16))


def _tile_row(ref, row):
    return ref.at[row >> 3, pl.ds(row & (SUBLANES - 1), 1), :]


def _dispatch_kernel(pad_lo_ref, pad_hi_ref, nvalid_ref, dest_ref, h_ref, xs_ref, zero_scr, sem, zsem,
                     *, tm, nblk):
    def copy(i, u):
        return pltpu.make_async_copy(h_ref.at[i, pl.ds(u, 1), :],
                                     _tile_row(xs_ref, dest_ref[0, 0, i * SUBLANES + u]), sem)

    def start(i, carry):
        for u in range(SUBLANES):
            copy(i, u).start()
        return carry

    lax.fori_loop(0, tm // SUBLANES, start, 0)

    @pl.when(pl.program_id(0) == 0)
    def _():
        zero_scr[...] = jnp.zeros_like(zero_scr)
        groups = MOE_ROWS // SUBLANES

        def pad_copy(r):
            return pltpu.make_async_copy(zero_scr.at[0, pl.ds(0, 1), :], _tile_row(xs_ref, r), zsem)

        def tail_copy(blk):
            return pltpu.make_async_copy(zero_scr, xs_ref.at[pl.ds(blk * groups, groups), :, :], zsem)

        for q in range(N_BUCKETS):
            lax.fori_loop(pad_lo_ref[q], pad_hi_ref[q], lambda r, c: (pad_copy(r).start(), c)[1], 0)
        lax.fori_loop(nvalid_ref[0], nblk, lambda blk, c: (tail_copy(blk).start(), c)[1], 0)
        for q in range(N_BUCKETS):
            lax.fori_loop(pad_lo_ref[q], pad_hi_ref[q], lambda r, c: (pad_copy(r).wait(), c)[1], 0)
        lax.fori_loop(nvalid_ref[0], nblk, lambda blk, c: (tail_copy(blk).wait(), c)[1], 0)

    pltpu.make_async_copy(h_ref, xs_ref.at[pl.ds(0, tm // SUBLANES), :, :], sem).wait()


def _dispatch(hx, dest, pad_lo, pad_hi, nvalid, n_rows, tm):
    N, W = hx.shape
    grid_spec = pltpu.PrefetchScalarGridSpec(
        num_scalar_prefetch=3,
        grid=(N // tm,),
        in_specs=[pl.BlockSpec((1, 1, tm), lambda i, lo, hi, nv: (i, 0, 0), memory_space=pltpu.SMEM),
                  pl.BlockSpec((tm // SUBLANES, SUBLANES, W), lambda i, lo, hi, nv: (i, 0, 0))],
        out_specs=pl.BlockSpec(memory_space=pl.ANY),
        scratch_shapes=[pltpu.VMEM((MOE_ROWS // SUBLANES, SUBLANES, W), F32), pltpu.SemaphoreType.DMA(()),
                        pltpu.SemaphoreType.DMA(())],
    )
    return pl.pallas_call(
        functools.partial(_dispatch_kernel, tm=tm, nblk=n_rows // MOE_ROWS),
        grid_spec=grid_spec,
        out_shape=jax.ShapeDtypeStruct((n_rows // SUBLANES, SUBLANES, W), F32),
        compiler_params=_cparams(("arbitrary",)),
        name="moe_dispatch",
    )(pad_lo, pad_hi, nvalid, dest.reshape(N // tm, 1, tm),
      hx.reshape(N // SUBLANES, SUBLANES, W)).reshape(n_rows, W)


def _expert_kernel(blk_a_ref, blk_b_ref, nvalid_ref, x_ref, w1a_ref, w3a_ref, w2a_ref, w1b_ref, w3b_ref,
                   w2b_ref, y_ref, up_scr, dn_scr):
    D = y_ref.shape[1]
    step = pl.program_id(0)
    prev = jnp.maximum(step - 1, 0)

    @pl.when((step == 0) | (blk_a_ref[step] != blk_a_ref[prev]))
    def _():
        up_scr[0] = w1a_ref[0].astype(BF16)
        up_scr[1] = w3a_ref[0].astype(BF16)
        dn_scr[0] = w2a_ref[0].astype(BF16)

    @pl.when((step == 0) | (blk_b_ref[step] != blk_b_ref[prev]))
    def _():
        up_scr[2] = w1b_ref[0].astype(BF16)
        up_scr[3] = w3b_ref[0].astype(BF16)
        dn_scr[1] = w2b_ref[0].astype(BF16)

    @pl.when(step < nvalid_ref[0])
    def _():
        xb = x_ref[:, :D].astype(BF16)
        wa = x_ref[:, D + 2:D + 3]
        wb = x_ref[:, D + 3:D + 4]
        up = [_dot(xb, up_scr[j]) for j in range(4)]
        hid = [(up[2 * e] * _sigmoid(up[2 * e]) * up[2 * e + 1]).astype(BF16) for e in range(2)]
        y_ref[...] = wa * _dot(hid[0], dn_scr[0]) + wb * _dot(hid[1], dn_scr[1])

    @pl.when(pl.program_id(0) >= nvalid_ref[0])
    def _():
        y_ref[...] = jnp.zeros_like(y_ref)


def _experts(xs, blk_a, blk_b, nvalid, w1, w3, w2):
    P, W = xs.shape
    D = W - ROUTE_W
    nblk = P // MOE_ROWS
    F = w1.shape[-1]
    up_a = pl.BlockSpec((1, D, F), lambda i, ba, bb, nv: (ba[i], 0, 0))
    up_b = pl.BlockSpec((1, D, F), lambda i, ba, bb, nv: (bb[i], 0, 0))
    grid_spec = pltpu.PrefetchScalarGridSpec(
        num_scalar_prefetch=3,
        grid=(nblk,),
        in_specs=[pl.BlockSpec((MOE_ROWS, W),
                               lambda i, ba, bb, nv: (jnp.maximum(jnp.minimum(i, nv[0] - 1), 0), 0)),
                  up_a, up_a, pl.BlockSpec((1, F, D), lambda i, ba, bb, nv: (ba[i], 0, 0)),
                  up_b, up_b, pl.BlockSpec((1, F, D), lambda i, ba, bb, nv: (bb[i], 0, 0))],
        out_specs=pl.BlockSpec((MOE_ROWS, D), lambda i, ba, bb, nv: (i, 0)),
        scratch_shapes=[pltpu.VMEM((4, D, F), BF16), pltpu.VMEM((2, F, D), BF16)],
    )
    return pl.pallas_call(
        _expert_kernel,
        grid_spec=grid_spec,
        out_shape=jax.ShapeDtypeStruct((P, D), F32),
        compiler_params=_cparams(("arbitrary",)),
        name="moe_experts",
    )(blk_a, blk_b, nvalid, xs, w1, w3, w2, w1, w3, w2)


def _combine_kernel(dest_ref, dest_next_ref, x_ref, g_ref, ys_ref, o_ref, buf, sem, *, tm, nsteps):
    step = pl.program_id(0)
    slot = step % 2
    groups = tm // SUBLANES

    def request(d_ref, s):
        def body(i, carry):
            for u in range(SUBLANES):
                pltpu.make_async_copy(_tile_row(ys_ref, d_ref[0, 0, i * SUBLANES + u]),
                                      buf.at[s, i, pl.ds(u, 1), :], sem.at[s]).start()
            return carry
        lax.fori_loop(0, groups, body, 0)

    @pl.when(step == 0)
    def _():
        request(dest_ref, 0)

    @pl.when(step + 1 < nsteps)
    def _():
        request(dest_next_ref, 1 - slot)

    pltpu.make_async_copy(ys_ref.at[pl.ds(0, groups), :, :], buf.at[slot], sem.at[slot]).wait()
    o_ref[...] = _rms(x_ref[...] + buf[slot].reshape(o_ref.shape), g_ref[...])


def _combine(x2, dest, ys, g_final, tm):
    N, D = x2.shape
    nsteps = N // tm
    dest3 = dest.reshape(nsteps, 1, tm)
    return pl.pallas_call(
        functools.partial(_combine_kernel, tm=tm, nsteps=nsteps),
        grid=(nsteps,),
        in_specs=[pl.BlockSpec((1, 1, tm), lambda i: (i, 0, 0), memory_space=pltpu.SMEM),
                  pl.BlockSpec((1, 1, tm), lambda i: (jnp.minimum(i + 1, nsteps - 1), 0, 0),
                               memory_space=pltpu.SMEM),
                  pl.BlockSpec((tm, D), lambda i: (i, 0)),
                  pl.BlockSpec((1, D), lambda i: (0, 0)),
                  pl.BlockSpec(memory_space=pl.ANY)],
        out_specs=pl.BlockSpec((tm, D), lambda i: (i, 0)),
        out_shape=jax.ShapeDtypeStruct((N, D), F32),
        scratch_shapes=[pltpu.VMEM((2, tm // SUBLANES, SUBLANES, D), F32),
                        pltpu.SemaphoreType.DMA((2,))],
        compiler_params=_cparams(("arbitrary",)),
        name="moe_combine",
    )(dest3, dest3, x2, g_final.reshape(1, D), ys.reshape(ys.shape[0] // SUBLANES, SUBLANES, D))


def kernel(x, mem, positions, g_mix, w_in, conv_w, conv_b, w_q_m, w_k_m, b_i, b_f, g_mhn, skip_m, w_out, g_cross, g_mem, w_q_x, w_kv_x, w_o_x, g_ffn, w_router_g, b_router_g, w_router_e, b_router_e, w1, w3, w2, g_final):
    B, S, D = x.shape
    N = B * S
    depth = g_mix.shape[0]
    tm_in = 512
    tm_post = 512
    tm_dispatch = 2048
    tm_combine = 1024
    assert all(window // d == ATT_BLOCK and CLASSES % d == 0 for window, d in DILATED_CONFIGS)
    assert B % M_BATCH == 0 and S % tm_in == 0 and S % tm_post == 0
    assert N % tm_dispatch == 0 and N % tm_combine == 0
    assert depth == 1
    for l in range(depth):
        x2d = x.reshape(N, D)
        pos = positions.astype(F32).reshape(N, 1)
        k_mem, v_mem = _mem_kv(mem, g_mem[l], w_kv_x[l])
        q, k, v, qc, kc, vc, mu, mvt, mo, gtt = _in_proj(
            x2d, pos, g_mix[l], w_in[l], b_i[l], b_f[l], tm_in, B)
        q, k, v = (t.reshape(B, 1, S, ATT_WIDTH) for t in (q, k, v))
        outs, lses = zip(*(_attention_config(*(qkv + (d,)))
                           for qkv, (_, d) in zip(((q, k, v), (qc, kc, vc), (qc, kc, vc)), DILATED_CONFIGS)))
        y_m = _mlstm(mu.reshape(B, S, M_WIDTH), mvt, mo.reshape(B, S, M_WIDTH), gtt, conv_w[l], conv_b[l],
                     w_q_m[l], w_k_m[l], g_mhn[l], skip_m[l]).reshape(N, M_WIDTH)
        x2, hx, route, cnt = _post_mix(x2d, outs, lses, y_m, w_out[l], g_cross[l], w_q_x[l], k_mem,
                                       v_mem, w_o_x[l], g_ffn[l], w_router_g[l], b_router_g[l],
                                       w_router_e[l], b_router_e[l], tm_post, B)

        rank = route[:, 4].astype(jnp.int32)
        bucket = route[:, 5].astype(jnp.int32)
        counts = cnt[0, :N_BUCKETS].astype(jnp.int32)
        padded = ((counts + MOE_ROWS - 1) // MOE_ROWS) * MOE_ROWS
        pends = jnp.cumsum(padded)
        pstarts = pends - padded
        onehot = bucket[:, None] == jnp.arange(N_BUCKETS, dtype=jnp.int32)
        dest = jnp.sum(jnp.where(onehot, pstarts, 0), axis=-1) + rank
        n_rows = N + N_BUCKETS * MOE_ROWS
        nblk = n_rows // MOE_ROWS
        blk_start = jnp.arange(nblk, dtype=jnp.int32) * MOE_ROWS
        blk_bucket = jnp.minimum(jnp.sum(pends[None, :] <= blk_start[:, None], axis=1), N_BUCKETS - 1)
        base = (np.arange(N_BUCKETS) // len(PAIRS)) * EXPERTS_PER_GROUP
        expert_a = jnp.asarray(base + np.array([p[0] for p in PAIRS] * N_GROUPS), jnp.int32)
        expert_b = jnp.asarray(base + np.array([p[1] for p in PAIRS] * N_GROUPS), jnp.int32)
        blk_a = jnp.take(expert_a, blk_bucket).astype(jnp.int32)
        blk_b = jnp.take(expert_b, blk_bucket).astype(jnp.int32)
        nvalid = (pends[-1] // MOE_ROWS).reshape(1).astype(jnp.int32)

        xs = _dispatch(hx, dest, (pstarts + counts).astype(jnp.int32), pends.astype(jnp.int32),
                       nvalid, n_rows, tm_dispatch)
        ys = _experts(xs, blk_a, blk_b, nvalid, w1[l], w3[l], w2[l])
        x = _combine(x2, dest, ys, g_final, tm_combine).reshape(B, S, D)
    return x
```

```python
import functools

import jax
import jax.numpy as jnp
import numpy as np
from jax import lax
from jax.experimental import pallas as pl
from jax.experimental.pallas import tpu as pltpu

F32 = jnp.float32
BF16 = jnp.bfloat16

EPS = 1e-6
LANES = 128
SUBLANES = 8
ATT_HEAD_DIM = 64
ATT_WIDTH = 512
DILATED_CONFIGS = ((128, 1), (512, 4), (2048, 16))
CLASSES = 16
ATT_BLOCK = 128
ATT_GROUP = 1
ROPE_THETA = 500000.0
ROPE_DIM = ATT_HEAD_DIM // 4
M_WIDTH = 512
M_HEADS = 4
M_HEAD_DIM = 128
CONV_WIDTH = 4
M_CHUNK = 128
M_BATCH = 4
X_HEADS = 4
X_HEAD_DIM = 64
X_WIDTH = X_HEADS * X_HEAD_DIM
N_GROUPS = 4
EXPERTS_PER_GROUP = 4
N_EXPERTS = 16
TOP_K = 2
EXPERT_FF = 512
PAIRS = tuple((a, b) for a in range(EXPERTS_PER_GROUP) for b in range(a + 1, EXPERTS_PER_GROUP))
N_BUCKETS = N_GROUPS * len(PAIRS)
MOE_ROWS = 512
ROUTE_W = LANES
POST_SPLIT = 1
POST_ROWS = 64
VMEM_LIMIT = 56 * 1024 * 1024

NEG_INF = float("-inf")


def _cparams(sem):
    return pltpu.CompilerParams(dimension_semantics=sem, vmem_limit_bytes=VMEM_LIMIT)


def _rms(x, g):
    return x * lax.rsqrt(jnp.mean(x * x, axis=-1, keepdims=True) + EPS) * g


def _dot(a, b):
    return jnp.dot(a, b, preferred_element_type=F32)


def _dot_nt(a, b):
    return lax.dot_general(a, b, (((1,), (1,)), ((), ())), preferred_element_type=F32)


def _dot_tn(a, b):
    return lax.dot_general(a, b, (((0,), (0,)), ((), ())), preferred_element_type=F32)


def _hi_lo(a):
    hi = a.astype(BF16)
    return hi, (a - hi.astype(F32)).astype(BF16)


def _log_sigmoid(x):
    return jnp.minimum(x, 0.0) - jnp.log(1.0 + jnp.exp(-jnp.abs(x)))


def _sigmoid(x):
    return 0.5 * jnp.tanh(0.5 * x) + 0.5


def _mem_kv_kernel(mem_ref, g_ref, w_ref, k_ref, v_ref):
    h = _rms(mem_ref[0], g_ref[...]).astype(BF16)
    kv = _dot(h, w_ref[...])
    k_ref[0] = kv[:, :X_WIDTH].astype(BF16)
    v_ref[0] = kv[:, X_WIDTH:].astype(BF16)


def _mem_kv(mem, g_mem, w_kv):
    B, M, D = mem.shape
    return pl.pallas_call(
        _mem_kv_kernel,
        grid=(B,),
        in_specs=[pl.BlockSpec((1, M, D), lambda b: (b, 0, 0)),
                  pl.BlockSpec((1, D), lambda b: (0, 0)),
                  pl.BlockSpec((D, 2 * X_WIDTH), lambda b: (0, 0))],
        out_specs=[pl.BlockSpec((1, M, X_WIDTH), lambda b: (b, 0, 0)),
                   pl.BlockSpec((1, M, X_WIDTH), lambda b: (b, 0, 0))],
        out_shape=[jax.ShapeDtypeStruct((B, M, X_WIDTH), BF16)] * 2,
        compiler_params=_cparams(("arbitrary",)),
        name="mem_kv",
    )(mem, g_mem.reshape(1, D), w_kv.astype(BF16))


def _in_proj_kernel(x_ref, pos_ref, g_ref, wqkv_ref, wm_ref, wmvt_ref, bgt_ref,
                    invf_ref, sgn_ref, spread_ref, perm_ref,
                    q_ref, k_ref, v_ref, qc_ref, kc_ref, vc_ref,
                    mu_ref, mvt_ref, mo_ref, gtt_ref, *, tm):
    hb = _rms(x_ref[...], g_ref[...]).astype(BF16)
    qkv = _dot(hb, wqkv_ref[...])
    slots = tm // pos_ref.shape[0]
    ang = pos_ref[...] * invf_ref[...]
    rows = pos_ref.shape[0]
    cs_hl = jnp.concatenate(_hi_lo(jnp.where(sgn_ref[0:1, :] > 0.0, jnp.sin(ang), jnp.cos(ang))), axis=0)
    spread = [_dot(cs_hl, spread_ref[i]) for i in range(slots)]
    spread = [sp[:rows] + sp[rows:] for sp in spread]
    cos = jnp.concatenate([sp[:, :LANES] for sp in spread], axis=0) + sgn_ref[1:2, :]
    sin = jnp.concatenate([sp[:, LANES:] for sp in spread], axis=0)
    half = ROPE_DIM // 2
    first_half = sgn_ref[2:3, :] > 0.0

    def emit_classes(nat_ref, cls_ref):
        rows = tm // CLASSES
        by_class = _dot(perm_ref[...], nat_ref[...])
        for r in range(CLASSES):
            cls_ref[0, r] = by_class[r * rows:(r + 1) * rows].astype(BF16)

    v_ref[...] = qkv[:, 2 * ATT_WIDTH:].astype(BF16)
    mm = _dot(hb, wm_ref[...])
    for which, refs in ((0, (q_ref, qc_ref)), (1, (k_ref, kc_ref))):
        for g in range(ATT_WIDTH // LANES):
            cols = slice(g * LANES, (g + 1) * LANES)
            t = qkv[:, which * ATT_WIDTH + g * LANES: which * ATT_WIDTH + (g + 1) * LANES]
            partner = jnp.where(first_half, pltpu.roll(t, LANES - half, 1), pltpu.roll(t, half, 1))
            refs[0][:, cols] = (t * cos + partner * sin).astype(BF16)
    mu_ref[...] = mm[:, :M_WIDTH].astype(BF16)
    mo_ref[...] = mm[:, M_WIDTH:].astype(BF16)
    mvt = _dot_nt(wmvt_ref[...], hb)
    emit_classes(v_ref, vc_ref)
    for c in range(tm // M_CHUNK):
        mvt_ref[c] = mvt[:M_WIDTH, c * M_CHUNK:(c + 1) * M_CHUNK].astype(BF16)
    emit_classes(q_ref, qc_ref)
    gtt_ref[0] = mvt[M_WIDTH:M_WIDTH + 2 * M_HEADS, :] + bgt_ref[...]
    emit_classes(k_ref, kc_ref)


def _in_proj(x2d, pos, g_mix, w_in, b_i, b_f, tm, B):
    N, D = x2d.shape
    S = N // B
    tps = S // tm
    A = ATT_WIDTH
    wq = w_in[:, :A] * (ATT_HEAD_DIM ** -0.5)
    wqkv = jnp.concatenate([wq, w_in[:, A:3 * A]], axis=1).astype(BF16)
    o = 3 * A
    wm = jnp.concatenate([w_in[:, o:o + M_WIDTH], w_in[:, o + 2 * M_WIDTH:o + 3 * M_WIDTH]],
                         axis=1).astype(BF16)
    wgates = w_in[:, 3 * A + 3 * M_WIDTH:]
    wmvt = jnp.pad(jnp.concatenate([w_in[:, o + M_WIDTH:o + 2 * M_WIDTH], wgates], axis=1).T,
                   ((0, 2 * SUBLANES - 2 * M_HEADS), (0, 0))).astype(BF16)
    bgt = jnp.concatenate([b_i, b_f]).astype(F32).reshape(2 * M_HEADS, 1)
    half = ROPE_DIM // 2
    slots = LANES // ROPE_DIM
    lane = np.arange(LANES)
    c = lane % ROPE_DIM
    j = lane % ATT_HEAD_DIM
    inv_freq = ROPE_THETA ** (-jnp.arange(0, ROPE_DIM, 2, dtype=F32) / ROPE_DIM)
    invf = inv_freq[c % half].reshape(1, LANES).astype(F32)
    sgn = np.zeros((8, LANES), np.float32)
    sgn[0] = c >= half
    sgn[1] = j >= ROPE_DIM
    sgn[2] = j < half
    spread = np.zeros((slots, LANES, 2 * LANES), np.float32)
    for i in range(slots):
        for f in range(half):
            spread[i, ROPE_DIM * i + f, lane[(j < ROPE_DIM) & (j % half == f)]] = 1.0
            spread[i, ROPE_DIM * i + half + f, LANES + lane[j == f]] = -1.0
            spread[i, ROPE_DIM * i + half + f, LANES + lane[j == half + f]] = 1.0
    posp = pos.reshape(N // tm, slots, tm // slots).transpose(0, 2, 1)
    posp = jnp.repeat(posp, ROPE_DIM, axis=2).reshape(N // slots, LANES)
    tok = np.arange(tm)
    perm = np.zeros((tm, tm), np.float32)
    perm[(tok % CLASSES) * (tm // CLASSES) + tok // CLASSES, tok] = 1.0
    const = lambda shape: pl.BlockSpec(shape, lambda i: (0,) * len(shape))
    row = lambda w: pl.BlockSpec((tm, w), lambda i: (i, 0))
    cls = pl.BlockSpec((1, CLASSES, tm // CLASSES, A), lambda i: (i // tps, 0, i % tps, 0))
    cls_shape = jax.ShapeDtypeStruct((B, CLASSES, S // CLASSES, A), BF16)
    return pl.pallas_call(
        functools.partial(_in_proj_kernel, tm=tm),
        grid=(N // tm,),
        in_specs=[row(D), pl.BlockSpec((tm // slots, LANES), lambda i: (i, 0)), const((1, D)),
                  const((D, 3 * A)), const((D, 2 * M_WIDTH)),
                  const((M_WIDTH + 2 * SUBLANES, D)), const((2 * M_HEADS, 1)),
                  const((1, LANES)), const((8, LANES)), const((slots, LANES, 2 * LANES)),
                  const((tm, tm))],
        out_specs=[row(A), row(A), row(A), cls, cls, cls, row(M_WIDTH),
                   pl.BlockSpec((tm // M_CHUNK, M_WIDTH, M_CHUNK), lambda i: (i, 0, 0)), row(M_WIDTH),
                   pl.BlockSpec((1, 2 * M_HEADS, tm), lambda i: (i // tps, 0, i % tps))],
        out_shape=[jax.ShapeDtypeStruct((N, A), BF16)] * 3 + [cls_shape] * 3
        + [jax.ShapeDtypeStruct((N, M_WIDTH), BF16),
           jax.ShapeDtypeStruct((N // M_CHUNK, M_WIDTH, M_CHUNK), BF16),
           jax.ShapeDtypeStruct((N, M_WIDTH), BF16),
           jax.ShapeDtypeStruct((B, 2 * M_HEADS, S), F32)],
        compiler_params=_cparams(("arbitrary",)),
        name="in_proj",
    )(x2d, posp, g_mix.reshape(1, D), wqkv, wm, wmvt, bgt, invf, jnp.asarray(sgn),
      jnp.asarray(spread, BF16), jnp.asarray(perm, BF16))


def _attn_kernel(q_ref, kc_ref, kp_ref, vc_ref, vp_ref, o_ref, l_ref, kbuf, vbuf, bias_scr, *, qb, nc):
    blk = ATT_BLOCK
    piece = blk // nc
    nsub = qb // blk
    first = pl.program_id(2) == 0
    pair_low = (lax.broadcasted_iota(jnp.int32, (1, ATT_WIDTH), 1) & (LANES - 1)) < ATT_HEAD_DIM

    def put_v(dst, v):
        zero = jnp.zeros_like(v)
        vbuf[0, dst, :] = jnp.where(pair_low, v, zero)
        vbuf[1, dst, :] = jnp.where(pair_low, zero, v)

    for c in range(nc):
        kbuf[c * piece:(c + 1) * piece, :] = kp_ref[c]
        put_v(slice(c * piece, (c + 1) * piece), vp_ref[c])
        for sub in range(nsub):
            dst = slice((sub + 1) * blk + c * piece, (sub + 1) * blk + (c + 1) * piece)
            src = slice(sub * piece, (sub + 1) * piece)
            kbuf[dst, :] = kc_ref[c, src, :]
            put_v(dst, vc_ref[c, src, :])

    def pos(p):
        p = p & (blk - 1)
        return nc * (p & (piece - 1)) + (p >> (piece.bit_length() - 1))

    qi = lax.broadcasted_iota(jnp.int32, (blk, 2 * blk), 0)
    ki = lax.broadcasted_iota(jnp.int32, (blk, 2 * blk), 1)
    dist = pos(qi) - pos(ki) + jnp.where(ki < blk, blk, 0)
    band = (dist >= 0) & (dist <= blk)
    band_first = band & ((ki >= blk) | jnp.logical_not(first))
    bias_scr[0] = jnp.where(band_first, 0.0, NEG_INF)
    bias_scr[1] = jnp.where(band, 0.0, NEG_INF)
    lane = lax.broadcasted_iota(jnp.int32, (1, LANES), 1)
    lane_full = lax.broadcasted_iota(jnp.int32, (blk, LANES), 1)
    low = lane < ATT_HEAD_DIM
    nh = 2 * (ATT_WIDTH // LANES)
    for sub0 in range(0, nsub, ATT_GROUP):
        subs = range(sub0, min(sub0 + ATT_GROUP, nsub))
        chains = [(sub, h) for sub in subs for h in range(nh)]
        ids = range(len(chains))
        col = [slice((h // 2) * LANES, (h // 2 + 1) * LANES) for _, h in chains]
        prow = [slice(sub * piece, (sub + 1) * piece) for sub, _ in chains]
        krow = [slice(sub * blk, (sub + 2) * blk) for sub, _ in chains]
        qs = [jnp.concatenate([q_ref[c, prow[i], col[i]] for c in range(nc)], axis=0) for i in ids]
        qs = [jnp.where(low if h % 2 == 0 else jnp.logical_not(low), qs[i], jnp.zeros_like(qs[i]))
              for i, (_, h) in enumerate(chains)]
        s = [_dot_nt(qs[i], kbuf[krow[i], col[i]]) + bias_scr[min(sub, 1)]
             for i, (sub, _) in enumerate(chains)]
        m = [jnp.max(s[i], axis=-1, keepdims=True) for i in ids]
        p = [jnp.exp(s[i] - m[i]) for i in ids]
        l = [jnp.sum(p[i], axis=-1, keepdims=True) for i in ids]
        pv = [_dot(p[i].astype(BF16), vbuf[h % 2, krow[i], col[i]]) * (1.0 / l[i])
              for i, (_, h) in enumerate(chains)]
        for n, sub in enumerate(subs):
            base = n * nh
            m_all = jnp.zeros((blk, LANES), F32)
            l_all = jnp.ones((blk, LANES), F32)
            for h in range(nh):
                m_all = jnp.where(lane_full == h, m[base + h], m_all)
                l_all = jnp.where(lane_full == h, l[base + h], l_all)
            for g in range(ATT_WIDTH // LANES):
                acc = (pv[base + 2 * g] + pv[base + 2 * g + 1]).astype(BF16)
                for c in range(nc):
                    o_ref[c, prow[base], col[base + 2 * g]] = acc[c * piece:(c + 1) * piece]
            lse_all = m_all + jnp.log(l_all)
            for c in range(nc):
                l_ref[c, prow[base], :] = lse_all[c * piece:(c + 1) * piece]


def _attention_config(q, k, v, d):
    B, C, L, W = q.shape
    nc = C // d
    qb = min(512, L * nc)
    rows = qb // nc
    piece = ATT_BLOCK // nc
    nsub = qb // ATT_BLOCK
    view = lambda t: t.reshape(B, nc, d, L, t.shape[-1])
    cur = lambda w: pl.BlockSpec((None, nc, None, rows, w), lambda b, r, j: (b, 0, r, j, 0))
    prev = pl.BlockSpec((None, nc, None, piece, W),
                        lambda b, r, j: (b, 0, r, jnp.maximum(j * nsub - 1, 0), 0))
    o, lse = pl.pallas_call(
        functools.partial(_attn_kernel, qb=qb, nc=nc),
        grid=(B, d, L // rows),
        in_specs=[cur(W), cur(W), prev, cur(W), prev],
        out_specs=[cur(W), cur(LANES)],
        out_shape=[jax.ShapeDtypeStruct((B, nc, d, L, W), BF16),
                   jax.ShapeDtypeStruct((B, nc, d, L, LANES), F32)],
        scratch_shapes=[pltpu.VMEM((qb + ATT_BLOCK, W), BF16), pltpu.VMEM((2, qb + ATT_BLOCK, W), BF16),
                        pltpu.VMEM((2, ATT_BLOCK, 2 * ATT_BLOCK), F32)],
        compiler_params=_cparams(("arbitrary", "arbitrary", "arbitrary")),
        name=f"attention_d{d}",
    )(view(q), view(k), view(k), view(v), view(v))
    return o.reshape(B, C, L, W), lse.reshape(B, C, L, LANES)


def _mlstm_kernel(mu_ref, mvt_ref, mo_ref, gtt_ref, cw_ref, cb_ref, wqk_ref, gn_ref,
                  sk_ref, y_ref, c_scr, n_scr, m_scr, ext_scr):
    L = M_CHUNK
    H = M_HEADS
    HD = M_HEAD_DIM

    @pl.when(pl.program_id(1) == 0)
    def _():
        c_scr[...] = jnp.zeros_like(c_scr)
        n_scr[...] = jnp.zeros_like(n_scr)
        m_scr[...] = jnp.zeros_like(m_scr)
        for bb in range(M_BATCH):
            ext_scr[bb, 0:8, :] = jnp.zeros((8, M_WIDTH), F32)

    ri = lax.broadcasted_iota(jnp.int32, (L, L), 0)
    ci = lax.broadcasted_iota(jnp.int32, (L, L), 1)
    causal_t = ri <= ci
    triu = jnp.where(causal_t, 1.0, 0.0).astype(BF16)
    scale = HD ** -0.5

    bbs = range(M_BATCH)
    cs, gtt, b_rows, key_cols = [], [], [], []
    for bb in bbs:
        mu = mu_ref[bb].astype(F32)
        ext_scr[bb, 8:8 + L, :] = mu
        conv = cb_ref[...]
        for jj in range(CONV_WIDTH):
            lo = 8 - (CONV_WIDTH - 1) + jj
            conv = conv + ext_scr[bb, lo:lo + L, :] * cw_ref[jj:jj + 1, :]
        ext_scr[bb, 0:8, :] = mu[L - 8:, :]
        cs.append(conv * _sigmoid(conv))
    cb16 = [cs[bb].astype(BF16) for bb in bbs]
    for bb in bbs:
        gtt.append(gtt_ref[bb])
        hi_r, lo_r = _hi_lo(_log_sigmoid(gtt[bb]))
        b_rows.append(_dot(hi_r, triu) + _dot(lo_r, triu))
        key_rows = gtt[bb] - pltpu.roll(b_rows[bb], H, 0)
        key_cols.append(jnp.concatenate([key_rows, jnp.zeros((L - 2 * H, L), F32)], axis=0).T)

    prs = [(bb, hd) for bb in bbs for hd in range(H)]
    ids = range(len(prs))
    col = [slice(hd * HD, (hd + 1) * HD) for _, hd in prs]
    qk = [_dot(cb16[bb][:, col[i]], wqk_ref[hd]) for i, (bb, hd) in enumerate(prs)]
    qb = [qk[i][:, :HD].astype(BF16) for i in ids]
    kb = [(qk[i][:, HD:] * scale).astype(BF16) for i in ids]
    vt = [mvt_ref[bb, col[i], :] for i, (bb, _) in enumerate(prs)]
    b_r = [b_rows[bb][H + hd:H + hd + 1, :] for bb, hd in prs]
    b_last = [b_r[i][:, L - 1:L] for i in ids]
    m_prev = [m_scr[i:i + 1, 0:1] for i in ids]
    ct_prev = [c_scr[i] for i in ids]
    n_prev = [n_scr[i] for i in ids]

    dlog = [jnp.where(causal_t, b_r[i] + key_cols[bb][:, hd:hd + 1], NEG_INF)
            for i, (bb, hd) in enumerate(prs)]
    m_inter = [b_r[i] + m_prev[i] for i in ids]
    m_t = [jnp.maximum(m_inter[i], jnp.max(dlog[i], axis=0, keepdims=True)) for i in ids]
    inter_w = [jnp.exp(m_inter[i] - m_t[i]) for i in ids]
    st_w = [_dot_nt(kb[i], qb[i]) * jnp.exp(dlog[i] - m_t[i]) for i in ids]
    num = [_dot(vt[i], st_w[i].astype(BF16)) + inter_w[i] * _dot_nt(ct_prev[i].astype(BF16), qb[i])
           for i in ids]
    den = [jnp.sum(st_w[i], axis=0, keepdims=True)
           + inter_w[i] * _dot_nt(n_prev[i].astype(BF16), qb[i])[0:1, :] for i in ids]
    ht = [num[i] * (1.0 / jnp.maximum(jnp.abs(den[i]), jnp.exp(-m_t[i]))) for i in ids]
    ht = [ht[i] * lax.rsqrt(jnp.mean(ht[i] * ht[i], axis=0, keepdims=True) + EPS) for i in ids]
    for i, (bb, _) in enumerate(prs):
        hn = ht[i].T * gn_ref[:, col[i]]
        y = _sigmoid(mo_ref[bb, :, col[i]].astype(F32)) * (hn + sk_ref[:, col[i]] * cs[bb][:, col[i]])
        y_ref[bb, :, col[i]] = y.astype(BF16)

    g_r = [b_last[i] - b_r[i] + gtt[bb][hd:hd + 1, :] for i, (bb, hd) in enumerate(prs)]
    m_loc = [jnp.max(g_r[i], axis=1, keepdims=True) for i in ids]
    wk_r = [jnp.exp(g_r[i] - m_loc[i]) for i in ids]
    c_loc = [_dot((vt[i].astype(F32) * wk_r[i]).astype(BF16), kb[i]) for i in ids]
    n_loc = [_dot(jnp.broadcast_to(wk_r[i], (8, L)).astype(BF16), kb[i]) for i in ids]
    for i in ids:
        m_new = jnp.maximum(b_last[i] + m_prev[i], m_loc[i])
        a = jnp.exp(b_last[i] + m_prev[i] - m_new)
        cc = jnp.exp(m_loc[i] - m_new)
        c_scr[i] = a * ct_prev[i] + cc * c_loc[i]
        n_scr[i] = a * n_prev[i] + cc * n_loc[i]
        m_scr[i:i + 1, :] = jnp.broadcast_to(m_new, (1, LANES))


def _mlstm(mu, mvt, mo, gtt, conv_w, conv_b, w_q_m, w_k_m, g_mhn, skip_m):
    B, S, W = mu.shape
    L = M_CHUNK
    nc = S // L
    nb = M_BATCH
    tok = pl.BlockSpec((nb, L, W), lambda b, c: (b, c, 0))
    const = lambda shape: pl.BlockSpec(shape, lambda b, c: (0,) * len(shape))
    wqk = jnp.concatenate([w_q_m, w_k_m], axis=-1).astype(BF16)
    return pl.pallas_call(
        _mlstm_kernel,
        grid=(B // nb, nc),
        in_specs=[tok, pl.BlockSpec((nb, None, W, L), lambda b, c: (b, c, 0, 0)), tok,
                  pl.BlockSpec((nb, 2 * M_HEADS, L), lambda b, c: (b, 0, c)),
                  const((CONV_WIDTH, W)), const((1, W)),
                  const((M_HEADS, M_HEAD_DIM, 2 * M_HEAD_DIM)), const((1, W)), const((1, W))],
        out_specs=tok,
        out_shape=jax.ShapeDtypeStruct((B, S, W), BF16),
        scratch_shapes=[pltpu.VMEM((nb * M_HEADS, M_HEAD_DIM, M_HEAD_DIM), F32),
                        pltpu.VMEM((nb * M_HEADS, 8, M_HEAD_DIM), F32),
                        pltpu.VMEM((nb * M_HEADS, LANES), F32),
                        pltpu.VMEM((nb, 8 + L, W), F32)],
        compiler_params=_cparams(("arbitrary", "arbitrary")),
        name="mlstm",
    )(mu, mvt.reshape(B, nc, W, L), mo, gtt, conv_w.astype(F32), conv_b.reshape(1, W).astype(F32), wqk,
      g_mhn.reshape(1, W).astype(F32), skip_m.reshape(1, W).astype(F32))


def _post_mix_kernel(x_ref, o1_ref, o4_ref, o16_ref, l1_ref, l4_ref, l16_ref, ym_ref, wo_ref, gc_ref,
                     wqx_ref, km_ref, vm_ref, wox_ref, gf_ref, wr_ref, br_ref, before_ref, unperm_ref,
                     x2_ref, hx_ref, route_ref, cnt_ref,
                     run_scr, l4_scr, l16_scr, ya_scr, ox_scr, hb_scr, hl_scr, *, tm):
    @pl.when(pl.program_id(0) == 0)
    def _():
        run_scr[...] = jnp.zeros_like(run_scr)

    rows = tm // CLASSES
    o4_tok = _dot(unperm_ref[...], o4_ref[...].reshape(tm, ATT_WIDTH))
    o16_tok = _dot(unperm_ref[...], o16_ref[...].reshape(tm, ATT_WIDTH))
    for src, dst in ((l4_ref, l4_scr), (l16_ref, l16_scr)):
        for r in range(CLASSES):
            dst[pl.ds(r, rows, stride=CLASSES), :] = src[r]

    lane1 = lax.broadcasted_iota(jnp.int32, (1, LANES), 1)
    low = lane1 < ATT_HEAD_DIM

    sub_rows = tm // POST_SPLIT
    parts = range(POST_SPLIT)
    row_sl = [slice(part * sub_rows, (part + 1) * sub_rows) for part in parts]

    def row_blocks(rs):
        return [slice(r, r + POST_ROWS) for r in range(rs.start, rs.stop, POST_ROWS)]

    def mix_stage(rs):
        l1, l2, l3 = l1_ref[rs, :], l4_scr[rs, :], l16_scr[rs, :]
        mx = jnp.maximum(jnp.maximum(l1, l2), l3)
        e1, e2, e3 = jnp.exp(l1 - mx), jnp.exp(l2 - mx), jnp.exp(l3 - mx)
        inv = 1.0 / (e1 + e2 + e3)
        wts = (e1 * inv, e2 * inv, e3 * inv)
        slabs = range(ATT_WIDTH // LANES)
        acol = [slice(g * LANES, (g + 1) * LANES) for g in slabs]
        wsl = [[jnp.where(low, wts[c][:, 2 * g:2 * g + 1], wts[c][:, 2 * g + 1:2 * g + 2])
                for c in range(3)] for g in slabs]
        ya = [wsl[g][0] * o1_ref[rs, acol[g]].astype(F32) + wsl[g][1] * o4_tok[rs, acol[g]]
              + wsl[g][2] * o16_tok[rs, acol[g]] for g in slabs]
        for g in slabs:
            ya_scr[rs, acol[g]] = ya[g].astype(BF16)
        mix = _dot(ya_scr[rs, :], wo_ref[:ATT_WIDTH, :]) + _dot(ym_ref[rs, :], wo_ref[ATT_WIDTH:, :])
        for k, rb in enumerate(row_blocks(rs)):
            x1 = x_ref[rb, :] + mix[k * POST_ROWS:(k + 1) * POST_ROWS, :]
            x2_ref[rb, :] = x1
            hb_scr[rb, :] = _rms(x1, gc_ref[...]).astype(BF16)

    def cross_stage(rs):
        qx = _dot(hb_scr[rs, :], wqx_ref[...]).astype(BF16)
        hds = range(X_HEADS)
        xcol = [slice((h // 2) * LANES, (h // 2 + 1) * LANES) for h in hds]
        hmask = [low if h % 2 == 0 else jnp.logical_not(low) for h in hds]
        qs = [jnp.where(hmask[h], qx[:, xcol[h]], jnp.zeros((sub_rows, LANES), BF16)) for h in hds]
        vs = [jnp.where(hmask[h], vm_ref[0, :, xcol[h]], jnp.zeros((km_ref.shape[1], LANES), BF16))
              for h in hds]
        s = [_dot_nt(qs[h], km_ref[0, :, xcol[h]]) for h in hds]
        m = [jnp.max(s[h], axis=-1, keepdims=True) for h in hds]
        p = [jnp.exp(s[h] - m[h]) for h in hds]
        l = [jnp.sum(p[h], axis=-1, keepdims=True) for h in hds]
        pv = [_dot(p[h].astype(BF16), vs[h]) * (1.0 / l[h]) for h in hds]
        for g in range(X_WIDTH // LANES):
            ox_scr[rs, xcol[2 * g]] = (pv[2 * g] + pv[2 * g + 1]).astype(BF16)
        upd = _dot(ox_scr[rs, :], wox_ref[...])
        for k, rb in enumerate(row_blocks(rs)):
            x2 = x2_ref[rb, :] + upd[k * POST_ROWS:(k + 1) * POST_ROWS, :]
            x2_ref[rb, :] = x2
            h3 = _rms(x2, gf_ref[...])
            hx_ref[rb, :x2_ref.shape[1]] = h3
            hb_scr[rb, :], hl_scr[rb, :] = _hi_lo(h3)

    def router_stage(rs):
        t = _dot(hb_scr[rs, :], wr_ref[...])
        return t[:, :LANES] + t[:, LANES:] + _dot(hl_scr[rs, :], wr_ref[:, :LANES]) + br_ref[...]

    for p in parts:
        mix_stage(row_sl[p])
    for p in parts:
        cross_stage(row_sl[p])
    picks = [_pick_experts(router_stage(row_sl[p])) for p in parts]
    ia, ib, wa, wb, bucket = (jnp.concatenate([pk[f] for pk in picks], axis=0) for f in range(5))
    lane = lax.broadcasted_iota(jnp.int32, (tm, LANES), 1).astype(F32)

    hit = lane == bucket
    cnt = jnp.where(hit, 1.0, 0.0)
    prefix = _dot(before_ref[...], cnt.astype(BF16)) + run_scr[0:1, :]
    rank = jnp.sum(jnp.where(hit, prefix, 0.0), axis=-1, keepdims=True)
    total = run_scr[0:1, :] + jnp.sum(cnt, axis=0, keepdims=True)
    run_scr[...] = jnp.broadcast_to(total, run_scr.shape)
    cnt_ref[...] = jnp.broadcast_to(total, cnt_ref.shape)

    fields = (ia - N_GROUPS, ib - N_GROUPS, wa, wb, rank, bucket)
    route = jnp.zeros((tm, LANES), F32)
    for idx, val in enumerate(fields):
        route = jnp.where(lane == float(idx), val, route)
    route_ref[...] = route
    hx_ref[:, x2_ref.shape[1]:] = route


def _pick_experts(logits):
    lane = lax.broadcasted_iota(jnp.int32, logits.shape, 1).astype(F32)
    far = float(LANES)
    gmask = lane < N_GROUPS
    gl = jnp.where(gmask, logits, NEG_INF)
    gmax = jnp.max(gl, axis=-1, keepdims=True)
    gidx = jnp.min(jnp.where(gl == gmax, lane, far), axis=-1, keepdims=True)
    gsum = jnp.sum(jnp.where(gmask, jnp.exp(gl - gmax), 0.0), axis=-1, keepdims=True)
    g_w = 1.0 / gsum
    lo_lane = N_GROUPS + gidx * EXPERTS_PER_GROUP
    emask = (lane >= lo_lane) & (lane < lo_lane + EXPERTS_PER_GROUP)
    el = jnp.where(emask, logits, NEG_INF)
    t1 = jnp.max(el, axis=-1, keepdims=True)
    i1 = jnp.min(jnp.where(el == t1, lane, far), axis=-1, keepdims=True)
    el2 = jnp.where(lane == i1, NEG_INF, el)
    t2 = jnp.max(el2, axis=-1, keepdims=True)
    i2 = jnp.min(jnp.where(el2 == t2, lane, far), axis=-1, keepdims=True)
    ee = jnp.exp(t2 - t1)
    w1 = g_w / (1.0 + ee)
    w2 = w1 * ee

    first = i1 < i2
    ia = jnp.minimum(i1, i2)
    ib = jnp.maximum(i1, i2)
    wa = jnp.where(first, w1, w2)
    wb = jnp.where(first, w2, w1)
    la = ia - lo_lane
    lb = ib - lo_lane
    pair = la * (EXPERTS_PER_GROUP - 1) - la * (la - 1.0) * 0.5 + (lb - la - 1.0)
    return ia, ib, wa, wb, gidx * len(PAIRS) + pair


def _post_mix(x2d, outs, lses, y_m, w_out, g_cross, w_q_x, k_mem, v_mem, w_o_x, g_ffn,
              w_router_g, b_router_g, w_router_e, b_router_e, tm, B):
    N, D = x2d.shape
    S = N // B
    tps = S // tm
    M = k_mem.shape[1]
    wr = jnp.pad(jnp.concatenate([w_router_g, w_router_e], axis=1).astype(F32),
                 ((0, 0), (0, LANES - N_GROUPS - N_EXPERTS)))
    wr_hi = wr.astype(BF16)
    wr_cat = jnp.concatenate([wr_hi, (wr - wr_hi.astype(F32)).astype(BF16)], axis=1)
    br = jnp.pad(jnp.concatenate([b_router_g, b_router_e]).astype(F32),
                 (0, LANES - N_GROUPS - N_EXPERTS)).reshape(1, LANES)
    const = lambda shape: pl.BlockSpec(shape, lambda i: (0,) * len(shape))
    row = lambda w: pl.BlockSpec((tm, w), lambda i: (i, 0))
    cls = lambda w: pl.BlockSpec((None, CLASSES, tm // CLASSES, w), lambda i: (i // tps, 0, i % tps, 0))
    memspec = pl.BlockSpec((1, M, X_WIDTH), lambda i: (i // tps, 0, 0))
    tok = np.arange(tm)
    unperm = np.zeros((tm, tm), np.float32)
    unperm[tok, (tok % CLASSES) * (tm // CLASSES) + tok // CLASSES] = 1.0
    return pl.pallas_call(
        functools.partial(_post_mix_kernel, tm=tm),
        grid=(N // tm,),
        in_specs=[row(D), row(ATT_WIDTH), cls(ATT_WIDTH), cls(ATT_WIDTH), row(LANES),
                  cls(LANES), cls(LANES), row(M_WIDTH), const((D, D)), const((1, D)),
                  const((D, X_WIDTH)), memspec, memspec, const((X_WIDTH, D)), const((1, D)),
                  const((D, 2 * LANES)), const((1, LANES)), const((tm, tm)), const((tm, tm))],
        out_specs=[row(D), row(D + ROUTE_W), row(LANES), const((8, LANES))],
        out_shape=[jax.ShapeDtypeStruct((N, D), F32), jax.ShapeDtypeStruct((N, D + ROUTE_W), F32),
                   jax.ShapeDtypeStruct((N, LANES), F32), jax.ShapeDtypeStruct((8, LANES), F32)],
        scratch_shapes=[pltpu.VMEM((8, LANES), F32), pltpu.VMEM((tm, LANES), F32),
                        pltpu.VMEM((tm, LANES), F32), pltpu.VMEM((tm, ATT_WIDTH), BF16),
                        pltpu.VMEM((tm, X_WIDTH), BF16), pltpu.VMEM((tm, D), BF16),
                        pltpu.VMEM((tm, D), BF16)],
        compiler_params=_cparams(("arbitrary",)),
        name="post_mix",
    )(x2d, outs[0].reshape(N, ATT_WIDTH), outs[1], outs[2], lses[0].reshape(N, LANES), lses[1], lses[2],
      y_m, w_out.astype(BF16), g_cross.reshape(1, D), (w_q_x * (X_HEAD_DIM ** -0.5)).astype(BF16),
      k_mem, v_mem, w_o_x.astype(BF16), g_ffn.reshape(1, D), wr_cat, br,
      jnp.asarray(np.tril(np.ones((tm, tm), np.float32), -1), BF16),
      jnp.asarray(unperm, BF16))


def _tile_row(ref, row):
    return ref.at[row >> 3, pl.ds(row & (SUBLANES - 1), 1), :]


def _dispatch_kernel(pad_lo_ref, pad_hi_ref, nvalid_ref, dest_ref, h_ref, xs_ref, zero_scr, sem, zsem,
                     *, tm, nblk):
    def copy(i, u):
        return pltpu.make_async_copy(h_ref.at[i, pl.ds(u, 1), :],
                                     _tile_row(xs_ref, dest_ref[0, 0, i * SUBLANES + u]), sem)

    def start(i, carry):
        for u in range(SUBLANES):
            copy(i, u).start(priority=u % 2)
        return carry

    lax.fori_loop(0, tm // SUBLANES, start, 0)

    @pl.when(pl.program_id(0) == 0)
    def _():
        zero_scr[...] = jnp.zeros_like(zero_scr)
        groups = MOE_ROWS // SUBLANES

        def pad_copy(r):
            return pltpu.make_async_copy(zero_scr.at[0, pl.ds(0, 1), :], _tile_row(xs_ref, r), zsem)

        def tail_copy(blk):
            return pltpu.make_async_copy(zero_scr, xs_ref.at[pl.ds(blk * groups, groups), :, :], zsem)

        for q in range(N_BUCKETS):
            lax.fori_loop(pad_lo_ref[q], pad_hi_ref[q], lambda r, c: (pad_copy(r).start(), c)[1], 0)
        lax.fori_loop(nvalid_ref[0], nblk, lambda blk, c: (tail_copy(blk).start(), c)[1], 0)
        for q in range(N_BUCKETS):
            lax.fori_loop(pad_lo_ref[q], pad_hi_ref[q], lambda r, c: (pad_copy(r).wait(), c)[1], 0)
        lax.fori_loop(nvalid_ref[0], nblk, lambda blk, c: (tail_copy(blk).wait(), c)[1], 0)

    pltpu.make_async_copy(h_ref, xs_ref.at[pl.ds(0, tm // SUBLANES), :, :], sem).wait()


def _dispatch(hx, dest, pad_lo, pad_hi, nvalid, n_rows, tm):
    N, W = hx.shape
    grid_spec = pltpu.PrefetchScalarGridSpec(
        num_scalar_prefetch=3,
        grid=(N // tm,),
        in_specs=[pl.BlockSpec((1, 1, tm), lambda i, lo, hi, nv: (i, 0, 0), memory_space=pltpu.SMEM),
                  pl.BlockSpec((tm // SUBLANES, SUBLANES, W), lambda i, lo, hi, nv: (i, 0, 0))],
        out_specs=pl.BlockSpec(memory_space=pl.ANY),
        scratch_shapes=[pltpu.VMEM((MOE_ROWS // SUBLANES, SUBLANES, W), F32), pltpu.SemaphoreType.DMA(()),
                        pltpu.SemaphoreType.DMA(())],
    )
    return pl.pallas_call(
        functools.partial(_dispatch_kernel, tm=tm, nblk=n_rows // MOE_ROWS),
        grid_spec=grid_spec,
        out_shape=jax.ShapeDtypeStruct((n_rows // SUBLANES, SUBLANES, W), F32),
        compiler_params=_cparams(("arbitrary",)),
        name="moe_dispatch",
    )(pad_lo, pad_hi, nvalid, dest.reshape(N // tm, 1, tm),
      hx.reshape(N // SUBLANES, SUBLANES, W)).reshape(n_rows, W)


def _expert_kernel(blk_a_ref, blk_b_ref, nvalid_ref, x_ref, w1a_ref, w3a_ref, w2a_ref, w1b_ref, w3b_ref,
                   w2b_ref, y_ref, up_scr, dn_scr):
    D = y_ref.shape[1]
    step = pl.program_id(0)
    prev = jnp.maximum(step - 1, 0)

    @pl.when((step == 0) | (blk_a_ref[step] != blk_a_ref[prev]))
    def _():
        up_scr[0] = w1a_ref[0].astype(BF16)
        up_scr[1] = w3a_ref[0].astype(BF16)
        dn_scr[0] = w2a_ref[0].astype(BF16)

    @pl.when((step == 0) | (blk_b_ref[step] != blk_b_ref[prev]))
    def _():
        up_scr[2] = w1b_ref[0].astype(BF16)
        up_scr[3] = w3b_ref[0].astype(BF16)
        dn_scr[1] = w2b_ref[0].astype(BF16)

    @pl.when(step < nvalid_ref[0])
    def _():
        xb = x_ref[:, :D].astype(BF16)
        wa = x_ref[:, D + 2:D + 3]
        wb = x_ref[:, D + 3:D + 4]
        up = [_dot(xb, up_scr[j]) for j in range(4)]
        hid = [(up[2 * e] * _sigmoid(up[2 * e]) * up[2 * e + 1]).astype(BF16) for e in range(2)]
        y_ref[...] = wa * _dot(hid[0], dn_scr[0]) + wb * _dot(hid[1], dn_scr[1])

    @pl.when(pl.program_id(0) >= nvalid_ref[0])
    def _():
        y_ref[...] = jnp.zeros_like(y_ref)


def _experts(xs, blk_a, blk_b, nvalid, w1, w3, w2):
    P, W = xs.shape
    D = W - ROUTE_W
    nblk = P // MOE_ROWS
    F = w1.shape[-1]
    up_a = pl.BlockSpec((1, D, F), lambda i, ba, bb, nv: (ba[i], 0, 0))
    up_b = pl.BlockSpec((1, D, F), lambda i, ba, bb, nv: (bb[i], 0, 0))
    grid_spec = pltpu.PrefetchScalarGridSpec(
        num_scalar_prefetch=3,
        grid=(nblk,),
        in_specs=[pl.BlockSpec((MOE_ROWS, W),
                               lambda i, ba, bb, nv: (jnp.maximum(jnp.minimum(i, nv[0] - 1), 0), 0)),
                  up_a, up_a, pl.BlockSpec((1, F, D), lambda i, ba, bb, nv: (ba[i], 0, 0)),
                  up_b, up_b, pl.BlockSpec((1, F, D), lambda i, ba, bb, nv: (bb[i], 0, 0))],
        out_specs=pl.BlockSpec((MOE_ROWS, D), lambda i, ba, bb, nv: (i, 0)),
        scratch_shapes=[pltpu.VMEM((4, D, F), BF16), pltpu.VMEM((2, F, D), BF16)],
    )
    return pl.pallas_call(
        _expert_kernel,
        grid_spec=grid_spec,
        out_shape=jax.ShapeDtypeStruct((P, D), F32),
        compiler_params=_cparams(("arbitrary",)),
        name="moe_experts",
    )(blk_a, blk_b, nvalid, xs, w1, w3, w2, w1, w3, w2)


def _combine_kernel(dest_ref, dest_next_ref, x_ref, g_ref, ys_ref, o_ref, buf, sem, *, tm, nsteps):
    step = pl.program_id(0)
    slot = step % 2
    groups = tm // SUBLANES

    def request(d_ref, s):
        def body(i, carry):
            for u in range(SUBLANES):
                pltpu.make_async_copy(_tile_row(ys_ref, d_ref[0, 0, i * SUBLANES + u]),
                                      buf.at[s, i, pl.ds(u, 1), :], sem.at[s]).start(priority=u % 2)
            return carry
        lax.fori_loop(0, groups, body, 0)

    @pl.when(step == 0)
    def _():
        request(dest_ref, 0)

    @pl.when(step + 1 < nsteps)
    def _():
        request(dest_next_ref, 1 - slot)

    pltpu.make_async_copy(ys_ref.at[pl.ds(0, groups), :, :], buf.at[slot], sem.at[slot]).wait()
    o_ref[...] = _rms(x_ref[...] + buf[slot].reshape(o_ref.shape), g_ref[...])


def _combine(x2, dest, ys, g_final, tm):
    N, D = x2.shape
    nsteps = N // tm
    dest3 = dest.reshape(nsteps, 1, tm)
    return pl.pallas_call(
        functools.partial(_combine_kernel, tm=tm, nsteps=nsteps),
        grid=(nsteps,),
        in_specs=[pl.BlockSpec((1, 1, tm), lambda i: (i, 0, 0), memory_space=pltpu.SMEM),
                  pl.BlockSpec((1, 1, tm), lambda i: (jnp.minimum(i + 1, nsteps - 1), 0, 0),
                               memory_space=pltpu.SMEM),
                  pl.BlockSpec((tm, D), lambda i: (i, 0)),
                  pl.BlockSpec((1, D), lambda i: (0, 0)),
                  pl.BlockSpec(memory_space=pl.ANY)],
        out_specs=pl.BlockSpec((tm, D), lambda i: (i, 0)),
        out_shape=jax.ShapeDtypeStruct((N, D), F32),
        scratch_shapes=[pltpu.VMEM((2, tm // SUBLANES, SUBLANES, D), F32),
                        pltpu.SemaphoreType.DMA((2,))],
        compiler_params=_cparams(("arbitrary",)),
        name="moe_combine",
    )(dest3, dest3, x2, g_final.reshape(1, D), ys.reshape(ys.shape[0] // SUBLANES, SUBLANES, D))


def kernel(x, mem, positions, g_mix, w_in, conv_w, conv_b, w_q_m, w_k_m, b_i, b_f, g_mhn, skip_m, w_out, g_cross, g_mem, w_q_x, w_kv_x, w_o_x, g_ffn, w_router_g, b_router_g, w_router_e, b_router_e, w1, w3, w2, g_final):
    B, S, D = x.shape
    N = B * S
    depth = g_mix.shape[0]
    tm_in = 512
    tm_post = 512
    tm_dispatch = 2048
    tm_combine = 1024
    assert all(window // d == ATT_BLOCK and CLASSES % d == 0 for window, d in DILATED_CONFIGS)
    assert B % M_BATCH == 0 and S % tm_in == 0 and S % tm_post == 0
    assert N % tm_dispatch == 0 and N % tm_combine == 0
    assert depth == 1
    for l in range(depth):
        x2d = x.reshape(N, D)
        pos = positions.astype(F32).reshape(N, 1)
        k_mem, v_mem = _mem_kv(mem, g_mem[l], w_kv_x[l])
        q, k, v, qc, kc, vc, mu, mvt, mo, gtt = _in_proj(
            x2d, pos, g_mix[l], w_in[l], b_i[l], b_f[l], tm_in, B)
        q, k, v = (t.reshape(B, 1, S, ATT_WIDTH) for t in (q, k, v))
        outs, lses = zip(*(_attention_config(*(qkv + (d,)))
                           for qkv, (_, d) in zip(((q, k, v), (qc, kc, vc), (qc, kc, vc)), DILATED_CONFIGS)))
        y_m = _mlstm(mu.reshape(B, S, M_WIDTH), mvt, mo.reshape(B, S, M_WIDTH), gtt, conv_w[l], conv_b[l],
                     w_q_m[l], w_k_m[l], g_mhn[l], skip_m[l]).reshape(N, M_WIDTH)
        x2, hx, route, cnt = _post_mix(x2d, outs, lses, y_m, w_out[l], g_cross[l], w_q_x[l], k_mem,
                                       v_mem, w_o_x[l], g_ffn[l], w_router_g[l], b_router_g[l],
                                       w_router_e[l], b_router_e[l], tm_post, B)

        rank = route[:, 4].astype(jnp.int32)
        bucket = route[:, 5].astype(jnp.int32)
        counts = cnt[0, :N_BUCKETS].astype(jnp.int32)
        padded = ((counts + MOE_ROWS - 1) // MOE_ROWS) * MOE_ROWS
        pends = jnp.cumsum(padded)
        pstarts = pends - padded
        onehot = bucket[:, None] == jnp.arange(N_BUCKETS, dtype=jnp.int32)
        dest = jnp.sum(jnp.where(onehot, pstarts, 0), axis=-1) + rank
        n_rows = N + N_BUCKETS * MOE_ROWS
        nblk = n_rows // MOE_ROWS
        blk_start = jnp.arange(nblk, dtype=jnp.int32) * MOE_ROWS
        blk_bucket = jnp.minimum(jnp.sum(pends[None, :] <= blk_start[:, None], axis=1), N_BUCKETS - 1)
        base = (np.arange(N_BUCKETS) // len(PAIRS)) * EXPERTS_PER_GROUP
        expert_a = jnp.asarray(base + np.array([p[0] for p in PAIRS] * N_GROUPS), jnp.int32)
        expert_b = jnp.asarray(base + np.array([p[1] for p in PAIRS] * N_GROUPS), jnp.int32)
        blk_a = jnp.take(expert_a, blk_bucket).astype(jnp.int32)
        blk_b = jnp.take(expert_b, blk_bucket).astype(jnp.int32)
        nvalid = (pends[-1] // MOE_ROWS).reshape(1).astype(jnp.int32)

        xs = _dispatch(hx, dest, (pstarts + counts).astype(jnp.int32), pends.astype(jnp.int32),
                       nvalid, n_rows, tm_dispatch)
        ys = _experts(xs, blk_a, blk_b, nvalid, w1[l], w3[l], w2[l])
        x = _combine(x2, dest, ys, g_final, tm_combine).reshape(B, S, D)
    return x
```
